```python
import math
import jax, jax.numpy as jnp
from jax import lax
import numpy as np

D_MODEL = 2048
BATCH = 4
SEQ = 4096
DEPTH = 4

RET_HEADS = 8
RET_DK = 128
RET_DV = 128
RET_WIDTH = RET_HEADS * RET_DV
RET_CHUNK = 128
ROPE_BASE = 10000.0
RNN_WIDTH = D_MODEL // 2
RNN_BLOCKS = 8
RNN_BLOCK = RNN_WIDTH // RNN_BLOCKS
CONV_WIDTH = 4
RG_C = 8.0
EVEN_SPLITS = [RET_HEADS * RET_DK, 2 * RET_HEADS * RET_DK,
               2 * RET_HEADS * RET_DK + RET_WIDTH,
               2 * RET_HEADS * RET_DK + 2 * RET_WIDTH,
               2 * RET_HEADS * RET_DK + 2 * RET_WIDTH + RNN_WIDTH]
EVEN_IN = 2 * RET_HEADS * RET_DK + 2 * RET_WIDTH + 2 * RNN_WIDTH
EVEN_MIX = RET_WIDTH + RNN_WIDTH

FOX_HEADS = 16
FOX_DH = D_MODEL // FOX_HEADS
FOX_QBLOCK = 128
ODD_IN = 3 * D_MODEL + FOX_HEADS

N_EXPERTS = 64
TOP_K = 8
D_EXPERT = 512
D_SHARED = 512
ROUTED_SCALE = 2.5
MOE_BLOCK = 128

DN_ALPHA = (2 * DEPTH) ** 0.25
DN_BETA = (8 * DEPTH) ** -0.25
LN_EPS = 1e-5
N_EVEN = (DEPTH + 1) // 2
N_ODD = DEPTH // 2

kernel_name = 'hybrid_retention_rglru_fox_moe_deepnorm'


def _layer_norm(x, g, b):
    xf = x.astype(jnp.float32)
    mu = jnp.mean(xf, axis=-1, keepdims=True)
    var = jnp.mean(jnp.square(xf - mu), axis=-1, keepdims=True)
    y = (xf - mu) * lax.rsqrt(var + LN_EPS) * g.astype(jnp.float32) + b.astype(jnp.float32)
    return y.astype(x.dtype)


def _rotary(t, pos):
    half = t.shape[-1] // 2
    inv = ROPE_BASE ** (-jnp.arange(half, dtype=jnp.float32) / half)
    ang = pos[:, None] * inv[None, :]
    cos = jnp.cos(ang)[None, :, None, :]
    sin = jnp.sin(ang)[None, :, None, :]
    t = t.astype(jnp.float32)
    t1, t2 = t[..., :half], t[..., half:]
    return jnp.concatenate([t1 * cos - t2 * sin, t2 * cos + t1 * sin], axis=-1)


def _retention(q, k, v):
    B, S, H, DK = q.shape
    DV = v.shape[-1]
    C = RET_CHUNK
    N = S // C
    log_g = jnp.log1p(-jnp.exp2(-5.0 - jnp.arange(H, dtype=jnp.float32)))
    i = jnp.arange(C, dtype=jnp.float32)
    rel = i[:, None] - i[None, :]
    intra = jnp.where(rel >= 0, jnp.exp(log_g[:, None, None] * jnp.maximum(rel, 0.0)), 0.0)
    k_dec = jnp.exp(log_g[:, None] * (C - 1.0 - i)[None, :])
    q_dec = jnp.exp(log_g[:, None] * (i + 1.0)[None, :])
    chunk_dec = jnp.exp(log_g * C)
    qc = q.reshape(B, N, C, H, DK)
    kc = k.reshape(B, N, C, H, DK)
    vc = v.reshape(B, N, C, H, DV)
    scores = jnp.einsum('bnihd,bnjhd->bnhij', qc, kc) * intra
    y_in = jnp.einsum('bnhij,bnjhe->bnihe', scores, vc)
    kv = jnp.einsum('bnjhd,hj,bnjhe->nbhde', kc, k_dec, vc)

    def step(state, kv_n):
        return state * chunk_dec[None, :, None, None] + kv_n, state

    _, prev = lax.scan(step, jnp.zeros((B, H, DK, DV), jnp.float32), kv)
    y_cross = jnp.einsum('bnihd,hi,nbhde->bnihe', qc, q_dec, prev)
    return (y_in + y_cross).reshape(B, S, H, DV)


def _head_group_norm(y, g):
    B, S, H, DV = y.shape
    mu = jnp.mean(y, axis=-1, keepdims=True)
    var = jnp.mean(jnp.square(y - mu), axis=-1, keepdims=True)
    yn = (y - mu) * lax.rsqrt(var + LN_EPS)
    return yn.reshape(B, S, H * DV) * g.astype(jnp.float32)


def _causal_depthwise_conv(u, w, b):
    C = u.shape[-1]
    out = lax.conv_general_dilated(u, w[:, None, :].astype(u.dtype), window_strides=(1,),
                                   padding=[(CONV_WIDTH - 1, 0)],
                                   dimension_numbers=('NWC', 'WIO', 'NWC'),
                                   feature_group_count=C)
    return out + b.astype(u.dtype)


def _rg_lru(u, w_a, b_a, w_i, b_i, lam):
    B, S, _ = u.shape
    ub = u.reshape(B, S, RNN_BLOCKS, RNN_BLOCK)
    r = jax.nn.sigmoid(jnp.einsum('bsnc,ncd->bsnd', ub, w_a).reshape(B, S, RNN_WIDTH).astype(jnp.float32)
                       + b_a.astype(jnp.float32))
    ig = jax.nn.sigmoid(jnp.einsum('bsnc,ncd->bsnd', ub, w_i).reshape(B, S, RNN_WIDTH).astype(jnp.float32)
                        + b_i.astype(jnp.float32))
    log_a = -RG_C * r * jax.nn.softplus(-lam.astype(jnp.float32))
    a = jnp.exp(log_a)
    bt = jnp.sqrt(-jnp.expm1(2.0 * log_a)) * (ig * u.astype(jnp.float32))

    def combine(left, right):
        a1, b1 = left
        a2, b2 = right
        return a1 * a2, a2 * b1 + b2

    _, h = lax.associative_scan(combine, (a, bt), axis=1)
    return h


def _even_mixer(x, w_in, gn_g, conv_w, conv_b, w_a, b_a, w_i, b_i, lam, w_out):
    B, S, _ = x.shape
    proj = x @ w_in
    q, k, v, g, u, gate = jnp.split(proj, EVEN_SPLITS, axis=-1)
    pos = jnp.arange(S, dtype=jnp.float32)
    q = _rotary(q.reshape(B, S, RET_HEADS, RET_DK), pos)
    k = _rotary(k.reshape(B, S, RET_HEADS, RET_DK), pos) * (RET_DK ** -0.5)
    v = v.reshape(B, S, RET_HEADS, RET_DV).astype(jnp.float32)
    y_ret = _head_group_norm(_retention(q, k, v), gn_g) * jax.nn.silu(g.astype(jnp.float32))
    u = _causal_depthwise_conv(u, conv_w, conv_b)
    h = _rg_lru(u, w_a, b_a, w_i, b_i, lam)
    y_rnn = h * jax.nn.gelu(gate.astype(jnp.float32))
    y_mix = jnp.concatenate([y_ret, y_rnn], axis=-1).astype(x.dtype)
    return y_mix @ w_out


def _fox_mixer(x, w_in, b_f, w_out):
    B, S, D = x.shape
    proj = x @ w_in
    q, k, v, fl = jnp.split(proj, [D, 2 * D, 3 * D], axis=-1)
    q = q.reshape(B, S, FOX_HEADS, FOX_DH)
    k = k.reshape(B, S, FOX_HEADS, FOX_DH)
    v = v.reshape(B, S, FOX_HEADS, FOX_DH)
    log_f = jax.nn.log_sigmoid(fl.astype(jnp.float32) + b_f.astype(jnp.float32))
    cum = jnp.transpose(jnp.cumsum(log_f, axis=1), (0, 2, 1))
    scale = FOX_DH ** -0.5
    outs = []
    for blk in range(S // FOX_QBLOCK):
        q0 = blk * FOX_QBLOCK
        q1 = q0 + FOX_QBLOCK
        s = jnp.einsum('bqhd,bkhd->bhqk', q[:, q0:q1], k[:, :q1],
                       preferred_element_type=jnp.float32) * scale
        s = s + cum[:, :, q0:q1, None] - cum[:, :, None, :q1]
        qi = jnp.arange(q0, q1)
        ki = jnp.arange(q1)
        s = jnp.where(ki[None, :] <= qi[:, None], s, -jnp.inf)
        p = jax.nn.softmax(s, axis=-1)
        outs.append(jnp.einsum('bhqk,bkhd->bqhd', p.astype(v.dtype), v[:, :q1]))
    o = jnp.concatenate(outs, axis=1).reshape(B, S, D)
    return o @ w_out


def _moe(x, router_w, router_bias, w_gate, w_up, w_down, sh_gate, sh_up, sh_down):
    B, S, D = x.shape
    T = B * S
    TK = T * TOP_K
    xf = x.reshape(T, D)
    scores = jax.nn.sigmoid((xf @ router_w).astype(jnp.float32))
    _, idx = lax.top_k(scores + router_bias.astype(jnp.float32), TOP_K)
    wts = jnp.take_along_axis(scores, idx, axis=1)
    wts = wts / jnp.sum(wts, axis=-1, keepdims=True) * ROUTED_SCALE
    flat_e = idx.reshape(-1).astype(jnp.int32)
    flat_tok = jnp.arange(TK, dtype=jnp.int32) // TOP_K
    flat_w = wts.reshape(-1)
    order = jnp.argsort(flat_e)
    se, stok, sw = flat_e[order], flat_tok[order], flat_w[order]
    counts = jnp.bincount(flat_e, length=N_EXPERTS).astype(jnp.int32)
    padded = (counts + MOE_BLOCK - 1) // MOE_BLOCK * MOE_BLOCK
    starts = jnp.cumsum(counts) - counts
    pends = jnp.cumsum(padded)
    pstarts = pends - padded
    dest = pstarts[se] + (jnp.arange(TK, dtype=jnp.int32) - starts[se])
    n_blocks = (TK + N_EXPERTS * (MOE_BLOCK - 1) + MOE_BLOCK - 1) // MOE_BLOCK
    P = n_blocks * MOE_BLOCK
    buf_tok = jnp.zeros((P,), jnp.int32).at[dest].set(stok)
    buf_w = jnp.zeros((P,), jnp.float32).at[dest].set(sw)
    block_e = jnp.minimum(jnp.searchsorted(pends, jnp.arange(n_blocks, dtype=jnp.int32) * MOE_BLOCK,
                                           side='right'), N_EXPERTS - 1)

    def body(acc, blk):
        e, tok, wt = blk
        xb = xf[tok]
        hb = jax.nn.silu(xb @ w_gate[e]) * (xb @ w_up[e])
        yb = (hb @ w_down[e]) * wt[:, None].astype(hb.dtype)
        return acc.at[tok].add(yb.astype(acc.dtype)), None

    routed, _ = lax.scan(body, jnp.zeros_like(xf),
                         (block_e, buf_tok.reshape(n_blocks, MOE_BLOCK), buf_w.reshape(n_blocks, MOE_BLOCK)))
    shared = (jax.nn.silu(xf @ sh_gate) * (xf @ sh_up)) @ sh_down
    return (routed + shared).reshape(B, S, D)


def setup_inputs(seed: int = 0) -> dict:
    key = jax.random.key(seed)
    ks = jax.random.split(key, 26)
    f32 = jnp.float32

    def nrm(k, shape, scale):
        return jax.random.normal(k, shape, f32) * scale

    x = nrm(ks[0], (BATCH, SEQ, D_MODEL), 1.0)
    even_w_in = nrm(ks[1], (N_EVEN, D_MODEL, EVEN_IN), D_MODEL ** -0.5)
    ret_gn_g = 1.0 + nrm(ks[2], (N_EVEN, RET_WIDTH), 0.02)
    rg_conv_w = nrm(ks[3], (N_EVEN, CONV_WIDTH, RNN_WIDTH), CONV_WIDTH ** -0.5)
    rg_conv_b = nrm(ks[4], (N_EVEN, RNN_WIDTH), 0.02)
    rg_w_a = nrm(ks[5], (N_EVEN, RNN_BLOCKS, RNN_BLOCK, RNN_BLOCK), RNN_BLOCK ** -0.5)
    rg_b_a = nrm(ks[6], (N_EVEN, RNN_WIDTH), 0.02)
    rg_w_i = nrm(ks[7], (N_EVEN, RNN_BLOCKS, RNN_BLOCK, RNN_BLOCK), RNN_BLOCK ** -0.5)
    rg_b_i = nrm(ks[8], (N_EVEN, RNN_WIDTH), 0.02)
    a_base = jax.random.uniform(ks[9], (N_EVEN, RNN_WIDTH), f32, minval=0.9, maxval=0.999)
    s_lam = a_base ** (1.0 / RG_C)
    rg_lambda = jnp.log(s_lam) - jnp.log1p(-s_lam)
    even_w_out = nrm(ks[10], (N_EVEN, EVEN_MIX, D_MODEL), EVEN_MIX ** -0.5 * DN_BETA)
    fox_w_in = nrm(ks[11], (N_ODD, D_MODEL, ODD_IN), D_MODEL ** -0.5)
    fox_b_f = jax.random.uniform(ks[12], (N_ODD, FOX_HEADS), f32, minval=1.0, maxval=6.0)
    fox_w_out = nrm(ks[13], (N_ODD, D_MODEL, D_MODEL), D_MODEL ** -0.5 * DN_BETA)
    ln_g = 1.0 + nrm(ks[14], (DEPTH, 2, D_MODEL), 0.02)
    ln_b = nrm(ks[15], (DEPTH, 2, D_MODEL), 0.02)
    router_w = nrm(ks[16], (DEPTH, D_MODEL, N_EXPERTS), D_MODEL ** -0.5)
    router_bias = nrm(ks[17], (DEPTH, N_EXPERTS), 0.01)
    exp_w_gate = nrm(ks[18], (DEPTH, N_EXPERTS, D_MODEL, D_EXPERT), D_MODEL ** -0.5)
    exp_w_up = nrm(ks[19], (DEPTH, N_EXPERTS, D_MODEL, D_EXPERT), D_MODEL ** -0.5)
    exp_w_down = nrm(ks[20], (DEPTH, N_EXPERTS, D_EXPERT, D_MODEL), D_EXPERT ** -0.5 * DN_BETA)
    sh_w_gate = nrm(ks[21], (DEPTH, D_MODEL, D_SHARED), D_MODEL ** -0.5)
    sh_w_up = nrm(ks[22], (DEPTH, D_MODEL, D_SHARED), D_MODEL ** -0.5)
    sh_w_down = nrm(ks[23], (DEPTH, D_SHARED, D_MODEL), D_SHARED ** -0.5 * DN_BETA)
    return {'x': x, 'even_w_in': even_w_in, 'ret_gn_g': ret_gn_g, 'rg_conv_w': rg_conv_w,
            'rg_conv_b': rg_conv_b, 'rg_w_a': rg_w_a, 'rg_b_a': rg_b_a, 'rg_w_i': rg_w_i,
            'rg_b_i': rg_b_i, 'rg_lambda': rg_lambda, 'even_w_out': even_w_out,
            'fox_w_in': fox_w_in, 'fox_b_f': fox_b_f, 'fox_w_out': fox_w_out,
            'ln_g': ln_g, 'ln_b': ln_b, 'router_w': router_w, 'router_bias': router_bias,
            'exp_w_gate': exp_w_gate, 'exp_w_up': exp_w_up, 'exp_w_down': exp_w_down,
            'sh_w_gate': sh_w_gate, 'sh_w_up': sh_w_up, 'sh_w_down': sh_w_down}


def reference(x, even_w_in, ret_gn_g, rg_conv_w, rg_conv_b, rg_w_a, rg_b_a, rg_w_i, rg_b_i,
              rg_lambda, even_w_out, fox_w_in, fox_b_f, fox_w_out, ln_g, ln_b, router_w,
              router_bias, exp_w_gate, exp_w_up, exp_w_down, sh_w_gate, sh_w_up, sh_w_down):
    h = x
    for layer in range(DEPTH):
        j = layer // 2
        if layer % 2 == 0:
            mix = _even_mixer(h, even_w_in[j], ret_gn_g[j], rg_conv_w[j], rg_conv_b[j],
                              rg_w_a[j], rg_b_a[j], rg_w_i[j], rg_b_i[j], rg_lambda[j],
                              even_w_out[j])
        else:
            mix = _fox_mixer(h, fox_w_in[j], fox_b_f[j], fox_w_out[j])
        h = _layer_norm(DN_ALPHA * h + mix, ln_g[layer, 0], ln_b[layer, 0])
        ffn = _moe(h, router_w[layer], router_bias[layer], exp_w_gate[layer], exp_w_up[layer],
                   exp_w_down[layer], sh_w_gate[layer], sh_w_up[layer], sh_w_down[layer])
        h = _layer_norm(DN_ALPHA * h + ffn, ln_g[layer, 1], ln_b[layer, 1])
    return h
```

```python
import functools
import math

import numpy as np
import jax
import jax.numpy as jnp
from jax import lax
from jax.experimental import pallas as pl
from jax.experimental.pallas import tpu as pltpu

F32 = jnp.float32
BF16 = jnp.bfloat16

HEAD_DIM = 128
TOP_K = 8
ROUTED_SCALE = 2.5
ROPE_BASE = 10000.0
RG_C = 8.0
LN_EPS = 1e-5
CONV_TAIL = 8

V7X_VMEM_BYTES = 64 * 1024 * 1024
VMEM_LIMIT = V7X_VMEM_BYTES - 8 * 1024 * 1024
LANES = 128

EXPERT_BLOCK = 256


def _cparams(grid_rank):
    return pltpu.CompilerParams(dimension_semantics=("arbitrary",) * grid_rank,
                                vmem_limit_bytes=VMEM_LIMIT)


def _dot(a, b):
    return jnp.dot(a, b, preferred_element_type=F32)


def _silu(x):
    return x * jax.nn.sigmoid(x)


def _softplus(x):
    return jnp.maximum(x, 0.0) + jnp.log1p(jnp.exp(-jnp.abs(x)))


def _layer_norm(z, g, b):
    mu = jnp.mean(z, axis=-1, keepdims=True)
    zc = z - mu
    var = jnp.mean(zc * zc, axis=-1, keepdims=True)
    return zc * lax.rsqrt(var + LN_EPS) * g + b


def _mm_kernel(x_ref, w_ref, o_ref):
    o_ref[...] = _dot(x_ref[...], w_ref[...]).astype(o_ref.dtype)


def _tile(n, target):
    best = LANES
    for c in range(LANES, min(n, target) + 1, LANES):
        if n % c == 0:
            best = c
    assert n % best == 0
    return best


def _matmul(x, w, out_dtype, tm=1024, tn=1024):
    m, k = x.shape
    n = w.shape[1]
    tm, tn = _tile(m, tm), _tile(n, tn)
    return pl.pallas_call(
        _mm_kernel,
        grid=(m // tm, n // tn),
        in_specs=[pl.BlockSpec((tm, k), lambda i, j: (i, 0)),
                  pl.BlockSpec((k, tn), lambda i, j: (0, j))],
        out_specs=pl.BlockSpec((tm, tn), lambda i, j: (i, j)),
        out_shape=jax.ShapeDtypeStruct((m, n), out_dtype),
        compiler_params=_cparams(2),
        name="in_proj",
    )(x, w)


def _proj_ln_kernel(*refs, n_in, alpha):
    ys, ws = refs[:n_in], refs[n_in:2 * n_in]
    h_ref, g_ref, b_ref, of_ref, ob_ref = refs[2 * n_in:]
    acc = _dot(ys[0][...], ws[0][...])
    for y_ref, w_ref in zip(ys[1:], ws[1:]):
        acc = acc + _dot(y_ref[...], w_ref[...])
    out = _layer_norm(alpha * h_ref[...] + acc, g_ref[...], b_ref[...])
    of_ref[...] = out
    ob_ref[...] = out.astype(BF16)


def _proj_ln(ys, ws, h, g, b, alpha, tm=256):
    t, d = h.shape
    tm = min(tm, t)
    n_in = len(ys)
    in_specs = ([pl.BlockSpec((tm, y.shape[1]), lambda i: (i, 0)) for y in ys]
                + [pl.BlockSpec(w.shape, lambda i: (0, 0)) for w in ws]
                + [pl.BlockSpec((tm, d), lambda i: (i, 0)),
                   pl.BlockSpec((1, d), lambda i: (0, 0)),
                   pl.BlockSpec((1, d), lambda i: (0, 0))])
    return pl.pallas_call(
        functools.partial(_proj_ln_kernel, n_in=n_in, alpha=alpha),
        grid=(t // tm,),
        in_specs=in_specs,
        out_specs=[pl.BlockSpec((tm, d), lambda i: (i, 0)),
                   pl.BlockSpec((tm, d), lambda i: (i, 0))],
        out_shape=[jax.ShapeDtypeStruct((t, d), F32), jax.ShapeDtypeStruct((t, d), BF16)],
        compiler_params=_cparams(1),
        name="out_proj_ln",
    )(*ys, *ws, h, g.reshape(1, d), b.reshape(1, d))


def _retention_kernel(q_ref, k_ref, v_ref, g_ref, cos_ref, sin_ref, intra_ref, qdec_ref, kdec_ref,
                      gn_ref, o_ref, state_ref, *, heads, chunk_decay):
    @pl.when(pl.program_id(1) == 0)
    def _():
        state_ref[...] = jnp.zeros_like(state_ref)

    cos, sin = cos_ref[...], sin_ref[...]
    half = HEAD_DIM // 2
    for h in range(heads):
        sl = slice(h * HEAD_DIM, (h + 1) * HEAD_DIM)
        q = q_ref[:, sl].astype(F32)
        k = k_ref[:, sl].astype(F32)
        q = q * cos + pltpu.roll(q, half, 1) * sin
        k = (k * cos + pltpu.roll(k, half, 1) * sin) * (HEAD_DIM ** -0.5)
        v = v_ref[:, sl]
        s = lax.dot_general(q.astype(BF16), k.astype(BF16), (((1,), (1,)), ((), ())),
                            preferred_element_type=F32) * intra_ref[h]
        state = state_ref[h]
        y = _dot(s.astype(BF16), v) + _dot((q * qdec_ref[h]).astype(BF16), state.astype(BF16))
        kv = lax.dot_general((k * kdec_ref[h]).astype(BF16), v, (((0,), (0,)), ((), ())),
                             preferred_element_type=F32)
        state_ref[h] = state * chunk_decay[h] + kv
        mu = jnp.mean(y, axis=-1, keepdims=True)
        yc = y - mu
        var = jnp.mean(yc * yc, axis=-1, keepdims=True)
        yn = yc * lax.rsqrt(var + LN_EPS) * gn_ref[:, sl]
        o_ref[:, sl] = (yn * _silu(g_ref[:, sl].astype(F32))).astype(BF16)


def _retention(proj, gn_g, batch, seq, chunk):
    t = proj.shape[0]
    width = gn_g.shape[0]
    heads = width // HEAD_DIM
    chunk = min(chunk, seq)
    n_chunks = seq // chunk
    half = HEAD_DIM // 2

    pos = jnp.arange(seq, dtype=F32)
    inv = ROPE_BASE ** (-jnp.arange(half, dtype=F32) / half)
    ang = pos[:, None] * inv[None, :]
    cos2 = jnp.concatenate([jnp.cos(ang), jnp.cos(ang)], axis=1)
    sin2 = jnp.concatenate([-jnp.sin(ang), jnp.sin(ang)], axis=1)

    log_g = jnp.log1p(-jnp.exp2(-5.0 - jnp.arange(heads, dtype=F32)))
    i = jnp.arange(chunk, dtype=F32)
    rel = i[:, None] - i[None, :]
    intra = jnp.where(rel >= 0, jnp.exp(log_g[:, None, None] * jnp.maximum(rel, 0.0)), 0.0)
    kdec = jnp.exp(log_g[:, None] * (chunk - 1.0 - i)[None, :])
    qdec = jnp.exp(log_g[:, None] * (i + 1.0)[None, :])
    kdec = jnp.broadcast_to(kdec[:, :, None], (heads, chunk, HEAD_DIM))
    qdec = jnp.broadcast_to(qdec[:, :, None], (heads, chunk, HEAD_DIM))
    chunk_decay = tuple(math.exp(math.log1p(-2.0 ** (-5.0 - h)) * chunk) for h in range(heads))

    def col(c):
        return pl.BlockSpec((chunk, width), lambda b, n: (b * n_chunks + n, c))

    const3 = lambda shape: pl.BlockSpec(shape, lambda b, n: (0, 0, 0))
    return pl.pallas_call(
        functools.partial(_retention_kernel, heads=heads, chunk_decay=chunk_decay),
        grid=(batch, n_chunks),
        in_specs=[col(0), col(1), col(2), col(3),
                  pl.BlockSpec((chunk, HEAD_DIM), lambda b, n: (n, 0)),
                  pl.BlockSpec((chunk, HEAD_DIM), lambda b, n: (n, 0)),
                  const3((heads, chunk, chunk)),
                  const3((heads, chunk, HEAD_DIM)),
                  const3((heads, chunk, HEAD_DIM)),
                  pl.BlockSpec((1, width), lambda b, n: (0, 0))],
        out_specs=pl.BlockSpec((chunk, width), lambda b, n: (b * n_chunks + n, 0)),
        out_shape=jax.ShapeDtypeStruct((t, width), BF16),
        scratch_shapes=[pltpu.VMEM((heads, HEAD_DIM, HEAD_DIM), F32)],
        compiler_params=_cparams(2),
        name="retention",
    )(proj, proj, proj, proj, cos2, sin2, intra, qdec, kdec, gn_g.reshape(1, width))


def _rglru_kernel(u_ref, gate_ref, cw_ref, cb_ref, wa_ref, ba_ref, wi_ref, bi_ref, lam_ref,
                  o_ref, uext_ref, hc_ref, *, tt, nblk):
    n = pl.program_id(1)

    @pl.when(n == 0)
    def _():
        uext_ref[0:CONV_TAIL, :] = jnp.zeros((CONV_TAIL, uext_ref.shape[1]), F32)
        hc_ref[...] = jnp.zeros_like(hc_ref)

    @pl.when(n > 0)
    def _():
        uext_ref[0:CONV_TAIL, :] = uext_ref[tt:tt + CONV_TAIL, :]

    u = u_ref[...].astype(F32)
    uext_ref[CONV_TAIL:CONV_TAIL + tt, :] = u
    uc = (cb_ref[...] + cw_ref[3:4, :] * u
          + cw_ref[2:3, :] * uext_ref[CONV_TAIL - 1:CONV_TAIL - 1 + tt, :]
          + cw_ref[1:2, :] * uext_ref[CONV_TAIL - 2:CONV_TAIL - 2 + tt, :]
          + cw_ref[0:1, :] * uext_ref[CONV_TAIL - 3:CONV_TAIL - 3 + tt, :])
    ucb = uc.astype(BF16)
    ra = jnp.concatenate(
        [_dot(ucb[:, i * HEAD_DIM:(i + 1) * HEAD_DIM], wa_ref[i]) for i in range(nblk)], axis=1)
    ia = jnp.concatenate(
        [_dot(ucb[:, i * HEAD_DIM:(i + 1) * HEAD_DIM], wi_ref[i]) for i in range(nblk)], axis=1)
    r = jax.nn.sigmoid(ra + ba_ref[...])
    ig = jax.nn.sigmoid(ia + bi_ref[...])
    log_a = (-RG_C) * r * _softplus(-lam_ref[...])
    a = jnp.exp(log_a)
    bt = jnp.sqrt(-jnp.tanh(log_a) * (a * a + 1.0)) * (ig * uc)

    row = lax.broadcasted_iota(jnp.int32, a.shape, 0)
    d = 1
    while d < tt:
        keep = row >= d
        bt = jnp.where(keep, a * pltpu.roll(bt, d, 0) + bt, bt)
        a = jnp.where(keep, a * pltpu.roll(a, d, 0), a)
        d *= 2
    h = a * hc_ref[...] + bt
    hc_ref[...] = h[tt - 1:tt, :]

    g = gate_ref[...].astype(F32)
    gelu = 0.5 * g * (1.0 + jnp.tanh(math.sqrt(2.0 / math.pi) * (g + 0.044715 * (g * g * g))))
    o_ref[...] = (h * gelu).astype(BF16)


def _rglru(proj, conv_w, conv_b, w_a, b_a, w_i, b_i, lam, batch, seq, tt):
    t = proj.shape[0]
    width = conv_w.shape[1]
    nblk = w_a.shape[0]
    tt = min(tt, seq)
    n_t = seq // tt
    row = lambda x: x.reshape(1, width)
    vec = pl.BlockSpec((1, width), lambda b, n: (0, 0))
    blk = pl.BlockSpec((nblk, HEAD_DIM, HEAD_DIM), lambda b, n: (0, 0, 0))
    return pl.pallas_call(
        functools.partial(_rglru_kernel, tt=tt, nblk=nblk),
        grid=(batch, n_t),
        in_specs=[pl.BlockSpec((tt, width), lambda b, n: (b * n_t + n, 4)),
                  pl.BlockSpec((tt, width), lambda b, n: (b * n_t + n, 5)),
                  pl.BlockSpec((4, width), lambda b, n: (0, 0)),
                  vec, blk, vec, blk, vec, vec],
        out_specs=pl.BlockSpec((tt, width), lambda b, n: (b * n_t + n, 0)),
        out_shape=jax.ShapeDtypeStruct((t, width), BF16),
        scratch_shapes=[pltpu.VMEM((tt + CONV_TAIL, width), F32), pltpu.VMEM((1, width), F32)],
        compiler_params=_cparams(2),
        name="rglru",
    )(proj, proj, conv_w, row(conv_b), w_a.astype(BF16), row(b_a), w_i.astype(BF16), row(b_i),
      row(lam))


def _logf_cumsum_kernel(fl_ref, bf_ref, o_ref, *, seq):
    z = fl_ref[...] + bf_ref[...]
    c = jnp.minimum(z, 0.0) - jnp.log1p(jnp.exp(-jnp.abs(z)))
    row = lax.broadcasted_iota(jnp.int32, c.shape, 0)
    d = 1
    while d < seq:
        c = c + jnp.where(row >= d, pltpu.roll(c, d, 0), 0.0)
        d *= 2
    o_ref[...] = c


def _logf_cumsum(fl, b_f, batch, seq):
    return pl.pallas_call(
        functools.partial(_logf_cumsum_kernel, seq=seq),
        grid=(batch,),
        in_specs=[pl.BlockSpec((seq, LANES), lambda b: (b, 0)),
                  pl.BlockSpec((1, LANES), lambda b: (0, 0))],
        out_specs=pl.BlockSpec((seq, LANES), lambda b: (b, 0)),
        out_shape=jax.ShapeDtypeStruct(fl.shape, F32),
        compiler_params=_cparams(1),
        name="logf_cumsum",
    )(fl, b_f)


def _fox_kernel(qt_ref, kt_ref, fl_ref, q_ref, k_ref, v_ref, ck_ref, o_ref, m_ref, l_ref, acc_ref,
                *, tq, tk):
    p = pl.program_id(2)
    qi, ki, flags = qt_ref[p], kt_ref[p], fl_ref[p]

    @pl.when(ki == 0)
    def _():
        m_ref[...] = jnp.full_like(m_ref, -jnp.inf)
        l_ref[...] = jnp.zeros_like(l_ref)
        acc_ref[...] = jnp.zeros_like(acc_ref)

    def step(masked):
        s = lax.dot_general(q_ref[...], k_ref[...], (((1,), (1,)), ((), ())),
                            preferred_element_type=F32)
        s = s * (HEAD_DIM ** -0.5) - ck_ref[0]
        if masked:
            row = lax.broadcasted_iota(jnp.int32, s.shape, 0) + qi * tq
            col = lax.broadcasted_iota(jnp.int32, s.shape, 1) + ki * tk
            s = jnp.where(col <= row, s, -jnp.inf)
        m_old = m_ref[...]
        m_new = jnp.maximum(m_old, jnp.max(s, axis=-1, keepdims=True))
        alpha = jnp.exp(m_old - m_new)
        pr = jnp.exp(s - m_new)
        l_ref[...] = alpha * l_ref[...] + jnp.sum(pr, axis=-1, keepdims=True)
        acc_ref[...] = alpha * acc_ref[...] + _dot(pr.astype(BF16), v_ref[...])
        m_ref[...] = m_new

    @pl.when((flags & 1) == 0)
    def _():
        step(False)

    @pl.when((flags & 1) == 1)
    def _():
        step(True)

    @pl.when((flags & 2) == 2)
    def _():
        o_ref[...] = (acc_ref[...] / l_ref[...]).astype(BF16)


def _fox_attention(qkv, cum_k, batch, seq, heads, tq, tk):
    t = qkv.shape[0]
    tq, tk = min(tq, seq), min(tk, seq)
    nq, nk = seq // tq, seq // tk
    pairs = []
    for qi in range(nq):
        last = (qi * tq + tq - 1) // tk
        for ki in range(last + 1):
            masked = (ki + 1) * tk - 1 > qi * tq
            pairs.append((qi, ki, int(masked) | (2 if ki == last else 0)))
    qt, kt, fl = (jnp.asarray(np.array(c, np.int32)) for c in zip(*pairs))
    grid_spec = pltpu.PrefetchScalarGridSpec(
        num_scalar_prefetch=3,
        grid=(batch, heads, len(pairs)),
        in_specs=[
            pl.BlockSpec((tq, HEAD_DIM), lambda b, h, p, qt, kt, fl: (b * nq + qt[p], h)),
            pl.BlockSpec((tk, HEAD_DIM), lambda b, h, p, qt, kt, fl: (b * nk + kt[p], heads + h)),
            pl.BlockSpec((tk, HEAD_DIM), lambda b, h, p, qt, kt, fl: (b * nk + kt[p], 2 * heads + h)),
            pl.BlockSpec((1, 1, tk), lambda b, h, p, qt, kt, fl: (b * heads + h, 0, kt[p])),
        ],
        out_specs=pl.BlockSpec((tq, HEAD_DIM), lambda b, h, p, qt, kt, fl: (b * nq + qt[p], h)),
        scratch_shapes=[pltpu.VMEM((tq, 1), F32), pltpu.VMEM((tq, 1), F32),
                        pltpu.VMEM((tq, HEAD_DIM), F32)],
    )
    return pl.pallas_call(
        functools.partial(_fox_kernel, tq=tq, tk=tk),
        grid_spec=grid_spec,
        out_shape=jax.ShapeDtypeStruct((t, heads * HEAD_DIM), BF16),
        compiler_params=_cparams(3),
        name="fox_attention",
    )(qt, kt, fl, qkv, qkv, qkv, cum_k)


def _router_kernel(h_ref, rw_ref, rb_ref, idx_ref, wts_ref, rank_ref, cnt_ref, *, tm):
    @pl.when(pl.program_id(0) == 0)
    def _():
        cnt_ref[...] = jnp.zeros_like(cnt_ref)

    logits = jnp.dot(h_ref[...], rw_ref[...], preferred_element_type=F32,
                     precision=lax.Precision.HIGHEST)
    scores = jax.nn.sigmoid(logits)
    sel = scores + rb_ref[...]
    lane = lax.broadcasted_iota(jnp.int32, sel.shape, 1)
    lane_f = lane.astype(F32)
    idx = jnp.zeros(sel.shape, F32)
    wts = jnp.zeros(sel.shape, F32)
    chosen = jnp.zeros(sel.shape, F32)
    hits = []
    for k in range(TOP_K):
        top = jnp.max(sel, axis=-1, keepdims=True)
        first = jnp.min(jnp.where(sel == top, lane_f, float(LANES)), axis=-1, keepdims=True)
        hit = lane_f == first
        hits.append(hit)
        idx = jnp.where(lane == k, first, idx)
        wts = jnp.where(lane == k, jnp.sum(jnp.where(hit, scores, 0.0), axis=-1, keepdims=True), wts)
        chosen = jnp.where(hit, 1.0, chosen)
        sel = jnp.where(hit, -jnp.inf, sel)
    wts = wts / jnp.sum(wts, axis=-1, keepdims=True) * ROUTED_SCALE

    r = lax.broadcasted_iota(jnp.int32, (tm, tm), 0)
    c = lax.broadcasted_iota(jnp.int32, (tm, tm), 1)
    before = jnp.where(c < r, 1.0, 0.0).astype(BF16)
    rank_all = _dot(before, chosen.astype(BF16)) + cnt_ref[...]
    rank = jnp.zeros(sel.shape, F32)
    for k in range(TOP_K):
        rank = jnp.where(lane == k, jnp.sum(jnp.where(hits[k], rank_all, 0.0), axis=-1, keepdims=True),
                         rank)
    cnt_ref[...] = cnt_ref[...] + jnp.sum(chosen, axis=0, keepdims=True)
    idx_ref[...] = idx.astype(jnp.int32)
    wts_ref[...] = wts
    rank_ref[...] = rank.astype(jnp.int32)


def _router(h, router_w, router_bias, tm=256):
    t, d = h.shape
    e = router_w.shape[1]
    tm = min(tm, t)
    rw = jnp.pad(router_w, ((0, 0), (0, LANES - e)))
    rb = jnp.pad(router_bias.astype(F32), (0, LANES - e), constant_values=-jnp.inf).reshape(1, LANES)
    tile = pl.BlockSpec((tm, LANES), lambda i: (i, 0))
    return pl.pallas_call(
        functools.partial(_router_kernel, tm=tm),
        grid=(t // tm,),
        in_specs=[pl.BlockSpec((tm, d), lambda i: (i, 0)),
                  pl.BlockSpec((d, LANES), lambda i: (0, 0)),
                  pl.BlockSpec((1, LANES), lambda i: (0, 0))],
        out_specs=[tile, tile, tile, pl.BlockSpec((1, LANES), lambda i: (0, 0))],
        out_shape=[jax.ShapeDtypeStruct((t, LANES), jnp.int32),
                   jax.ShapeDtypeStruct((t, LANES), F32),
                   jax.ShapeDtypeStruct((t, LANES), jnp.int32),
                   jax.ShapeDtypeStruct((1, LANES), F32)],
        compiler_params=_cparams(1),
        name="router_topk",
    )(h, rw, rb)


def _dispatch_kernel(pstart_ref, cnt_ref, idx_ref, rank_ref, x_ref, xs_ref, zero_ref, sem, *,
                     tm, n_exp):
    i = pl.program_id(0)

    def row_copy(r, dst):
        return pltpu.make_async_copy(x_ref.at[pl.ds(r, 1), :], xs_ref.at[pl.ds(dst, 1), :], sem)

    def zero_copy(dst):
        return pltpu.make_async_copy(zero_ref, xs_ref.at[pl.ds(dst, 1), :], sem)

    def pad_rows(e):
        first = pstart_ref[e] + cnt_ref[e]
        return first, pstart_ref[e + 1] - first

    @pl.when(i == 0)
    def _():
        zero_ref[...] = jnp.zeros_like(zero_ref)

        def per_expert(e, carry):
            first, n = pad_rows(e)
            lax.fori_loop(0, n, lambda j, c: (zero_copy(first + j).start(), c)[1], 0)
            return carry

        lax.fori_loop(0, n_exp, per_expert, 0)

    def issue(a, carry):
        dst = pstart_ref[idx_ref[0, 0, a]] + rank_ref[0, 0, a]
        row_copy(a // TOP_K, dst).start()
        return carry

    lax.fori_loop(0, tm * TOP_K, issue, 0)
    lax.fori_loop(0, tm * TOP_K, lambda a, c: (row_copy(0, 0).wait(), c)[1], 0)

    @pl.when(i == 0)
    def _():
        def per_expert(e, carry):
            _, n = pad_rows(e)
            lax.fori_loop(0, n, lambda j, c: (zero_copy(0).wait(), c)[1], 0)
            return carry

        lax.fori_loop(0, n_exp, per_expert, 0)


def _dispatch(x, idx, rank, pstart, counts, n_rows, tm=256):
    t, d = x.shape
    tm = min(tm, t)
    n_exp = counts.shape[0]
    smem_tile = pl.BlockSpec((1, 1, tm * TOP_K), lambda i, ps, cn: (i, 0, 0), memory_space=pltpu.SMEM)
    grid_spec = pltpu.PrefetchScalarGridSpec(
        num_scalar_prefetch=2,
        grid=(t // tm,),
        in_specs=[smem_tile, smem_tile, pl.BlockSpec((tm, d), lambda i, ps, cn: (i, 0))],
        out_specs=pl.BlockSpec(memory_space=pl.ANY),
        scratch_shapes=[pltpu.VMEM((1, d), x.dtype), pltpu.SemaphoreType.DMA(())],
    )
    return pl.pallas_call(
        functools.partial(_dispatch_kernel, tm=tm, n_exp=n_exp),
        grid_spec=grid_spec,
        out_shape=jax.ShapeDtypeStruct((n_rows, d), x.dtype),
        compiler_params=_cparams(1),
        name="moe_dispatch",
    )(pstart, counts, idx.reshape(t // tm, 1, tm * TOP_K), rank.reshape(t // tm, 1, tm * TOP_K), x)


def _expert_ffn_kernel(be_ref, nb_ref, x_ref, wg_ref, wu_ref, wd_ref, o_ref):
    i = pl.program_id(0)

    @pl.when(i < nb_ref[0])
    def _():
        x = x_ref[...].astype(BF16)
        hb = (_silu(_dot(x, wg_ref[...])) * _dot(x, wu_ref[...])).astype(BF16)
        o_ref[...] = _dot(hb, wd_ref[...])

    @pl.when(i >= nb_ref[0])
    def _():
        o_ref[...] = jnp.zeros_like(o_ref)


def _expert_ffn(xs, block_e, n_used, w_gate, w_up, w_down):
    p, d = xs.shape
    f = w_gate.shape[2]
    nb = p // EXPERT_BLOCK
    grid_spec = pltpu.PrefetchScalarGridSpec(
        num_scalar_prefetch=2,
        grid=(nb,),
        in_specs=[
            pl.BlockSpec((EXPERT_BLOCK, d), lambda i, be, nu: (jnp.minimum(i, nu[0] - 1), 0)),
            pl.BlockSpec((None, d, f), lambda i, be, nu: (be[i], 0, 0)),
            pl.BlockSpec((None, d, f), lambda i, be, nu: (be[i], 0, 0)),
            pl.BlockSpec((None, f, d), lambda i, be, nu: (be[i], 0, 0)),
        ],
        out_specs=pl.BlockSpec((EXPERT_BLOCK, d), lambda i, be, nu: (i, 0)),
    )
    return pl.pallas_call(
        _expert_ffn_kernel,
        grid_spec=grid_spec,
        out_shape=jax.ShapeDtypeStruct((p, d), F32),
        compiler_params=_cparams(1),
        name="expert_ffn",
    )(block_e, n_used, xs, w_gate, w_up, w_down)


def _combine_kernel(pstart_ref, idx_ref, rank_ref, wts_ref, hf_ref, hb_ref, sg_ref, su_ref, sd_ref,
                    g_ref, b_ref, ys_ref, of_ref, ob_ref, buf_ref, sem, *, tm, alpha):
    def row_copy(k, r, src):
        return pltpu.make_async_copy(ys_ref.at[pl.ds(src, 1), :], buf_ref.at[k, pl.ds(r, 1), :], sem)

    def issue(a, carry):
        src = pstart_ref[idx_ref[0, 0, a]] + rank_ref[0, 0, a]
        row_copy(a % TOP_K, a // TOP_K, src).start()
        return carry

    lax.fori_loop(0, tm * TOP_K, issue, 0)

    xb = hb_ref[...]
    shared = _dot((_silu(_dot(xb, sg_ref[...])) * _dot(xb, su_ref[...])).astype(BF16), sd_ref[...])
    z = alpha * hf_ref[...] + shared

    lax.fori_loop(0, tm * TOP_K, lambda a, c: (row_copy(0, 0, 0).wait(), c)[1], 0)
    wts = wts_ref[...]
    for k in range(TOP_K):
        z = z + wts[:, k:k + 1] * buf_ref[k]
    out = _layer_norm(z, g_ref[...], b_ref[...])
    of_ref[...] = out
    ob_ref[...] = out.astype(BF16)


def _combine(ys, idx, rank, wts, pstart, h_f32, h_bf16, sh_gate, sh_up, sh_down, g, b, alpha, tm=128):
    t, d = h_f32.shape
    f = sh_gate.shape[1]
    tm = min(tm, t)
    smem_tile = pl.BlockSpec((1, 1, tm * TOP_K), lambda i, ps: (i, 0, 0), memory_space=pltpu.SMEM)
    tile = lambda w: pl.BlockSpec((tm, w), lambda i, ps: (i, 0))
    const = lambda shape: pl.BlockSpec(shape, lambda i, ps: (0, 0))
    grid_spec = pltpu.PrefetchScalarGridSpec(
        num_scalar_prefetch=1,
        grid=(t // tm,),
        in_specs=[smem_tile, smem_tile, tile(LANES), tile(d), tile(d),
                  const((d, f)), const((d, f)), const((f, d)), const((1, d)), const((1, d)),
                  pl.BlockSpec(memory_space=pl.ANY)],
        out_specs=[tile(d), tile(d)],
        scratch_shapes=[pltpu.VMEM((TOP_K, tm, d), F32), pltpu.SemaphoreType.DMA(())],
    )
    return pl.pallas_call(
        functools.partial(_combine_kernel, tm=tm, alpha=alpha),
        grid_spec=grid_spec,
        out_shape=[jax.ShapeDtypeStruct((t, d), F32), jax.ShapeDtypeStruct((t, d), BF16)],
        compiler_params=_cparams(1),
        name="moe_combine",
    )(pstart, idx.reshape(t // tm, 1, tm * TOP_K), rank.reshape(t // tm, 1, tm * TOP_K), wts,
      h_f32, h_bf16, sh_gate, sh_up, sh_down, g.reshape(1, d), b.reshape(1, d), ys)


def _moe_ln(h_f32, h_bf16, router_w, router_bias, w_gate, w_up, w_down, sh_gate, sh_up, sh_down,
            g, b, alpha):
    t, d = h_f32.shape
    n_exp = router_w.shape[1]
    idx, wts, rank, counts = _router(h_f32, router_w, router_bias)
    idx, rank = idx[:, :TOP_K], rank[:, :TOP_K]

    counts = counts[0, :n_exp].astype(jnp.int32)
    padded = (counts + EXPERT_BLOCK - 1) // EXPERT_BLOCK * EXPERT_BLOCK
    pends = jnp.cumsum(padded)
    pstart = jnp.concatenate([jnp.zeros((1,), jnp.int32), pends]).astype(jnp.int32)
    n_blocks = t * TOP_K // EXPERT_BLOCK + n_exp
    block_e = jnp.minimum(
        jnp.searchsorted(pends, jnp.arange(n_blocks, dtype=jnp.int32) * EXPERT_BLOCK, side="right"),
        n_exp - 1).astype(jnp.int32)
    n_used = (pends[-1:] // EXPERT_BLOCK).astype(jnp.int32)

    xs = _dispatch(h_f32, idx, rank, pstart, counts, n_blocks * EXPERT_BLOCK)
    ys = _expert_ffn(xs, block_e, n_used, w_gate.astype(BF16), w_up.astype(BF16), w_down.astype(BF16))
    return _combine(ys, idx, rank, wts, pstart, h_f32, h_bf16, sh_gate.astype(BF16),
                    sh_up.astype(BF16), sh_down.astype(BF16), g, b, alpha)


def kernel(x, even_w_in, ret_gn_g, rg_conv_w, rg_conv_b, rg_w_a, rg_b_a, rg_w_i, rg_b_i, rg_lambda,
           even_w_out, fox_w_in, fox_b_f, fox_w_out, ln_g, ln_b, router_w, router_bias, exp_w_gate,
           exp_w_up, exp_w_down, sh_w_gate, sh_w_up, sh_w_down):
    batch, seq, d = x.shape
    depth = ln_g.shape[0]
    alpha = (2 * depth) ** 0.25
    ret_width = ret_gn_g.shape[1]
    fox_heads = fox_b_f.shape[1]

    h_f32 = x.reshape(batch * seq, d)
    h_bf16 = h_f32.astype(BF16)
    for layer in range(depth):
        j = layer // 2
        if layer % 2 == 0:
            proj = _matmul(h_bf16, even_w_in[j].astype(BF16), BF16)
            y_ret = _retention(proj, ret_gn_g[j], batch, seq, chunk=256)
            y_rnn = _rglru(proj, rg_conv_w[j], rg_conv_b[j], rg_w_a[j], rg_b_a[j], rg_w_i[j],
                           rg_b_i[j], rg_lambda[j], batch, seq, tt=256)
            w_out = even_w_out[j].astype(BF16)
            ys, ws = [y_ret, y_rnn], [w_out[:ret_width], w_out[ret_width:]]
        else:
            w_in = fox_w_in[j]
            qkv = _matmul(h_bf16, w_in[:, :3 * d].astype(BF16), BF16)
            w_f = jnp.pad(w_in[:, 3 * d:], ((0, 0), (0, LANES - fox_heads))).astype(BF16)
            fl = _matmul(h_bf16, w_f, F32)
            b_f = jnp.pad(fox_b_f[j], (0, LANES - fox_heads)).reshape(1, LANES)
            cum = _logf_cumsum(fl, b_f, batch, seq)
            cum_k = jnp.transpose(cum.reshape(batch, seq, LANES)[:, :, :fox_heads], (0, 2, 1))
            cum_k = cum_k.reshape(batch * fox_heads, 1, seq)
            o = _fox_attention(qkv, cum_k, batch, seq, fox_heads, tq=1024, tk=512)
            ys, ws = [o], [fox_w_out[j].astype(BF16)]
        h_f32, h_bf16 = _proj_ln(ys, ws, h_f32, ln_g[layer, 0], ln_b[layer, 0], alpha)
        h_f32, h_bf16 = _moe_ln(h_f32, h_bf16, router_w[layer], router_bias[layer],
                                exp_w_gate[layer], exp_w_up[layer], exp_w_down[layer],
                                sh_w_gate[layer], sh_w_up[layer], sh_w_down[layer],
                                ln_g[layer, 1], ln_b[layer, 1], alpha)
    return h_f32.reshape(batch, seq, d)
```

```python
import functools
import math

import numpy as np
import jax
import jax.numpy as jnp
from jax import lax
from jax.experimental import pallas as pl
from jax.experimental.pallas import tpu as pltpu

F32 = jnp.float32
BF16 = jnp.bfloat16

HEAD_DIM = 128
TOP_K = 8
ROUTED_SCALE = 2.5
ROPE_BASE = 10000.0
RG_C = 8.0
LN_EPS = 1e-5
CONV_TAIL = 8

V7X_VMEM_BYTES = 64 * 1024 * 1024
VMEM_LIMIT = V7X_VMEM_BYTES - 8 * 1024 * 1024
LANES = 128

EXPERT_BLOCK = 256


def _cparams(grid_rank):
    return pltpu.CompilerParams(dimension_semantics=("arbitrary",) * grid_rank,
                                vmem_limit_bytes=VMEM_LIMIT)


def _dot(a, b):
    return jnp.dot(a, b, preferred_element_type=F32)


def _silu(x):
    return x * jax.nn.sigmoid(x)


def _softplus(x):
    return jnp.maximum(x, 0.0) + jnp.log1p(jnp.exp(-jnp.abs(x)))


def _layer_norm(z, g, b):
    mu = jnp.mean(z, axis=-1, keepdims=True)
    zc = z - mu
    var = jnp.mean(zc * zc, axis=-1, keepdims=True)
    return zc * lax.rsqrt(var + LN_EPS) * g + b


def _mm_kernel(x_ref, w_ref, o_ref):
    o_ref[...] = _dot(x_ref[...], w_ref[...]).astype(o_ref.dtype)


def _tile(n, target):
    best = LANES
    for c in range(LANES, min(n, target) + 1, LANES):
        if n % c == 0:
            best = c
    assert n % best == 0
    return best


def _matmul(x, w, out_dtype, tm=1024, tn=1024):
    m, k = x.shape
    n = w.shape[1]
    tm, tn = _tile(m, tm), _tile(n, tn)
    return pl.pallas_call(
        _mm_kernel,
        grid=(m // tm, n // tn),
        in_specs=[pl.BlockSpec((tm, k), lambda i, j: (i, 0)),
                  pl.BlockSpec((k, tn), lambda i, j: (0, j))],
        out_specs=pl.BlockSpec((tm, tn), lambda i, j: (i, j)),
        out_shape=jax.ShapeDtypeStruct((m, n), out_dtype),
        compiler_params=_cparams(2),
        name="in_proj",
    )(x, w)


def _slab_rows(d):
    assert d % (2 * LANES) == 0
    return d // (2 * LANES)


def _pack_pairs(x):
    half = x.shape[1] // 2
    hi = lax.bitcast_convert_type(x[:, :half].astype(BF16).astype(F32), jnp.uint32)
    lo = lax.bitcast_convert_type(x[:, half:].astype(BF16).astype(F32), jnp.uint32)
    return hi | (lo >> 16)


def _unpack_pairs(w):
    hi = lax.bitcast_convert_type(w & jnp.uint32(0xFFFF0000), F32)
    lo = lax.bitcast_convert_type(w << 16, F32)
    return hi, lo


def _store_slabs(ref, words, rows, nc):
    for s in range(nc):
        ref[pl.ds(s, rows, stride=nc), :] = words[:, s * LANES:(s + 1) * LANES]


def _proj_ln_kernel(*refs, n_in, alpha, tm, nc):
    ys, ws = refs[:n_in], refs[n_in:2 * n_in]
    h_ref, g_ref, b_ref, of_ref, ob_ref, op_ref = refs[2 * n_in:]
    acc = _dot(ys[0][...], ws[0][...])
    for y_ref, w_ref in zip(ys[1:], ws[1:]):
        acc = acc + _dot(y_ref[...], w_ref[...])
    out = _layer_norm(alpha * h_ref[...] + acc, g_ref[...], b_ref[...])
    of_ref[...] = out
    ob_ref[...] = out.astype(BF16)
    _store_slabs(op_ref, _pack_pairs(out), tm, nc)


def _proj_ln(ys, ws, h, g, b, alpha, tm=256):
    t, d = h.shape
    tm = min(tm, t)
    n_in = len(ys)
    nc = _slab_rows(d)
    in_specs = ([pl.BlockSpec((tm, y.shape[1]), lambda i: (i, 0)) for y in ys]
                + [pl.BlockSpec(w.shape, lambda i: (0, 0)) for w in ws]
                + [pl.BlockSpec((tm, d), lambda i: (i, 0)),
                   pl.BlockSpec((1, d), lambda i: (0, 0)),
                   pl.BlockSpec((1, d), lambda i: (0, 0))])
    return pl.pallas_call(
        functools.partial(_proj_ln_kernel, n_in=n_in, alpha=alpha, tm=tm, nc=nc),
        grid=(t // tm,),
        in_specs=in_specs,
        out_specs=[pl.BlockSpec((tm, d), lambda i: (i, 0)),
                   pl.BlockSpec((tm, d), lambda i: (i, 0)),
                   pl.BlockSpec((tm * nc, LANES), lambda i: (i, 0))],
        out_shape=[jax.ShapeDtypeStruct((t, d), F32), jax.ShapeDtypeStruct((t, d), BF16),
                   jax.ShapeDtypeStruct((t * nc, LANES), jnp.uint32)],
        compiler_params=_cparams(1),
        name="out_proj_ln",
    )(*ys, *ws, h, g.reshape(1, d), b.reshape(1, d))


def _retention_kernel(q_ref, k_ref, v_ref, g_ref, cos_ref, sin_ref, intra_ref, qdec_ref, kdec_ref,
                      gn_ref, o_ref, state_ref, *, heads, chunk_decay):
    @pl.when(pl.program_id(1) == 0)
    def _():
        state_ref[...] = jnp.zeros_like(state_ref)

    cos, sin = cos_ref[...], sin_ref[...]
    half = HEAD_DIM // 2
    for h in range(heads):
        sl = slice(h * HEAD_DIM, (h + 1) * HEAD_DIM)
        q = q_ref[:, sl].astype(F32)
        k = k_ref[:, sl].astype(F32)
        q = q * cos + pltpu.roll(q, half, 1) * sin
        k = (k * cos + pltpu.roll(k, half, 1) * sin) * (HEAD_DIM ** -0.5)
        v = v_ref[:, sl]
        s = lax.dot_general(q.astype(BF16), k.astype(BF16), (((1,), (1,)), ((), ())),
                            preferred_element_type=F32) * intra_ref[h]
        state = state_ref[h]
        y = _dot(s.astype(BF16), v) + _dot((q * qdec_ref[h]).astype(BF16), state.astype(BF16))
        kv = lax.dot_general((k * kdec_ref[h]).astype(BF16), v, (((0,), (0,)), ((), ())),
                             preferred_element_type=F32)
        state_ref[h] = state * chunk_decay[h] + kv
        mu = jnp.mean(y, axis=-1, keepdims=True)
        yc = y - mu
        var = jnp.mean(yc * yc, axis=-1, keepdims=True)
        yn = yc * lax.rsqrt(var + LN_EPS) * gn_ref[:, sl]
        o_ref[:, sl] = (yn * _silu(g_ref[:, sl].astype(F32))).astype(BF16)


def _retention(proj, gn_g, batch, seq, chunk):
    t = proj.shape[0]
    width = gn_g.shape[0]
    heads = width // HEAD_DIM
    chunk = min(chunk, seq)
    n_chunks = seq // chunk
    half = HEAD_DIM // 2

    pos = jnp.arange(seq, dtype=F32)
    inv = ROPE_BASE ** (-jnp.arange(half, dtype=F32) / half)
    ang = pos[:, None] * inv[None, :]
    cos2 = jnp.concatenate([jnp.cos(ang), jnp.cos(ang)], axis=1)
    sin2 = jnp.concatenate([-jnp.sin(ang), jnp.sin(ang)], axis=1)

    log_g = jnp.log1p(-jnp.exp2(-5.0 - jnp.arange(heads, dtype=F32)))
    i = jnp.arange(chunk, dtype=F32)
    rel = i[:, None] - i[None, :]
    intra = jnp.where(rel >= 0, jnp.exp(log_g[:, None, None] * jnp.maximum(rel, 0.0)), 0.0)
    kdec = jnp.exp(log_g[:, None] * (chunk - 1.0 - i)[None, :])
    qdec = jnp.exp(log_g[:, None] * (i + 1.0)[None, :])
    kdec = jnp.broadcast_to(kdec[:, :, None], (heads, chunk, HEAD_DIM))
    qdec = jnp.broadcast_to(qdec[:, :, None], (heads, chunk, HEAD_DIM))
    chunk_decay = tuple(math.exp(math.log1p(-2.0 ** (-5.0 - h)) * chunk) for h in range(heads))

    def col(c):
        return pl.BlockSpec((chunk, width), lambda b, n: (b * n_chunks + n, c))

    const3 = lambda shape: pl.BlockSpec(shape, lambda b, n: (0, 0, 0))
    return pl.pallas_call(
        functools.partial(_retention_kernel, heads=heads, chunk_decay=chunk_decay),
        grid=(batch, n_chunks),
        in_specs=[col(0), col(1), col(2), col(3),
                  pl.BlockSpec((chunk, HEAD_DIM), lambda b, n: (n, 0)),
                  pl.BlockSpec((chunk, HEAD_DIM), lambda b, n: (n, 0)),
                  const3((heads, chunk, chunk)),
                  const3((heads, chunk, HEAD_DIM)),
                  const3((heads, chunk, HEAD_DIM)),
                  pl.BlockSpec((1, width), lambda b, n: (0, 0))],
        out_specs=pl.BlockSpec((chunk, width), lambda b, n: (b * n_chunks + n, 0)),
        out_shape=jax.ShapeDtypeStruct((t, width), BF16),
        scratch_shapes=[pltpu.VMEM((heads, HEAD_DIM, HEAD_DIM), F32)],
        compiler_params=_cparams(2),
        name="retention",
    )(proj, proj, proj, proj, cos2, sin2, intra, qdec, kdec, gn_g.reshape(1, width))


def _rglru_kernel(u_ref, gate_ref, cw_ref, cb_ref, wa_ref, ba_ref, wi_ref, bi_ref, lam_ref,
                  o_ref, uext_ref, hc_ref, *, tt, nblk):
    n = pl.program_id(1)

    @pl.when(n == 0)
    def _():
        uext_ref[0:CONV_TAIL, :] = jnp.zeros((CONV_TAIL, uext_ref.shape[1]), F32)
        hc_ref[...] = jnp.zeros_like(hc_ref)

    @pl.when(n > 0)
    def _():
        uext_ref[0:CONV_TAIL, :] = uext_ref[tt:tt + CONV_TAIL, :]

    u = u_ref[...].astype(F32)
    uext_ref[CONV_TAIL:CONV_TAIL + tt, :] = u
    uc = (cb_ref[...] + cw_ref[3:4, :] * u
          + cw_ref[2:3, :] * uext_ref[CONV_TAIL - 1:CONV_TAIL - 1 + tt, :]
          + cw_ref[1:2, :] * uext_ref[CONV_TAIL - 2:CONV_TAIL - 2 + tt, :]
          + cw_ref[0:1, :] * uext_ref[CONV_TAIL - 3:CONV_TAIL - 3 + tt, :])
    ucb = uc.astype(BF16)
    ra = jnp.concatenate(
        [_dot(ucb[:, i * HEAD_DIM:(i + 1) * HEAD_DIM], wa_ref[i]) for i in range(nblk)], axis=1)
    ia = jnp.concatenate(
        [_dot(ucb[:, i * HEAD_DIM:(i + 1) * HEAD_DIM], wi_ref[i]) for i in range(nblk)], axis=1)
    r = jax.nn.sigmoid(ra + ba_ref[...])
    ig = jax.nn.sigmoid(ia + bi_ref[...])
    log_a = (-RG_C) * r * _softplus(-lam_ref[...])
    a = jnp.exp(log_a)
    bt = jnp.sqrt(-jnp.tanh(log_a) * (a * a + 1.0)) * (ig * uc)

    row = lax.broadcasted_iota(jnp.int32, a.shape, 0)
    d = 1
    while d < tt:
        keep = row >= d
        bt = jnp.where(keep, a * pltpu.roll(bt, d, 0) + bt, bt)
        a = jnp.where(keep, a * pltpu.roll(a, d, 0), a)
        d *= 2
    h = a * hc_ref[...] + bt
    hc_ref[...] = h[tt - 1:tt, :]

    g = gate_ref[...].astype(F32)
    gelu = 0.5 * g * (1.0 + jnp.tanh(math.sqrt(2.0 / math.pi) * (g + 0.044715 * (g * g * g))))
    o_ref[...] = (h * gelu).astype(BF16)


def _rglru(proj, conv_w, conv_b, w_a, b_a, w_i, b_i, lam, batch, seq, tt):
    t = proj.shape[0]
    width = conv_w.shape[1]
    nblk = w_a.shape[0]
    tt = min(tt, seq)
    n_t = seq // tt
    row = lambda x: x.reshape(1, width)
    vec = pl.BlockSpec((1, width), lambda b, n: (0, 0))
    blk = pl.BlockSpec((nblk, HEAD_DIM, HEAD_DIM), lambda b, n: (0, 0, 0))
    return pl.pallas_call(
        functools.partial(_rglru_kernel, tt=tt, nblk=nblk),
        grid=(batch, n_t),
        in_specs=[pl.BlockSpec((tt, width), lambda b, n: (b * n_t + n, 4)),
                  pl.BlockSpec((tt, width), lambda b, n: (b * n_t + n, 5)),
                  pl.BlockSpec((4, width), lambda b, n: (0, 0)),
                  vec, blk, vec, blk, vec, vec],
        out_specs=pl.BlockSpec((tt, width), lambda b, n: (b * n_t + n, 0)),
        out_shape=jax.ShapeDtypeStruct((t, width), BF16),
        scratch_shapes=[pltpu.VMEM((tt + CONV_TAIL, width), F32), pltpu.VMEM((1, width), F32)],
        compiler_params=_cparams(2),
        name="rglru",
    )(proj, proj, conv_w, row(conv_b), w_a.astype(BF16), row(b_a), w_i.astype(BF16), row(b_i),
      row(lam))


FORGET_SPLIT = 3


def _bf16_head(x):
    bits = lax.bitcast_convert_type(x, jnp.uint32) & jnp.uint32(0xFFFF0000)
    return lax.bitcast_convert_type(bits, F32)


def _logf_cumsum_kernel(fl_ref, bf_ref, *o_refs, seq):
    z = fl_ref[...] + bf_ref[...]
    c = jnp.minimum(z, 0.0) - jnp.log1p(jnp.exp(-jnp.abs(z)))
    row = lax.broadcasted_iota(jnp.int32, c.shape, 0)
    d = 1
    while d < seq:
        c = c + jnp.where(row >= d, pltpu.roll(c, d, 0), 0.0)
        d *= 2
    x = c * (-(HEAD_DIM ** 0.5))
    for o_ref in o_refs:
        piece = _bf16_head(x)
        o_ref[...] = piece.astype(BF16)
        x = x - piece


def _logf_cumsum(fl, b_f, batch, seq):
    tile = pl.BlockSpec((seq, LANES), lambda b: (b, 0))
    return pl.pallas_call(
        functools.partial(_logf_cumsum_kernel, seq=seq),
        grid=(batch,),
        in_specs=[tile, pl.BlockSpec((1, LANES), lambda b: (0, 0))],
        out_specs=[tile] * FORGET_SPLIT,
        out_shape=[jax.ShapeDtypeStruct(fl.shape, BF16)] * FORGET_SPLIT,
        compiler_params=_cparams(1),
        name="logf_cumsum",
    )(fl, b_f)


def _fox_kernel(q_ref, k_ref, v_ref, kc_ref, o_ref, kext_ref, vext_ref, *, tq, seq):
    qi = pl.program_id(2)

    @pl.when(qi == 0)
    def _():
        kext_ref[:, :HEAD_DIM] = k_ref[...]
        kext_ref[:, HEAD_DIM:] = kc_ref[...]
        vext_ref[:, :HEAD_DIM] = v_ref[...]
        vext_ref[:, HEAD_DIM:] = jnp.ones((seq, HEAD_DIM), BF16)

    lane = lax.broadcasted_iota(jnp.int32, (tq, HEAD_DIM), 1)
    ones = jnp.where(lane < FORGET_SPLIT, 1.0, 0.0).astype(BF16)
    q2 = jnp.concatenate([q_ref[...], ones], axis=1)
    c = (HEAD_DIM ** -0.5) * math.log2(math.e)
    reps = tq // HEAD_DIM

    def logits(kb):
        off = pl.multiple_of(kb * tq, tq)
        return lax.dot_general(q2, kext_ref[pl.ds(off, tq), :], (((1,), (1,)), ((), ())),
                               preferred_element_type=F32)

    def fold(kb, u, m, l, acc, masked):
        if masked:
            row = lax.broadcasted_iota(jnp.int32, u.shape, 0)
            col = lax.broadcasted_iota(jnp.int32, u.shape, 1)
            u = jnp.where(col <= row, u, -jnp.inf)
        m_new = jnp.maximum(m, jnp.broadcast_to(jnp.max(u, axis=-1, keepdims=True), m.shape))
        p = jnp.exp2((u - jnp.concatenate([m_new] * reps, axis=1)) * c)
        alpha = jnp.exp2((m - m_new) * c)
        off = pl.multiple_of(kb * tq, tq)
        pv = _dot(p.astype(BF16), vext_ref[pl.ds(off, tq), :])
        return m_new, alpha * l + pv[:, HEAD_DIM:], alpha * acc + pv[:, :HEAD_DIM]

    def body(kb, carry):
        m, l, acc, u = carry
        u_next = logits(kb + 1)
        return fold(kb, u, m, l, acc, False) + (u_next,)

    init = (jnp.full((tq, HEAD_DIM), -jnp.inf, F32), jnp.zeros((tq, HEAD_DIM), F32),
            jnp.zeros((tq, HEAD_DIM), F32), logits(0))
    m, l, acc, u = lax.fori_loop(0, qi, body, init)
    _, l, acc = fold(qi, u, m, l, acc, True)
    o_ref[...] = (acc / l).astype(BF16)


def _fox_attention(qkv, kc, batch, seq, heads, tq):
    t = qkv.shape[0]
    tq = min(tq, max(HEAD_DIM, seq // 4))
    nq = seq // tq
    assert tq % HEAD_DIM == 0 and seq % tq == 0
    return pl.pallas_call(
        functools.partial(_fox_kernel, tq=tq, seq=seq),
        grid=(batch, heads, nq),
        in_specs=[
            pl.BlockSpec((tq, HEAD_DIM), lambda b, h, qi: (b * nq + qi, h)),
            pl.BlockSpec((seq, HEAD_DIM), lambda b, h, qi: (b, heads + h)),
            pl.BlockSpec((seq, HEAD_DIM), lambda b, h, qi: (b, 2 * heads + h)),
            pl.BlockSpec((seq, HEAD_DIM), lambda b, h, qi: (b * heads + h, 0)),
        ],
        out_specs=pl.BlockSpec((tq, HEAD_DIM), lambda b, h, qi: (b * nq + qi, h)),
        out_shape=jax.ShapeDtypeStruct((t, heads * HEAD_DIM), BF16),
        scratch_shapes=[pltpu.VMEM((seq, 2 * HEAD_DIM), BF16), pltpu.VMEM((seq, 2 * HEAD_DIM), BF16)],
        compiler_params=_cparams(3),
        name="fox_attention",
    )(qkv, qkv, qkv, kc)


def _forget_key_bias(pieces, batch, seq, heads):
    kc = jnp.stack([p.reshape(batch, seq, LANES)[:, :, :heads] for p in pieces], axis=-1)
    kc = jnp.transpose(kc, (0, 2, 1, 3)).reshape(batch * heads * seq, FORGET_SPLIT)
    return jnp.pad(kc, ((0, 0), (0, HEAD_DIM - FORGET_SPLIT)))


def _router_kernel(h_ref, rw_ref, rb_ref, idx_ref, wts_ref, rank_ref, cnt_ref, *, tm):
    @pl.when(pl.program_id(0) == 0)
    def _():
        cnt_ref[...] = jnp.zeros_like(cnt_ref)

    logits = jnp.dot(h_ref[...], rw_ref[...], preferred_element_type=F32,
                     precision=lax.Precision.HIGHEST)
    scores = jax.nn.sigmoid(logits)
    sel = scores + rb_ref[...]
    lane = lax.broadcasted_iota(jnp.int32, sel.shape, 1)
    lane_f = lane.astype(F32)
    idx = jnp.zeros(sel.shape, F32)
    wts = jnp.zeros(sel.shape, F32)
    chosen = jnp.zeros(sel.shape, F32)
    hits = []
    for k in range(TOP_K):
        top = jnp.max(sel, axis=-1, keepdims=True)
        first = jnp.min(jnp.where(sel == top, lane_f, float(LANES)), axis=-1, keepdims=True)
        hit = lane_f == first
        hits.append(hit)
        idx = jnp.where(lane == k, first, idx)
        wts = jnp.where(lane == k, jnp.sum(jnp.where(hit, scores, 0.0), axis=-1, keepdims=True), wts)
        chosen = jnp.where(hit, 1.0, chosen)
        sel = jnp.where(hit, -jnp.inf, sel)
    wts = wts / jnp.sum(wts, axis=-1, keepdims=True) * ROUTED_SCALE

    r = lax.broadcasted_iota(jnp.int32, (tm, tm), 0)
    c = lax.broadcasted_iota(jnp.int32, (tm, tm), 1)
    before = jnp.where(c < r, 1.0, 0.0).astype(BF16)
    rank_all = _dot(before, chosen.astype(BF16)) + cnt_ref[...]
    rank = jnp.zeros(sel.shape, F32)
    for k in range(TOP_K):
        rank = jnp.where(lane == k, jnp.sum(jnp.where(hits[k], rank_all, 0.0), axis=-1, keepdims=True),
                         rank)
    cnt_ref[...] = cnt_ref[...] + jnp.sum(chosen, axis=0, keepdims=True)
    idx_ref[...] = idx.astype(jnp.int32)
    wts_ref[...] = wts
    rank_ref[...] = rank.astype(jnp.int32)


def _router(h, router_w, router_bias, tm=256):
    t, d = h.shape
    e = router_w.shape[1]
    tm = min(tm, t)
    rw = jnp.pad(router_w, ((0, 0), (0, LANES - e)))
    rb = jnp.pad(router_bias.astype(F32), (0, LANES - e), constant_values=-jnp.inf).reshape(1, LANES)
    tile = pl.BlockSpec((tm, LANES), lambda i: (i, 0))
    return pl.pallas_call(
        functools.partial(_router_kernel, tm=tm),
        grid=(t // tm,),
        in_specs=[pl.BlockSpec((tm, d), lambda i: (i, 0)),
                  pl.BlockSpec((d, LANES), lambda i: (0, 0)),
                  pl.BlockSpec((1, LANES), lambda i: (0, 0))],
        out_specs=[tile, tile, tile, pl.BlockSpec((1, LANES), lambda i: (0, 0))],
        out_shape=[jax.ShapeDtypeStruct((t, LANES), jnp.int32),
                   jax.ShapeDtypeStruct((t, LANES), F32),
                   jax.ShapeDtypeStruct((t, LANES), jnp.int32),
                   jax.ShapeDtypeStruct((1, LANES), F32)],
        compiler_params=_cparams(1),
        name="router_topk",
    )(h, rw, rb)


def _dispatch_kernel(pstart_ref, cnt_ref, dest_ref, x_ref, xs_ref, zero_ref, sem, *, tm, n_exp, nc,
                     n_blocks):
    i = pl.program_id(0)
    blk = EXPERT_BLOCK * nc

    def row_copy(r, dst):
        return pltpu.make_async_copy(x_ref.at[pl.ds(pl.multiple_of(r * nc, nc), nc), :],
                                     xs_ref.at[pl.ds(pl.multiple_of(dst * nc, nc), nc), :], sem)

    def zero_copy(dst):
        return pltpu.make_async_copy(zero_ref.at[pl.ds(0, nc), :],
                                     xs_ref.at[pl.ds(pl.multiple_of(dst * nc, nc), nc), :], sem)

    def zero_block(b):
        return pltpu.make_async_copy(zero_ref, xs_ref.at[pl.ds(pl.multiple_of(b * blk, blk), blk), :],
                                     sem)

    def pad_rows(e):
        first = pstart_ref[e] + cnt_ref[e]
        return first, pstart_ref[e + 1] - first

    def first_unused_block():
        return lax.div(pstart_ref[n_exp], EXPERT_BLOCK)

    @pl.when(i == 0)
    def _():
        zero_ref[...] = jnp.zeros_like(zero_ref)

        def per_expert(e, carry):
            first, n = pad_rows(e)
            lax.fori_loop(0, n, lambda j, c: (zero_copy(first + j).start(), c)[1], 0)
            return carry

        lax.fori_loop(0, n_exp, per_expert, 0)
        lax.fori_loop(first_unused_block(), n_blocks, lambda b, c: (zero_block(b).start(), c)[1], 0)

    def issue(r, carry):
        for k in range(TOP_K):
            row_copy(r, dest_ref[0, 0, r * TOP_K + k]).start(priority=k % 2)
        return carry

    lax.fori_loop(0, tm, issue, 0)
    all_rows = xs_ref.at[pl.ds(0, tm * TOP_K * nc), :]
    pltpu.make_async_copy(all_rows, all_rows, sem).wait()

    @pl.when(i == 0)
    def _():
        def per_expert(e, carry):
            _, n = pad_rows(e)
            lax.fori_loop(0, n, lambda j, c: (zero_copy(0).wait(), c)[1], 0)
            return carry

        lax.fori_loop(0, n_exp, per_expert, 0)
        lax.fori_loop(first_unused_block(), n_blocks, lambda b, c: (zero_block(0).wait(), c)[1], 0)


def _dispatch(xp, dest, pstart, counts, n_rows, nc, tm=256):
    t = xp.shape[0] // nc
    tm = min(tm, t)
    n_exp = counts.shape[0]
    assert n_rows >= tm * TOP_K
    grid_spec = pltpu.PrefetchScalarGridSpec(
        num_scalar_prefetch=2,
        grid=(t // tm,),
        in_specs=[pl.BlockSpec((1, 1, tm * TOP_K), lambda i, ps, cn: (i, 0, 0),
                               memory_space=pltpu.SMEM),
                  pl.BlockSpec((tm * nc, LANES), lambda i, ps, cn: (i, 0))],
        out_specs=pl.BlockSpec(memory_space=pl.ANY),
        scratch_shapes=[pltpu.VMEM((EXPERT_BLOCK * nc, LANES), xp.dtype),
                        pltpu.SemaphoreType.DMA(())],
    )
    assert n_rows % EXPERT_BLOCK == 0
    return pl.pallas_call(
        functools.partial(_dispatch_kernel, tm=tm, n_exp=n_exp, nc=nc,
                          n_blocks=n_rows // EXPERT_BLOCK),
        grid_spec=grid_spec,
        out_shape=jax.ShapeDtypeStruct((n_rows * nc, LANES), xp.dtype),
        compiler_params=_cparams(1),
        name="moe_dispatch",
    )(pstart, counts, dest.reshape(t // tm, 1, tm * TOP_K), xp)


def _expert_ffn_kernel(be_ref, nb_ref, x_ref, wg_ref, wu_ref, wd_ref, o_ref, wgb_ref, wub_ref,
                       wdb_ref, *, nc):
    i = pl.program_id(0)
    used = i < nb_ref[0]
    new_expert = jnp.logical_or(i == 0, be_ref[i] != be_ref[jnp.maximum(i - 1, 0)])

    @pl.when(jnp.logical_and(used, new_expert))
    def _():
        wgb_ref[...] = wg_ref[...].astype(BF16)
        wub_ref[...] = wu_ref[...].astype(BF16)
        wdb_ref[...] = wd_ref[...].astype(BF16)

    @pl.when(used)
    def _():
        halves = [_unpack_pairs(x_ref[pl.ds(s, EXPERT_BLOCK, stride=nc), :]) for s in range(nc)]
        x = jnp.concatenate([h[0].astype(BF16) for h in halves] + [h[1].astype(BF16) for h in halves],
                            axis=1)
        hb = (_silu(_dot(x, wgb_ref[...])) * _dot(x, wub_ref[...])).astype(BF16)
        _store_slabs(o_ref, _pack_pairs(_dot(hb, wdb_ref[...])), EXPERT_BLOCK, nc)

    @pl.when(jnp.logical_not(used))
    def _():
        o_ref[...] = jnp.zeros_like(o_ref)


def _expert_ffn(xs, block_e, n_used, w_gate, w_up, w_down, layer, nc):
    _, _, d, f = w_gate.shape
    nb = xs.shape[0] // (EXPERT_BLOCK * nc)
    wspec = lambda a, b: pl.BlockSpec((None, None, a, b), lambda i, be, nu: (layer, be[i], 0, 0))
    grid_spec = pltpu.PrefetchScalarGridSpec(
        num_scalar_prefetch=2,
        grid=(nb,),
        in_specs=[
            pl.BlockSpec((EXPERT_BLOCK * nc, LANES), lambda i, be, nu: (jnp.minimum(i, nu[0] - 1), 0)),
            wspec(d, f), wspec(d, f), wspec(f, d),
        ],
        out_specs=pl.BlockSpec((EXPERT_BLOCK * nc, LANES), lambda i, be, nu: (i, 0)),
        scratch_shapes=[pltpu.VMEM((d, f), BF16), pltpu.VMEM((d, f), BF16), pltpu.VMEM((f, d), BF16)],
    )
    return pl.pallas_call(
        functools.partial(_expert_ffn_kernel, nc=nc),
        grid_spec=grid_spec,
        out_shape=jax.ShapeDtypeStruct(xs.shape, xs.dtype),
        compiler_params=_cparams(1),
        name="expert_ffn",
    )(block_e, n_used, xs, w_gate, w_up, w_down)


def _combine_kernel(dest_ref, destn_ref, wts_ref, hf_ref, hb_ref, sg_ref, su_ref, sd_ref, g_ref,
                    b_ref, ys_ref, of_ref, ob_ref, buf_ref, sems, *, tm, nc, alpha):
    i = pl.program_id(0)
    slot = i % 2

    def slab_copy(sl, k, r, src):
        dst_row = pl.multiple_of((k * tm + r) * nc, nc)
        return pltpu.make_async_copy(ys_ref.at[pl.ds(pl.multiple_of(src * nc, nc), nc), :],
                                     buf_ref.at[sl, pl.ds(dst_row, nc), :], sems.at[sl])

    def gather(sl, rows_ref):
        def issue(r, carry):
            for k in range(TOP_K):
                slab_copy(sl, k, r, rows_ref[0, 0, r * TOP_K + k]).start(priority=k % 2)
            return carry

        lax.fori_loop(0, tm, issue, 0)

    @pl.when(i == 0)
    def _():
        gather(slot, dest_ref)

    @pl.when(i + 1 < pl.num_programs(0))
    def _():
        gather(1 - slot, destn_ref)

    xb = hb_ref[...]
    shared = _dot((_silu(_dot(xb, sg_ref[...])) * _dot(xb, su_ref[...])).astype(BF16), sd_ref[...])
    z = alpha * hf_ref[...] + shared

    pltpu.make_async_copy(buf_ref.at[slot], buf_ref.at[slot], sems.at[slot]).wait()
    wts = wts_ref[...]
    his = [None] * nc
    los = [None] * nc
    for k in range(TOP_K):
        wk = wts[:, k:k + 1]
        for s in range(nc):
            hi, lo = _unpack_pairs(buf_ref[slot, pl.ds(k * tm * nc + s, tm, stride=nc), :])
            his[s] = wk * hi if k == 0 else his[s] + wk * hi
            los[s] = wk * lo if k == 0 else los[s] + wk * lo
    out = _layer_norm(z + jnp.concatenate(his + los, axis=1), g_ref[...], b_ref[...])
    of_ref[...] = out
    ob_ref[...] = out.astype(BF16)


def _combine(ys, dest, wts, h_f32, h_bf16, sh_gate, sh_up, sh_down, g, b, alpha, nc, tm=128):
    t, d = h_f32.shape
    f = sh_gate.shape[1]
    tm = min(tm, t)
    n = t // tm
    smem = lambda im: pl.BlockSpec((1, 1, tm * TOP_K), im, memory_space=pltpu.SMEM)
    tile = lambda w: pl.BlockSpec((tm, w), lambda i: (i, 0))
    const = lambda shape: pl.BlockSpec(shape, lambda i: (0, 0))
    rows = dest.reshape(n, 1, tm * TOP_K)
    return pl.pallas_call(
        functools.partial(_combine_kernel, tm=tm, nc=nc, alpha=alpha),
        grid=(n,),
        in_specs=[smem(lambda i: (i, 0, 0)), smem(lambda i: (jnp.minimum(i + 1, n - 1), 0, 0)),
                  tile(LANES), tile(d), tile(d),
                  const((d, f)), const((d, f)), const((f, d)), const((1, d)), const((1, d)),
                  pl.BlockSpec(memory_space=pl.ANY)],
        out_specs=[tile(d), tile(d)],
        out_shape=[jax.ShapeDtypeStruct((t, d), F32), jax.ShapeDtypeStruct((t, d), BF16)],
        scratch_shapes=[pltpu.VMEM((2, TOP_K * tm * nc, LANES), ys.dtype),
                        pltpu.SemaphoreType.DMA((2,))],
        compiler_params=_cparams(1),
        name="moe_combine",
    )(rows, rows, wts, h_f32, h_bf16, sh_gate, sh_up, sh_down, g.reshape(1, d), b.reshape(1, d), ys)


def _moe_ln(h_f32, h_bf16, h_slabs, router_w, router_bias, w_gate, w_up, w_down, layer, sh_gate,
            sh_up, sh_down, g, b, alpha):
    t, d = h_f32.shape
    n_exp = router_w.shape[1]
    nc = _slab_rows(d)
    idx, wts, rank, counts = _router(h_f32, router_w, router_bias)
    idx, rank = idx[:, :TOP_K], rank[:, :TOP_K]

    counts = counts[0, :n_exp].astype(jnp.int32)
    padded = (counts + EXPERT_BLOCK - 1) // EXPERT_BLOCK * EXPERT_BLOCK
    pends = jnp.cumsum(padded)
    pstart = jnp.concatenate([jnp.zeros((1,), jnp.int32), pends]).astype(jnp.int32)
    n_blocks = t * TOP_K // EXPERT_BLOCK + n_exp
    block_start = jnp.arange(n_blocks, dtype=jnp.int32) * EXPERT_BLOCK
    block_e = jnp.sum((pends[None, :] <= block_start[:, None]).astype(jnp.int32), axis=1)
    block_e = jnp.minimum(block_e, n_exp - 1)
    n_used = (pends[-1:] // EXPERT_BLOCK).astype(jnp.int32)
    group_start = jnp.sum(jnp.where(idx[:, :, None] == jnp.arange(n_exp, dtype=jnp.int32),
                                    pstart[:n_exp], 0), axis=-1)
    dest = group_start + rank

    xs = _dispatch(h_slabs, dest, pstart, counts, n_blocks * EXPERT_BLOCK, nc)
    ys = _expert_ffn(xs, block_e, n_used, w_gate, w_up, w_down, layer, nc)
    return _combine(ys, dest, wts, h_f32, h_bf16, sh_gate.astype(BF16), sh_up.astype(BF16),
                    sh_down.astype(BF16), g, b, alpha, nc)


def kernel(x, even_w_in, ret_gn_g, rg_conv_w, rg_conv_b, rg_w_a, rg_b_a, rg_w_i, rg_b_i, rg_lambda,
           even_w_out, fox_w_in, fox_b_f, fox_w_out, ln_g, ln_b, router_w, router_bias, exp_w_gate,
           exp_w_up, exp_w_down, sh_w_gate, sh_w_up, sh_w_down):
    batch, seq, d = x.shape
    depth = ln_g.shape[0]
    alpha = (2 * depth) ** 0.25
    ret_width = ret_gn_g.shape[1]
    fox_heads = fox_b_f.shape[1]

    h_f32 = x.reshape(batch * seq, d)
    h_bf16 = h_f32.astype(BF16)
    for layer in range(depth):
        j = layer // 2
        if layer % 2 == 0:
            proj = _matmul(h_bf16, even_w_in[j].astype(BF16), BF16)
            y_ret = _retention(proj, ret_gn_g[j], batch, seq, chunk=256)
            y_rnn = _rglru(proj, rg_conv_w[j], rg_conv_b[j], rg_w_a[j], rg_b_a[j], rg_w_i[j],
                           rg_b_i[j], rg_lambda[j], batch, seq, tt=256)
            w_out = even_w_out[j].astype(BF16)
            ys, ws = [y_ret, y_rnn], [w_out[:ret_width], w_out[ret_width:]]
        else:
            w_in = fox_w_in[j]
            qkv = _matmul(h_bf16, w_in[:, :3 * d].astype(BF16), BF16)
            w_f = jnp.pad(w_in[:, 3 * d:], ((0, 0), (0, LANES - fox_heads))).astype(BF16)
            fl = _matmul(h_bf16, w_f, F32)
            b_f = jnp.pad(fox_b_f[j], (0, LANES - fox_heads)).reshape(1, LANES)
            kc = _forget_key_bias(_logf_cumsum(fl, b_f, batch, seq), batch, seq, fox_heads)
            o = _fox_attention(qkv, kc, batch, seq, fox_heads, tq=1024)
            ys, ws = [o], [fox_w_out[j].astype(BF16)]
        h_f32, h_bf16, h_slabs = _proj_ln(ys, ws, h_f32, ln_g[layer, 0], ln_b[layer, 0], alpha)
        h_f32, h_bf16 = _moe_ln(h_f32, h_bf16, h_slabs, router_w[layer], router_bias[layer],
                                exp_w_gate, exp_w_up, exp_w_down, layer,
                                sh_w_gate[layer], sh_w_up[layer], sh_w_down[layer],
                                ln_g[layer, 1], ln_b[layer, 1], alpha)
    return h_f32.reshape(batch, seq, d)
```

```python
import functools
import math

import numpy as np
import jax
import jax.numpy as jnp
from jax import lax
from jax.experimental import pallas as pl
from jax.experimental.pallas import tpu as pltpu

F32 = jnp.float32
BF16 = jnp.bfloat16

HEAD_DIM = 128
TOP_K = 8
ROUTED_SCALE = 2.5
ROPE_BASE = 10000.0
RG_C = 8.0
LN_EPS = 1e-5
CONV_TAIL = 8

V7X_VMEM_BYTES = 64 * 1024 * 1024
VMEM_LIMIT = V7X_VMEM_BYTES - 8 * 1024 * 1024
LANES = 128

EXPERT_BLOCK = 512


def _cparams(grid_rank):
    return pltpu.CompilerParams(dimension_semantics=("arbitrary",) * grid_rank,
                                vmem_limit_bytes=VMEM_LIMIT)


def _dot(a, b):
    return jnp.dot(a, b, preferred_element_type=F32)


def _silu(x):
    return x * jax.nn.sigmoid(x)


def _softplus(x):
    return jnp.maximum(x, 0.0) + jnp.log1p(jnp.exp(-jnp.abs(x)))


def _layer_norm(z, g, b):
    mu = jnp.mean(z, axis=-1, keepdims=True)
    zc = z - mu
    var = jnp.mean(zc * zc, axis=-1, keepdims=True)
    return zc * lax.rsqrt(var + LN_EPS) * g + b


def _mm_kernel(x_ref, w_ref, o_ref):
    o_ref[...] = _dot(x_ref[...], w_ref[...]).astype(o_ref.dtype)


def _tile(n, target):
    best = LANES
    for c in range(LANES, min(n, target) + 1, LANES):
        if n % c == 0:
            best = c
    assert n % best == 0
    return best


def _matmul(x, w, out_dtype, tm=1024, tn=1024):
    m, k = x.shape
    n = w.shape[1]
    tm, tn = _tile(m, tm), _tile(n, tn)
    return pl.pallas_call(
        _mm_kernel,
        grid=(m // tm, n // tn),
        in_specs=[pl.BlockSpec((tm, k), lambda i, j: (i, 0)),
                  pl.BlockSpec((k, tn), lambda i, j: (0, j))],
        out_specs=pl.BlockSpec((tm, tn), lambda i, j: (i, j)),
        out_shape=jax.ShapeDtypeStruct((m, n), out_dtype),
        compiler_params=_cparams(2),
        name="in_proj",
    )(x, w)


def _slab_rows(d):
    assert d % (2 * LANES) == 0
    return d // (2 * LANES)


def _pack_pairs(x):
    half = x.shape[1] // 2
    hi = lax.bitcast_convert_type(x[:, :half].astype(BF16).astype(F32), jnp.uint32)
    lo = lax.bitcast_convert_type(x[:, half:].astype(BF16).astype(F32), jnp.uint32)
    return hi | (lo >> 16)


def _unpack_pairs(w):
    hi = lax.bitcast_convert_type(w & jnp.uint32(0xFFFF0000), F32)
    lo = lax.bitcast_convert_type(w << 16, F32)
    return hi, lo


def _store_slabs(ref, words, rows, nc):
    for s in range(nc):
        ref[pl.ds(s, rows, stride=nc), :] = words[:, s * LANES:(s + 1) * LANES]


def _proj_ln_kernel(*refs, n_in, alpha, tm, nc):
    ys, ws = refs[:n_in], refs[n_in:2 * n_in]
    h_ref, g_ref, b_ref, of_ref, ob_ref, op_ref = refs[2 * n_in:]
    acc = _dot(ys[0][...], ws[0][...])
    for y_ref, w_ref in zip(ys[1:], ws[1:]):
        acc = acc + _dot(y_ref[...], w_ref[...])
    out = _layer_norm(alpha * h_ref[...] + acc, g_ref[...], b_ref[...])
    of_ref[...] = out
    ob_ref[...] = out.astype(BF16)
    _store_slabs(op_ref, _pack_pairs(out), tm, nc)


def _proj_ln(ys, ws, h, g, b, alpha, tm=256):
    t, d = h.shape
    tm = min(tm, t)
    n_in = len(ys)
    nc = _slab_rows(d)
    in_specs = ([pl.BlockSpec((tm, y.shape[1]), lambda i: (i, 0)) for y in ys]
                + [pl.BlockSpec(w.shape, lambda i: (0, 0)) for w in ws]
                + [pl.BlockSpec((tm, d), lambda i: (i, 0)),
                   pl.BlockSpec((1, d), lambda i: (0, 0)),
                   pl.BlockSpec((1, d), lambda i: (0, 0))])
    return pl.pallas_call(
        functools.partial(_proj_ln_kernel, n_in=n_in, alpha=alpha, tm=tm, nc=nc),
        grid=(t // tm,),
        in_specs=in_specs,
        out_specs=[pl.BlockSpec((tm, d), lambda i: (i, 0)),
                   pl.BlockSpec((tm, d), lambda i: (i, 0)),
                   pl.BlockSpec((tm * nc, LANES), lambda i: (i, 0))],
        out_shape=[jax.ShapeDtypeStruct((t, d), F32), jax.ShapeDtypeStruct((t, d), BF16),
                   jax.ShapeDtypeStruct((t * nc, LANES), jnp.uint32)],
        compiler_params=_cparams(1),
        name="out_proj_ln",
    )(*ys, *ws, h, g.reshape(1, d), b.reshape(1, d))


def _retention_kernel(q_ref, k_ref, v_ref, g_ref, cos_ref, sin_ref, intra_ref, qdec_ref, kdec_ref,
                      gn_ref, o_ref, state_ref, *, heads, chunk_decay):
    @pl.when(pl.program_id(1) == 0)
    def _():
        state_ref[...] = jnp.zeros_like(state_ref)

    cos, sin = cos_ref[...], sin_ref[...]
    half = HEAD_DIM // 2
    for h in range(heads):
        sl = slice(h * HEAD_DIM, (h + 1) * HEAD_DIM)
        q = q_ref[:, sl].astype(F32)
        k = k_ref[:, sl].astype(F32)
        q = q * cos + pltpu.roll(q, half, 1) * sin
        k = (k * cos + pltpu.roll(k, half, 1) * sin) * (HEAD_DIM ** -0.5)
        v = v_ref[:, sl]
        s = lax.dot_general(q.astype(BF16), k.astype(BF16), (((1,), (1,)), ((), ())),
                            preferred_element_type=F32) * intra_ref[h]
        state = state_ref[h]
        y = _dot(s.astype(BF16), v) + _dot((q * qdec_ref[h]).astype(BF16), state.astype(BF16))
        kv = lax.dot_general((k * kdec_ref[h]).astype(BF16), v, (((0,), (0,)), ((), ())),
                             preferred_element_type=F32)
        state_ref[h] = state * chunk_decay[h] + kv
        mu = jnp.mean(y, axis=-1, keepdims=True)
        yc = y - mu
        var = jnp.mean(yc * yc, axis=-1, keepdims=True)
        yn = yc * lax.rsqrt(var + LN_EPS) * gn_ref[:, sl]
        o_ref[:, sl] = (yn * _silu(g_ref[:, sl].astype(F32))).astype(BF16)


def _retention(proj, gn_g, batch, seq, chunk):
    t = proj.shape[0]
    width = gn_g.shape[0]
    heads = width // HEAD_DIM
    chunk = min(chunk, seq)
    n_chunks = seq // chunk
    half = HEAD_DIM // 2

    pos = jnp.arange(seq, dtype=F32)
    inv = ROPE_BASE ** (-jnp.arange(half, dtype=F32) / half)
    ang = pos[:, None] * inv[None, :]
    cos2 = jnp.concatenate([jnp.cos(ang), jnp.cos(ang)], axis=1)
    sin2 = jnp.concatenate([-jnp.sin(ang), jnp.sin(ang)], axis=1)

    log_g = jnp.log1p(-jnp.exp2(-5.0 - jnp.arange(heads, dtype=F32)))
    i = jnp.arange(chunk, dtype=F32)
    rel = i[:, None] - i[None, :]
    intra = jnp.where(rel >= 0, jnp.exp(log_g[:, None, None] * jnp.maximum(rel, 0.0)), 0.0)
    kdec = jnp.exp(log_g[:, None] * (chunk - 1.0 - i)[None, :])
    qdec = jnp.exp(log_g[:, None] * (i + 1.0)[None, :])
    kdec = jnp.broadcast_to(kdec[:, :, None], (heads, chunk, HEAD_DIM))
    qdec = jnp.broadcast_to(qdec[:, :, None], (heads, chunk, HEAD_DIM))
    chunk_decay = tuple(math.exp(math.log1p(-2.0 ** (-5.0 - h)) * chunk) for h in range(heads))

    def col(c):
        return pl.BlockSpec((chunk, width), lambda b, n: (b * n_chunks + n, c))

    const3 = lambda shape: pl.BlockSpec(shape, lambda b, n: (0, 0, 0))
    return pl.pallas_call(
        functools.partial(_retention_kernel, heads=heads, chunk_decay=chunk_decay),
        grid=(batch, n_chunks),
        in_specs=[col(0), col(1), col(2), col(3),
                  pl.BlockSpec((chunk, HEAD_DIM), lambda b, n: (n, 0)),
                  pl.BlockSpec((chunk, HEAD_DIM), lambda b, n: (n, 0)),
                  const3((heads, chunk, chunk)),
                  const3((heads, chunk, HEAD_DIM)),
                  const3((heads, chunk, HEAD_DIM)),
                  pl.BlockSpec((1, width), lambda b, n: (0, 0))],
        out_specs=pl.BlockSpec((chunk, width), lambda b, n: (b * n_chunks + n, 0)),
        out_shape=jax.ShapeDtypeStruct((t, width), BF16),
        scratch_shapes=[pltpu.VMEM((heads, HEAD_DIM, HEAD_DIM), F32)],
        compiler_params=_cparams(2),
        name="retention",
    )(proj, proj, proj, proj, cos2, sin2, intra, qdec, kdec, gn_g.reshape(1, width))


def _rglru_kernel(u_ref, gate_ref, cw_ref, cb_ref, wa_ref, ba_ref, wi_ref, bi_ref, lam_ref,
                  o_ref, uext_ref, hc_ref, *, tt, nblk):
    n = pl.program_id(1)

    @pl.when(n == 0)
    def _():
        uext_ref[0:CONV_TAIL, :] = jnp.zeros((CONV_TAIL, uext_ref.shape[1]), F32)
        hc_ref[...] = jnp.zeros_like(hc_ref)

    @pl.when(n > 0)
    def _():
        uext_ref[0:CONV_TAIL, :] = uext_ref[tt:tt + CONV_TAIL, :]

    u = u_ref[...].astype(F32)
    uext_ref[CONV_TAIL:CONV_TAIL + tt, :] = u
    uc = (cb_ref[...] + cw_ref[3:4, :] * u
          + cw_ref[2:3, :] * uext_ref[CONV_TAIL - 1:CONV_TAIL - 1 + tt, :]
          + cw_ref[1:2, :] * uext_ref[CONV_TAIL - 2:CONV_TAIL - 2 + tt, :]
          + cw_ref[0:1, :] * uext_ref[CONV_TAIL - 3:CONV_TAIL - 3 + tt, :])
    ucb = uc.astype(BF16)
    ra = jnp.concatenate(
        [_dot(ucb[:, i * HEAD_DIM:(i + 1) * HEAD_DIM], wa_ref[i]) for i in range(nblk)], axis=1)
    ia = jnp.concatenate(
        [_dot(ucb[:, i * HEAD_DIM:(i + 1) * HEAD_DIM], wi_ref[i]) for i in range(nblk)], axis=1)
    r = jax.nn.sigmoid(ra + ba_ref[...])
    ig = jax.nn.sigmoid(ia + bi_ref[...])
    log_a = (-RG_C) * r * _softplus(-lam_ref[...])
    a = jnp.exp(log_a)
    bt = jnp.sqrt(-jnp.tanh(log_a) * (a * a + 1.0)) * (ig * uc)

    row = lax.broadcasted_iota(jnp.int32, a.shape, 0)
    d = 1
    while d < tt:
        keep = row >= d
        bt = jnp.where(keep, a * pltpu.roll(bt, d, 0) + bt, bt)
        a = jnp.where(keep, a * pltpu.roll(a, d, 0), a)
        d *= 2
    h = a * hc_ref[...] + bt
    hc_ref[...] = h[tt - 1:tt, :]

    g = gate_ref[...].astype(F32)
    gelu = 0.5 * g * (1.0 + jnp.tanh(math.sqrt(2.0 / math.pi) * (g + 0.044715 * (g * g * g))))
    o_ref[...] = (h * gelu).astype(BF16)


def _rglru(proj, conv_w, conv_b, w_a, b_a, w_i, b_i, lam, batch, seq, tt):
    t = proj.shape[0]
    width = conv_w.shape[1]
    nblk = w_a.shape[0]
    tt = min(tt, seq)
    n_t = seq // tt
    row = lambda x: x.reshape(1, width)
    vec = pl.BlockSpec((1, width), lambda b, n: (0, 0))
    blk = pl.BlockSpec((nblk, HEAD_DIM, HEAD_DIM), lambda b, n: (0, 0, 0))
    return pl.pallas_call(
        functools.partial(_rglru_kernel, tt=tt, nblk=nblk),
        grid=(batch, n_t),
        in_specs=[pl.BlockSpec((tt, width), lambda b, n: (b * n_t + n, 4)),
                  pl.BlockSpec((tt, width), lambda b, n: (b * n_t + n, 5)),
                  pl.BlockSpec((4, width), lambda b, n: (0, 0)),
                  vec, blk, vec, blk, vec, vec],
        out_specs=pl.BlockSpec((tt, width), lambda b, n: (b * n_t + n, 0)),
        out_shape=jax.ShapeDtypeStruct((t, width), BF16),
        scratch_shapes=[pltpu.VMEM((tt + CONV_TAIL, width), F32), pltpu.VMEM((1, width), F32)],
        compiler_params=_cparams(2),
        name="rglru",
    )(proj, proj, conv_w, row(conv_b), w_a.astype(BF16), row(b_a), w_i.astype(BF16), row(b_i),
      row(lam))


FORGET_SPLIT = 3


def _bf16_head(x):
    bits = lax.bitcast_convert_type(x, jnp.uint32) & jnp.uint32(0xFFFF0000)
    return lax.bitcast_convert_type(bits, F32)


def _logf_cumsum_kernel(fl_ref, bf_ref, *o_refs, seq):
    z = fl_ref[...] + bf_ref[...]
    c = jnp.minimum(z, 0.0) - jnp.log1p(jnp.exp(-jnp.abs(z)))
    row = lax.broadcasted_iota(jnp.int32, c.shape, 0)
    d = 1
    while d < seq:
        c = c + jnp.where(row >= d, pltpu.roll(c, d, 0), 0.0)
        d *= 2
    x = c * (-(HEAD_DIM ** 0.5))
    for o_ref in o_refs:
        piece = _bf16_head(x)
        o_ref[...] = piece.astype(BF16)
        x = x - piece


def _logf_cumsum(fl, b_f, batch, seq):
    tile = pl.BlockSpec((seq, LANES), lambda b: (b, 0))
    return pl.pallas_call(
        functools.partial(_logf_cumsum_kernel, seq=seq),
        grid=(batch,),
        in_specs=[tile, pl.BlockSpec((1, LANES), lambda b: (0, 0))],
        out_specs=[tile] * FORGET_SPLIT,
        out_shape=[jax.ShapeDtypeStruct(fl.shape, BF16)] * FORGET_SPLIT,
        compiler_params=_cparams(1),
        name="logf_cumsum",
    )(fl, b_f)


def _fox_kernel(q_ref, k_ref, v_ref, kc_ref, o_ref, kext_ref, vext_ref, m_ref, l_ref, acc_ref, *,
                tq, seq, group):
    qi = pl.program_id(2)
    head = lambda g: slice(g * HEAD_DIM, (g + 1) * HEAD_DIM)

    @pl.when(qi == 0)
    def _():
        for g in range(group):
            kext_ref[g, :, :HEAD_DIM] = k_ref[:, head(g)]
            kext_ref[g, :, HEAD_DIM:] = kc_ref[g]
            vext_ref[g, :, :HEAD_DIM] = v_ref[:, head(g)]
            vext_ref[g, :, HEAD_DIM:] = jnp.ones((seq, HEAD_DIM), BF16)

    lane = lax.broadcasted_iota(jnp.int32, (tq, HEAD_DIM), 1)
    ones = jnp.where(lane < FORGET_SPLIT, 1.0, 0.0).astype(BF16)
    q2 = [jnp.concatenate([q_ref[:, head(g)], ones], axis=1) for g in range(group)]
    c = (HEAD_DIM ** -0.5) * math.log2(math.e)
    reps = tq // HEAD_DIM

    def logits(g, kb):
        off = pl.multiple_of(kb * tq, tq)
        return lax.dot_general(q2[g], kext_ref[g, pl.ds(off, tq), :], (((1,), (1,)), ((), ())),
                               preferred_element_type=F32)

    def fold(g, kb, masked):
        u = logits(g, kb)
        if masked:
            row = lax.broadcasted_iota(jnp.int32, u.shape, 0)
            col = lax.broadcasted_iota(jnp.int32, u.shape, 1)
            u = jnp.where(col <= row, u, -jnp.inf)
        m = m_ref[g]
        m_new = jnp.maximum(m, jnp.broadcast_to(jnp.max(u, axis=-1, keepdims=True), m.shape))
        p = jnp.exp2((u - jnp.concatenate([m_new] * reps, axis=1)) * c)
        alpha = jnp.exp2((m - m_new) * c)
        off = pl.multiple_of(kb * tq, tq)
        pv = _dot(p.astype(BF16), vext_ref[g, pl.ds(off, tq), :])
        m_ref[g] = m_new
        l_ref[g] = alpha * l_ref[g] + pv[:, HEAD_DIM:]
        acc_ref[g] = alpha * acc_ref[g] + pv[:, :HEAD_DIM]

    m_ref[...] = jnp.full(m_ref.shape, -jnp.inf, F32)
    l_ref[...] = jnp.zeros(l_ref.shape, F32)
    acc_ref[...] = jnp.zeros(acc_ref.shape, F32)

    def body(kb, carry):
        for g in range(group):
            fold(g, kb, False)
        return carry

    lax.fori_loop(0, qi, body, 0)
    for g in range(group):
        fold(g, qi, True)
        o_ref[:, head(g)] = (acc_ref[g] / l_ref[g]).astype(BF16)


def _fox_attention(qkv, kc, batch, seq, heads, tq, group=4):
    t = qkv.shape[0]
    tq = min(tq, max(HEAD_DIM, seq // 4))
    nq = seq // tq
    assert tq % HEAD_DIM == 0 and seq % tq == 0 and heads % group == 0
    hg = heads // group
    w = group * HEAD_DIM
    return pl.pallas_call(
        functools.partial(_fox_kernel, tq=tq, seq=seq, group=group),
        grid=(batch, hg, nq),
        in_specs=[
            pl.BlockSpec((tq, w), lambda b, h, qi: (b * nq + qi, h)),
            pl.BlockSpec((seq, w), lambda b, h, qi: (b, hg + h)),
            pl.BlockSpec((seq, w), lambda b, h, qi: (b, 2 * hg + h)),
            pl.BlockSpec((group, seq, HEAD_DIM), lambda b, h, qi: (b * hg + h, 0, 0)),
        ],
        out_specs=pl.BlockSpec((tq, w), lambda b, h, qi: (b * nq + qi, h)),
        out_shape=jax.ShapeDtypeStruct((t, heads * HEAD_DIM), BF16),
        scratch_shapes=[pltpu.VMEM((group, seq, 2 * HEAD_DIM), BF16),
                        pltpu.VMEM((group, seq, 2 * HEAD_DIM), BF16)]
        + [pltpu.VMEM((group, tq, HEAD_DIM), F32)] * 3,
        compiler_params=_cparams(3),
        name="fox_attention",
    )(qkv, qkv, qkv, kc)


def _forget_key_bias(pieces, batch, seq, heads):
    kc = jnp.stack([p.reshape(batch, seq, LANES)[:, :, :heads] for p in pieces], axis=-1)
    kc = jnp.transpose(kc, (0, 2, 1, 3)).reshape(batch * heads, seq, FORGET_SPLIT)
    return jnp.pad(kc, ((0, 0), (0, 0), (0, HEAD_DIM - FORGET_SPLIT)))


def _router_kernel(h_ref, rw_ref, rb_ref, idx_ref, wts_ref, rank_ref, cnt_ref, *, tm):
    @pl.when(pl.program_id(0) == 0)
    def _():
        cnt_ref[...] = jnp.zeros_like(cnt_ref)

    h = h_ref[...]
    h_hi = _bf16_head(h)
    first_order = _dot(h_hi.astype(BF16), rw_ref[...])
    logits = (first_order[:, :LANES] + first_order[:, LANES:]
              + _dot((h - h_hi).astype(BF16), rw_ref[:, :LANES]))
    scores = jax.nn.sigmoid(logits)
    sel = scores + rb_ref[...]
    lane = lax.broadcasted_iota(jnp.int32, sel.shape, 1)
    lane_f = lane.astype(F32)
    idx = jnp.zeros(sel.shape, F32)
    wts = jnp.zeros(sel.shape, F32)
    chosen = jnp.zeros(sel.shape, F32)
    hits = []
    for k in range(TOP_K):
        top = jnp.max(sel, axis=-1, keepdims=True)
        first = jnp.min(jnp.where(sel == top, lane_f, float(LANES)), axis=-1, keepdims=True)
        hit = lane_f == first
        hits.append(hit)
        idx = jnp.where(lane == k, first, idx)
        wts = jnp.where(lane == k, jnp.sum(jnp.where(hit, scores, 0.0), axis=-1, keepdims=True), wts)
        chosen = jnp.where(hit, 1.0, chosen)
        sel = jnp.where(hit, -jnp.inf, sel)
    wts = wts / jnp.sum(wts, axis=-1, keepdims=True) * ROUTED_SCALE

    r = lax.broadcasted_iota(jnp.int32, (tm, tm), 0)
    c = lax.broadcasted_iota(jnp.int32, (tm, tm), 1)
    before = jnp.where(c < r, 1.0, 0.0).astype(BF16)
    rank_all = _dot(before, chosen.astype(BF16)) + cnt_ref[...]
    rank = jnp.zeros(sel.shape, F32)
    for k in range(TOP_K):
        rank = jnp.where(lane == k, jnp.sum(jnp.where(hits[k], rank_all, 0.0), axis=-1, keepdims=True),
                         rank)
    cnt_ref[...] = cnt_ref[...] + jnp.sum(chosen, axis=0, keepdims=True)
    idx_ref[...] = idx.astype(jnp.int32)
    wts_ref[...] = wts
    rank_ref[...] = rank.astype(jnp.int32)


def _router(h, router_w, router_bias, tm=256):
    t, d = h.shape
    e = router_w.shape[1]
    tm = min(tm, t)
    rw = jnp.pad(router_w, ((0, 0), (0, LANES - e)))
    rw_hi = _bf16_head(rw)
    rw = jnp.concatenate([rw_hi.astype(BF16), (rw - rw_hi).astype(BF16)], axis=1)
    rb = jnp.pad(router_bias.astype(F32), (0, LANES - e), constant_values=-jnp.inf).reshape(1, LANES)
    tile = pl.BlockSpec((tm, LANES), lambda i: (i, 0))
    return pl.pallas_call(
        functools.partial(_router_kernel, tm=tm),
        grid=(t // tm,),
        in_specs=[pl.BlockSpec((tm, d), lambda i: (i, 0)),
                  pl.BlockSpec((d, 2 * LANES), lambda i: (0, 0)),
                  pl.BlockSpec((1, LANES), lambda i: (0, 0))],
        out_specs=[tile, tile, tile, pl.BlockSpec((1, LANES), lambda i: (0, 0))],
        out_shape=[jax.ShapeDtypeStruct((t, LANES), jnp.int32),
                   jax.ShapeDtypeStruct((t, LANES), F32),
                   jax.ShapeDtypeStruct((t, LANES), jnp.int32),
                   jax.ShapeDtypeStruct((1, LANES), F32)],
        compiler_params=_cparams(1),
        name="router_topk",
    )(h, rw, rb)


def _dispatch_kernel(pstart_ref, cnt_ref, dest_ref, x_ref, xs_ref, zero_ref, sem, *, tm, n_exp, nc,
                     n_blocks):
    i = pl.program_id(0)
    blk = EXPERT_BLOCK * nc

    def row_copy(r, dst):
        return pltpu.make_async_copy(x_ref.at[pl.ds(pl.multiple_of(r * nc, nc), nc), :],
                                     xs_ref.at[pl.ds(pl.multiple_of(dst * nc, nc), nc), :], sem)

    def zero_run(dst, n):
        return pltpu.make_async_copy(zero_ref.at[pl.ds(0, n * nc), :],
                                     xs_ref.at[pl.ds(pl.multiple_of(dst * nc, nc), n * nc), :], sem)

    def zero_block(b):
        return pltpu.make_async_copy(zero_ref, xs_ref.at[pl.ds(pl.multiple_of(b * blk, blk), blk), :],
                                     sem)

    def pad_rows(e, do):
        first = pstart_ref[e] + cnt_ref[e]
        n = pstart_ref[e + 1] - first
        bit = EXPERT_BLOCK // 2
        while bit:
            pl.when((n & bit) != 0)(functools.partial(do, first, bit))
            first = first + (n & bit)
            bit //= 2

    def first_unused_block():
        return lax.div(pstart_ref[n_exp], EXPERT_BLOCK)

    @pl.when(i == 0)
    def _():
        zero_ref[...] = jnp.zeros_like(zero_ref)

        def per_expert(e, carry):
            pad_rows(e, lambda first, n: zero_run(first, n).start())
            return carry

        lax.fori_loop(0, n_exp, per_expert, 0)
        lax.fori_loop(first_unused_block(), n_blocks, lambda b, c: (zero_block(b).start(), c)[1], 0)

    def issue(r, carry):
        for k in range(TOP_K):
            row_copy(r, dest_ref[0, 0, r * TOP_K + k]).start(priority=k % 2)
        return carry

    lax.fori_loop(0, tm, issue, 0)
    all_rows = xs_ref.at[pl.ds(0, tm * TOP_K * nc), :]
    pltpu.make_async_copy(all_rows, all_rows, sem).wait()

    @pl.when(i == 0)
    def _():
        def per_expert(e, carry):
            pad_rows(e, lambda first, n: zero_run(0, n).wait())
            return carry

        lax.fori_loop(0, n_exp, per_expert, 0)
        lax.fori_loop(first_unused_block(), n_blocks, lambda b, c: (zero_block(0).wait(), c)[1], 0)


def _dispatch(xp, dest, pstart, counts, n_rows, nc, tm=256):
    t = xp.shape[0] // nc
    tm = min(tm, t)
    n_exp = counts.shape[0]
    assert n_rows >= tm * TOP_K
    grid_spec = pltpu.PrefetchScalarGridSpec(
        num_scalar_prefetch=2,
        grid=(t // tm,),
        in_specs=[pl.BlockSpec((1, 1, tm * TOP_K), lambda i, ps, cn: (i, 0, 0),
                               memory_space=pltpu.SMEM),
                  pl.BlockSpec((tm * nc, LANES), lambda i, ps, cn: (i, 0))],
        out_specs=pl.BlockSpec(memory_space=pl.ANY),
        scratch_shapes=[pltpu.VMEM((EXPERT_BLOCK * nc, LANES), xp.dtype),
                        pltpu.SemaphoreType.DMA(())],
    )
    assert n_rows % EXPERT_BLOCK == 0
    return pl.pallas_call(
        functools.partial(_dispatch_kernel, tm=tm, n_exp=n_exp, nc=nc,
                          n_blocks=n_rows // EXPERT_BLOCK),
        grid_spec=grid_spec,
        out_shape=jax.ShapeDtypeStruct((n_rows * nc, LANES), xp.dtype),
        compiler_params=_cparams(1),
        name="moe_dispatch",
    )(pstart, counts, dest.reshape(t // tm, 1, tm * TOP_K), xp)


def _expert_ffn_kernel(be_ref, nb_ref, x_ref, wg_ref, wu_ref, wd_ref, o_ref, wgb_ref, wub_ref,
                       wdb_ref, *, nc):
    i = pl.program_id(0)
    used = i < nb_ref[0]
    new_expert = jnp.logical_or(i == 0, be_ref[i] != be_ref[jnp.maximum(i - 1, 0)])

    @pl.when(jnp.logical_and(used, new_expert))
    def _():
        wgb_ref[...] = wg_ref[...].astype(BF16)
        wub_ref[...] = wu_ref[...].astype(BF16)
        wdb_ref[...] = wd_ref[...].astype(BF16)

    @pl.when(used)
    def _():
        halves = [_unpack_pairs(x_ref[pl.ds(s, EXPERT_BLOCK, stride=nc), :]) for s in range(nc)]
        x = jnp.concatenate([h[0].astype(BF16) for h in halves] + [h[1].astype(BF16) for h in halves],
                            axis=1)
        hb = (_silu(_dot(x, wgb_ref[...])) * _dot(x, wub_ref[...])).astype(BF16)
        _store_slabs(o_ref, _pack_pairs(_dot(hb, wdb_ref[...])), EXPERT_BLOCK, nc)

    @pl.when(jnp.logical_not(used))
    def _():
        o_ref[...] = jnp.zeros_like(o_ref)


def _expert_ffn(xs, block_e, n_used, w_gate, w_up, w_down, layer, nc):
    _, _, d, f = w_gate.shape
    nb = xs.shape[0] // (EXPERT_BLOCK * nc)
    wspec = lambda a, b: pl.BlockSpec((None, None, a, b), lambda i, be, nu: (layer, be[i], 0, 0))
    grid_spec = pltpu.PrefetchScalarGridSpec(
        num_scalar_prefetch=2,
        grid=(nb,),
        in_specs=[
            pl.BlockSpec((EXPERT_BLOCK * nc, LANES), lambda i, be, nu: (jnp.minimum(i, nu[0] - 1), 0)),
            wspec(d, f), wspec(d, f), wspec(f, d),
        ],
        out_specs=pl.BlockSpec((EXPERT_BLOCK * nc, LANES), lambda i, be, nu: (i, 0)),
        scratch_shapes=[pltpu.VMEM((d, f), BF16), pltpu.VMEM((d, f), BF16), pltpu.VMEM((f, d), BF16)],
    )
    return pl.pallas_call(
        functools.partial(_expert_ffn_kernel, nc=nc),
        grid_spec=grid_spec,
        out_shape=jax.ShapeDtypeStruct(xs.shape, xs.dtype),
        compiler_params=_cparams(1),
        name="expert_ffn",
    )(block_e, n_used, xs, w_gate, w_up, w_down)


def _combine_kernel(dest_ref, destn_ref, wts_ref, hf_ref, hb_ref, sg_ref, su_ref, sd_ref, g_ref,
                    b_ref, ys_ref, of_ref, ob_ref, buf_ref, sems, *, tm, nc, alpha):
    i = pl.program_id(0)
    slot = i % 2

    def slab_copy(sl, k, r, src):
        dst_row = pl.multiple_of((k * tm + r) * nc, nc)
        return pltpu.make_async_copy(ys_ref.at[pl.ds(pl.multiple_of(src * nc, nc), nc), :],
                                     buf_ref.at[sl, pl.ds(dst_row, nc), :], sems.at[sl])

    def gather(sl, rows_ref):
        def issue(r, carry):
            for k in range(TOP_K):
                slab_copy(sl, k, r, rows_ref[0, 0, r * TOP_K + k]).start(priority=k % 2)
            return carry

        lax.fori_loop(0, tm, issue, 0)

    @pl.when(i == 0)
    def _():
        gather(slot, dest_ref)

    @pl.when(i + 1 < pl.num_programs(0))
    def _():
        gather(1 - slot, destn_ref)

    xb = hb_ref[...]
    shared = _dot((_silu(_dot(xb, sg_ref[...])) * _dot(xb, su_ref[...])).astype(BF16), sd_ref[...])
    z = alpha * hf_ref[...] + shared

    pltpu.make_async_copy(buf_ref.at[slot], buf_ref.at[slot], sems.at[slot]).wait()
    wts = wts_ref[...]
    his = [None] * nc
    los = [None] * nc
    for k in range(TOP_K):
        wk = wts[:, k:k + 1]
        for s in range(nc):
            hi, lo = _unpack_pairs(buf_ref[slot, pl.ds(k * tm * nc + s, tm, stride=nc), :])
            his[s] = wk * hi if k == 0 else his[s] + wk * hi
            los[s] = wk * lo if k == 0 else los[s] + wk * lo
    out = _layer_norm(z + jnp.concatenate(his + los, axis=1), g_ref[...], b_ref[...])
    of_ref[...] = out
    ob_ref[...] = out.astype(BF16)


def _combine(ys, dest, wts, h_f32, h_bf16, sh_gate, sh_up, sh_down, g, b, alpha, nc, tm=128):
    t, d = h_f32.shape
    f = sh_gate.shape[1]
    tm = min(tm, t)
    n = t // tm
    smem = lambda im: pl.BlockSpec((1, 1, tm * TOP_K), im, memory_space=pltpu.SMEM)
    tile = lambda w: pl.BlockSpec((tm, w), lambda i: (i, 0))
    const = lambda shape: pl.BlockSpec(shape, lambda i: (0, 0))
    rows = dest.reshape(n, 1, tm * TOP_K)
    return pl.pallas_call(
        functools.partial(_combine_kernel, tm=tm, nc=nc, alpha=alpha),
        grid=(n,),
        in_specs=[smem(lambda i: (i, 0, 0)), smem(lambda i: (jnp.minimum(i + 1, n - 1), 0, 0)),
                  tile(LANES), tile(d), tile(d),
                  const((d, f)), const((d, f)), const((f, d)), const((1, d)), const((1, d)),
                  pl.BlockSpec(memory_space=pl.ANY)],
        out_specs=[tile(d), tile(d)],
        out_shape=[jax.ShapeDtypeStruct((t, d), F32), jax.ShapeDtypeStruct((t, d), BF16)],
        scratch_shapes=[pltpu.VMEM((2, TOP_K * tm * nc, LANES), ys.dtype),
                        pltpu.SemaphoreType.DMA((2,))],
        compiler_params=_cparams(1),
        name="moe_combine",
    )(rows, rows, wts, h_f32, h_bf16, sh_gate, sh_up, sh_down, g.reshape(1, d), b.reshape(1, d), ys)


def _moe_ln(h_f32, h_bf16, h_slabs, router_w, router_bias, w_gate, w_up, w_down, layer, sh_gate,
            sh_up, sh_down, g, b, alpha):
    t, d = h_f32.shape
    n_exp = router_w.shape[1]
    nc = _slab_rows(d)
    idx, wts, rank, counts = _router(h_f32, router_w, router_bias)
    idx, rank = idx[:, :TOP_K], rank[:, :TOP_K]

    counts = counts[0, :n_exp].astype(jnp.int32)
    padded = (counts + EXPERT_BLOCK - 1) // EXPERT_BLOCK * EXPERT_BLOCK
    pends = jnp.cumsum(padded)
    pstart = jnp.concatenate([jnp.zeros((1,), jnp.int32), pends]).astype(jnp.int32)
    n_blocks = t * TOP_K // EXPERT_BLOCK + n_exp
    block_start = jnp.arange(n_blocks, dtype=jnp.int32) * EXPERT_BLOCK
    block_e = jnp.sum((pends[None, :] <= block_start[:, None]).astype(jnp.int32), axis=1)
    block_e = jnp.minimum(block_e, n_exp - 1)
    n_used = (pends[-1:] // EXPERT_BLOCK).astype(jnp.int32)
    group_start = jnp.sum(jnp.where(idx[:, :, None] == jnp.arange(n_exp, dtype=jnp.int32),
                                    pstart[:n_exp], 0), axis=-1)
    dest = group_start + rank

    xs = _dispatch(h_slabs, dest, pstart, counts, n_blocks * EXPERT_BLOCK, nc)
    ys = _expert_ffn(xs, block_e, n_used, w_gate, w_up, w_down, layer, nc)
    return _combine(ys, dest, wts, h_f32, h_bf16, sh_gate.astype(BF16), sh_up.astype(BF16),
                    sh_down.astype(BF16), g, b, alpha, nc)


def kernel(x, even_w_in, ret_gn_g, rg_conv_w, rg_conv_b, rg_w_a, rg_b_a, rg_w_i, rg_b_i, rg_lambda,
           even_w_out, fox_w_in, fox_b_f, fox_w_out, ln_g, ln_b, router_w, router_bias, exp_w_gate,
           exp_w_up, exp_w_down, sh_w_gate, sh_w_up, sh_w_down):
    batch, seq, d = x.shape
    depth = ln_g.shape[0]
    alpha = (2 * depth) ** 0.25
    ret_width = ret_gn_g.shape[1]
    fox_heads = fox_b_f.shape[1]

    h_f32 = x.reshape(batch * seq, d)
    h_bf16 = h_f32.astype(BF16)
    for layer in range(depth):
        j = layer // 2
        if layer % 2 == 0:
            proj = _matmul(h_bf16, even_w_in[j].astype(BF16), BF16)
            y_ret = _retention(proj, ret_gn_g[j], batch, seq, chunk=256)
            y_rnn = _rglru(proj, rg_conv_w[j], rg_conv_b[j], rg_w_a[j], rg_b_a[j], rg_w_i[j],
                           rg_b_i[j], rg_lambda[j], batch, seq, tt=256)
            w_out = even_w_out[j].astype(BF16)
            ys, ws = [y_ret, y_rnn], [w_out[:ret_width], w_out[ret_width:]]
        else:
            w_in = fox_w_in[j]
            qkv = _matmul(h_bf16, w_in[:, :3 * d].astype(BF16), BF16)
            w_f = jnp.pad(w_in[:, 3 * d:], ((0, 0), (0, LANES - fox_heads))).astype(BF16)
            fl = _matmul(h_bf16, w_f, F32)
            b_f = jnp.pad(fox_b_f[j], (0, LANES - fox_heads)).reshape(1, LANES)
            kc = _forget_key_bias(_logf_cumsum(fl, b_f, batch, seq), batch, seq, fox_heads)
            o = _fox_attention(qkv, kc, batch, seq, fox_heads, tq=512)
            ys, ws = [o], [fox_w_out[j].astype(BF16)]
        h_f32, h_bf16, h_slabs = _proj_ln(ys, ws, h_f32, ln_g[layer, 0], ln_b[layer, 0], alpha)
        h_f32, h_bf16 = _moe_ln(h_f32, h_bf16, h_slabs, router_w[layer], router_bias[layer],
                                exp_w_gate, exp_w_up, exp_w_down, layer,
                                sh_w_gate[layer], sh_w_up[layer], sh_w_down[layer],
                                ln_g[layer, 1], ln_b[layer, 1], alpha)
    return h_f32.reshape(batch, seq, d)
```

```python
import functools
import math

import numpy as np
import jax
import jax.numpy as jnp
from jax import lax
from jax.experimental import pallas as pl
from jax.experimental.pallas import tpu as pltpu

F32 = jnp.float32
BF16 = jnp.bfloat16

HEAD_DIM = 128
TOP_K = 8
ROUTED_SCALE = 2.5
ROPE_BASE = 10000.0
RG_C = 8.0
LN_EPS = 1e-5
CONV_TAIL = 8

V7X_VMEM_BYTES = 64 * 1024 * 1024
VMEM_LIMIT = V7X_VMEM_BYTES - 8 * 1024 * 1024
LANES = 128
SUBLANES = 8

EXPERT_BLOCK = 512


def _cparams(grid_rank):
    return pltpu.CompilerParams(dimension_semantics=("arbitrary",) * grid_rank,
                                vmem_limit_bytes=VMEM_LIMIT)


def _dot(a, b):
    return jnp.dot(a, b, preferred_element_type=F32)


def _silu(x):
    return x * jax.nn.sigmoid(x)


def _softplus(x):
    return jnp.maximum(x, 0.0) + jnp.log1p(jnp.exp(-jnp.abs(x)))


def _layer_norm(z, g, b):
    mu = jnp.mean(z, axis=-1, keepdims=True)
    zc = z - mu
    var = jnp.mean(zc * zc, axis=-1, keepdims=True)
    return zc * lax.rsqrt(var + LN_EPS) * g + b


def _mm_kernel(x_ref, w_ref, o_ref):
    o_ref[...] = _dot(x_ref[...], w_ref[...]).astype(o_ref.dtype)


def _tile(n, target):
    best = LANES
    for c in range(LANES, min(n, target) + 1, LANES):
        if n % c == 0:
            best = c
    assert n % best == 0
    return best


def _matmul(x, w, out_dtype, tm=1024, tn=1024):
    m, k = x.shape
    n = w.shape[1]
    tm, tn = _tile(m, tm), _tile(n, tn)
    return pl.pallas_call(
        _mm_kernel,
        grid=(m // tm, n // tn),
        in_specs=[pl.BlockSpec((tm, k), lambda i, j: (i, 0)),
                  pl.BlockSpec((k, tn), lambda i, j: (0, j))],
        out_specs=pl.BlockSpec((tm, tn), lambda i, j: (i, j)),
        out_shape=jax.ShapeDtypeStruct((m, n), out_dtype),
        compiler_params=_cparams(2),
        name="in_proj",
    )(x, w)


def _slab_rows(d):
    assert d % (2 * LANES) == 0
    return d // (2 * LANES)


def _pack_pairs(x):
    half = x.shape[1] // 2
    hi = lax.bitcast_convert_type(x[:, :half].astype(BF16).astype(F32), jnp.uint32)
    lo = lax.bitcast_convert_type(x[:, half:].astype(BF16).astype(F32), jnp.uint32)
    return hi | (lo >> 16)


def _unpack_pairs(w):
    hi = lax.bitcast_convert_type(w & jnp.uint32(0xFFFF0000), F32)
    lo = lax.bitcast_convert_type(w << 16, F32)
    return hi, lo


def _store_slabs(ref, words, rows, nc):
    for s in range(nc):
        ref[pl.ds(s, rows, stride=nc), :] = words[:, s * LANES:(s + 1) * LANES]


def _proj_ln_kernel(*refs, n_in, alpha, tm, nc):
    ys, ws = refs[:n_in], refs[n_in:2 * n_in]
    h_ref, g_ref, b_ref, of_ref, ob_ref, op_ref = refs[2 * n_in:]
    acc = _dot(ys[0][...], ws[0][...])
    for y_ref, w_ref in zip(ys[1:], ws[1:]):
        acc = acc + _dot(y_ref[...], w_ref[...])
    out = _layer_norm(alpha * h_ref[...] + acc, g_ref[...], b_ref[...])
    of_ref[...] = out
    ob_ref[...] = out.astype(BF16)
    _store_slabs(op_ref, _pack_pairs(out), tm, nc)


def _proj_ln(ys, ws, h, g, b, alpha, tm=256):
    t, d = h.shape
    tm = min(tm, t)
    n_in = len(ys)
    nc = _slab_rows(d)
    in_specs = ([pl.BlockSpec((tm, y.shape[1]), lambda i: (i, 0)) for y in ys]
                + [pl.BlockSpec(w.shape, lambda i: (0, 0)) for w in ws]
                + [pl.BlockSpec((tm, d), lambda i: (i, 0)),
                   pl.BlockSpec((1, d), lambda i: (0, 0)),
                   pl.BlockSpec((1, d), lambda i: (0, 0))])
    return pl.pallas_call(
        functools.partial(_proj_ln_kernel, n_in=n_in, alpha=alpha, tm=tm, nc=nc),
        grid=(t // tm,),
        in_specs=in_specs,
        out_specs=[pl.BlockSpec((tm, d), lambda i: (i, 0)),
                   pl.BlockSpec((tm, d), lambda i: (i, 0)),
                   pl.BlockSpec((tm * nc, LANES), lambda i: (i, 0))],
        out_shape=[jax.ShapeDtypeStruct((t, d), F32), jax.ShapeDtypeStruct((t, d), BF16),
                   jax.ShapeDtypeStruct((t * nc, LANES), jnp.uint32)],
        compiler_params=_cparams(1),
        name="out_proj_ln",
    )(*ys, *ws, h, g.reshape(1, d), b.reshape(1, d))


def _retention_kernel(q_ref, k_ref, v_ref, g_ref, cos_ref, sin_ref, intra_ref, qdec_ref, kdec_ref,
                      gn_ref, o_ref, state_ref, *, heads, chunk_decay):
    @pl.when(pl.program_id(1) == 0)
    def _():
        state_ref[...] = jnp.zeros_like(state_ref)

    cos, sin = cos_ref[...], sin_ref[...]
    half = HEAD_DIM // 2
    for h in range(heads):
        sl = slice(h * HEAD_DIM, (h + 1) * HEAD_DIM)
        q = q_ref[:, sl].astype(F32)
        k = k_ref[:, sl].astype(F32)
        q = q * cos + pltpu.roll(q, half, 1) * sin
        k = (k * cos + pltpu.roll(k, half, 1) * sin) * (HEAD_DIM ** -0.5)
        v = v_ref[:, sl]
        s = lax.dot_general(q.astype(BF16), k.astype(BF16), (((1,), (1,)), ((), ())),
                            preferred_element_type=F32) * intra_ref[h]
        state = state_ref[h]
        y = _dot(s.astype(BF16), v) + _dot((q * qdec_ref[h]).astype(BF16), state.astype(BF16))
        kv = lax.dot_general((k * kdec_ref[h]).astype(BF16), v, (((0,), (0,)), ((), ())),
                             preferred_element_type=F32)
        state_ref[h] = state * chunk_decay[h] + kv
        mu = jnp.mean(y, axis=-1, keepdims=True)
        yc = y - mu
        var = jnp.mean(yc * yc, axis=-1, keepdims=True)
        yn = yc * lax.rsqrt(var + LN_EPS) * gn_ref[:, sl]
        o_ref[:, sl] = (yn * _silu(g_ref[:, sl].astype(F32))).astype(BF16)


def _retention(proj, gn_g, batch, seq, chunk):
    t = proj.shape[0]
    width = gn_g.shape[0]
    heads = width // HEAD_DIM
    chunk = min(chunk, seq)
    n_chunks = seq // chunk
    half = HEAD_DIM // 2

    pos = jnp.arange(seq, dtype=F32)
    inv = ROPE_BASE ** (-jnp.arange(half, dtype=F32) / half)
    ang = pos[:, None] * inv[None, :]
    cos2 = jnp.concatenate([jnp.cos(ang), jnp.cos(ang)], axis=1)
    sin2 = jnp.concatenate([-jnp.sin(ang), jnp.sin(ang)], axis=1)

    log_g = jnp.log1p(-jnp.exp2(-5.0 - jnp.arange(heads, dtype=F32)))
    i = jnp.arange(chunk, dtype=F32)
    rel = i[:, None] - i[None, :]
    intra = jnp.where(rel >= 0, jnp.exp(log_g[:, None, None] * jnp.maximum(rel, 0.0)), 0.0)
    kdec = jnp.exp(log_g[:, None] * (chunk - 1.0 - i)[None, :])
    qdec = jnp.exp(log_g[:, None] * (i + 1.0)[None, :])
    kdec = jnp.broadcast_to(kdec[:, :, None], (heads, chunk, HEAD_DIM))
    qdec = jnp.broadcast_to(qdec[:, :, None], (heads, chunk, HEAD_DIM))
    chunk_decay = tuple(math.exp(math.log1p(-2.0 ** (-5.0 - h)) * chunk) for h in range(heads))

    def col(c):
        return pl.BlockSpec((chunk, width), lambda b, n: (b * n_chunks + n, c))

    const3 = lambda shape: pl.BlockSpec(shape, lambda b, n: (0, 0, 0))
    return pl.pallas_call(
        functools.partial(_retention_kernel, heads=heads, chunk_decay=chunk_decay),
        grid=(batch, n_chunks),
        in_specs=[col(0), col(1), col(2), col(3),
                  pl.BlockSpec((chunk, HEAD_DIM), lambda b, n: (n, 0)),
                  pl.BlockSpec((chunk, HEAD_DIM), lambda b, n: (n, 0)),
                  const3((heads, chunk, chunk)),
                  const3((heads, chunk, HEAD_DIM)),
                  const3((heads, chunk, HEAD_DIM)),
                  pl.BlockSpec((1, width), lambda b, n: (0, 0))],
        out_specs=pl.BlockSpec((chunk, width), lambda b, n: (b * n_chunks + n, 0)),
        out_shape=jax.ShapeDtypeStruct((t, width), BF16),
        scratch_shapes=[pltpu.VMEM((heads, HEAD_DIM, HEAD_DIM), F32)],
        compiler_params=_cparams(2),
        name="retention",
    )(proj, proj, proj, proj, cos2, sin2, intra, qdec, kdec, gn_g.reshape(1, width))


def _rglru_kernel(u_ref, gate_ref, cw_ref, cb_ref, wa_ref, ba_ref, wi_ref, bi_ref, lam_ref,
                  o_ref, uext_ref, hc_ref, *, tt, nblk):
    n = pl.program_id(1)

    @pl.when(n == 0)
    def _():
        uext_ref[0:CONV_TAIL, :] = jnp.zeros((CONV_TAIL, uext_ref.shape[1]), F32)
        hc_ref[...] = jnp.zeros_like(hc_ref)

    @pl.when(n > 0)
    def _():
        uext_ref[0:CONV_TAIL, :] = uext_ref[tt:tt + CONV_TAIL, :]

    u = u_ref[...].astype(F32)
    uext_ref[CONV_TAIL:CONV_TAIL + tt, :] = u
    uc = (cb_ref[...] + cw_ref[3:4, :] * u
          + cw_ref[2:3, :] * uext_ref[CONV_TAIL - 1:CONV_TAIL - 1 + tt, :]
          + cw_ref[1:2, :] * uext_ref[CONV_TAIL - 2:CONV_TAIL - 2 + tt, :]
          + cw_ref[0:1, :] * uext_ref[CONV_TAIL - 3:CONV_TAIL - 3 + tt, :])
    ucb = uc.astype(BF16)
    ra = jnp.concatenate(
        [_dot(ucb[:, i * HEAD_DIM:(i + 1) * HEAD_DIM], wa_ref[i]) for i in range(nblk)], axis=1)
    ia = jnp.concatenate(
        [_dot(ucb[:, i * HEAD_DIM:(i + 1) * HEAD_DIM], wi_ref[i]) for i in range(nblk)], axis=1)
    r = jax.nn.sigmoid(ra + ba_ref[...])
    ig = jax.nn.sigmoid(ia + bi_ref[...])
    log_a = (-RG_C) * r * _softplus(-lam_ref[...])
    a = jnp.exp(log_a)
    bt = jnp.sqrt(-jnp.tanh(log_a) * (a * a + 1.0)) * (ig * uc)

    row = lax.broadcasted_iota(jnp.int32, a.shape, 0) & (SUBLANES - 1)
    d = 1
    while d < SUBLANES:
        keep = row >= d
        bt = jnp.where(keep, a * pltpu.roll(bt, d, 0) + bt, bt)
        a = jnp.where(keep, a * pltpu.roll(a, d, 0), a)
        d *= 2
    carry = hc_ref[...]
    groups = []
    for g in range(tt // SUBLANES):
        rows = slice(g * SUBLANES, (g + 1) * SUBLANES)
        hg = a[rows] * carry + bt[rows]
        groups.append(hg)
        carry = hg[SUBLANES - 1:SUBLANES, :]
    h = jnp.concatenate(groups, axis=0)
    hc_ref[...] = carry

    g = gate_ref[...].astype(F32)
    gelu = 0.5 * g * (1.0 + jnp.tanh(math.sqrt(2.0 / math.pi) * (g + 0.044715 * (g * g * g))))
    o_ref[...] = (h * gelu).astype(BF16)


def _rglru(proj, conv_w, conv_b, w_a, b_a, w_i, b_i, lam, batch, seq, tt):
    t = proj.shape[0]
    width = conv_w.shape[1]
    nblk = w_a.shape[0]
    tt = min(tt, seq)
    n_t = seq // tt
    row = lambda x: x.reshape(1, width)
    vec = pl.BlockSpec((1, width), lambda b, n: (0, 0))
    blk = pl.BlockSpec((nblk, HEAD_DIM, HEAD_DIM), lambda b, n: (0, 0, 0))
    return pl.pallas_call(
        functools.partial(_rglru_kernel, tt=tt, nblk=nblk),
        grid=(batch, n_t),
        in_specs=[pl.BlockSpec((tt, width), lambda b, n: (b * n_t + n, 4)),
                  pl.BlockSpec((tt, width), lambda b, n: (b * n_t + n, 5)),
                  pl.BlockSpec((4, width), lambda b, n: (0, 0)),
                  vec, blk, vec, blk, vec, vec],
        out_specs=pl.BlockSpec((tt, width), lambda b, n: (b * n_t + n, 0)),
        out_shape=jax.ShapeDtypeStruct((t, width), BF16),
        scratch_shapes=[pltpu.VMEM((tt + CONV_TAIL, width), F32), pltpu.VMEM((1, width), F32)],
        compiler_params=_cparams(2),
        name="rglru",
    )(proj, proj, conv_w, row(conv_b), w_a.astype(BF16), row(b_a), w_i.astype(BF16), row(b_i),
      row(lam))


FORGET_SPLIT = 3


def _bf16_head(x):
    bits = lax.bitcast_convert_type(x, jnp.uint32) & jnp.uint32(0xFFFF0000)
    return lax.bitcast_convert_type(bits, F32)


def _logf_cumsum_kernel(fl_ref, bf_ref, *o_refs, seq):
    z = fl_ref[...] + bf_ref[...]
    c = jnp.minimum(z, 0.0) - jnp.log1p(jnp.exp(-jnp.abs(z)))
    row = lax.broadcasted_iota(jnp.int32, c.shape, 0)
    d = 1
    while d < seq:
        c = c + jnp.where(row >= d, pltpu.roll(c, d, 0), 0.0)
        d *= 2
    x = c * (-(HEAD_DIM ** 0.5))
    for o_ref in o_refs:
        piece = _bf16_head(x)
        o_ref[...] = piece.astype(BF16)
        x = x - piece


def _logf_cumsum(fl, b_f, batch, seq):
    tile = pl.BlockSpec((seq, LANES), lambda b: (b, 0))
    return pl.pallas_call(
        functools.partial(_logf_cumsum_kernel, seq=seq),
        grid=(batch,),
        in_specs=[tile, pl.BlockSpec((1, LANES), lambda b: (0, 0))],
        out_specs=[tile] * FORGET_SPLIT,
        out_shape=[jax.ShapeDtypeStruct(fl.shape, BF16)] * FORGET_SPLIT,
        compiler_params=_cparams(1),
        name="logf_cumsum",
    )(fl, b_f)


def _fox_kernel(q_ref, k_ref, v_ref, kc_ref, o_ref, kext_ref, vext_ref, m_ref, l_ref, acc_ref, *,
                tq, seq, group):
    qi = pl.program_id(2)
    head = lambda g: slice(g * HEAD_DIM, (g + 1) * HEAD_DIM)

    @pl.when(qi == 0)
    def _():
        for g in range(group):
            kext_ref[g, :, :HEAD_DIM] = k_ref[:, head(g)]
            kext_ref[g, :, HEAD_DIM:] = kc_ref[g]
            vext_ref[g, :, :HEAD_DIM] = v_ref[:, head(g)]
            vext_ref[g, :, HEAD_DIM:] = jnp.ones((seq, HEAD_DIM), BF16)

    lane = lax.broadcasted_iota(jnp.int32, (tq, HEAD_DIM), 1)
    ones = jnp.where(lane < FORGET_SPLIT, 1.0, 0.0).astype(BF16)
    q2 = [jnp.concatenate([q_ref[:, head(g)], ones], axis=1) for g in range(group)]
    c = (HEAD_DIM ** -0.5) * math.log2(math.e)
    reps = tq // HEAD_DIM

    def logits(g, kb):
        off = pl.multiple_of(kb * tq, tq)
        return lax.dot_general(q2[g], kext_ref[g, pl.ds(off, tq), :], (((1,), (1,)), ((), ())),
                               preferred_element_type=F32)

    def fold(g, kb, masked):
        u = logits(g, kb)
        if masked:
            row = lax.broadcasted_iota(jnp.int32, u.shape, 0)
            col = lax.broadcasted_iota(jnp.int32, u.shape, 1)
            u = jnp.where(col <= row, u, -jnp.inf)
        m = m_ref[g]
        m_new = jnp.maximum(m, jnp.broadcast_to(jnp.max(u, axis=-1, keepdims=True), m.shape))
        p = jnp.exp2((u - jnp.concatenate([m_new] * reps, axis=1)) * c)
        alpha = jnp.exp2((m - m_new) * c)
        off = pl.multiple_of(kb * tq, tq)
        pv = _dot(p.astype(BF16), vext_ref[g, pl.ds(off, tq), :])
        m_ref[g] = m_new
        l_ref[g] = alpha * l_ref[g] + pv[:, HEAD_DIM:]
        acc_ref[g] = alpha * acc_ref[g] + pv[:, :HEAD_DIM]

    m_ref[...] = jnp.full(m_ref.shape, -jnp.inf, F32)
    l_ref[...] = jnp.zeros(l_ref.shape, F32)
    acc_ref[...] = jnp.zeros(acc_ref.shape, F32)

    def body(kb, carry):
        for g in range(group):
            fold(g, kb, False)
        return carry

    lax.fori_loop(0, qi, body, 0)
    for g in range(group):
        fold(g, qi, True)
        o_ref[:, head(g)] = (acc_ref[g] / l_ref[g]).astype(BF16)


def _fox_attention(qkv, kc, batch, seq, heads, tq, group=4):
    t = qkv.shape[0]
    tq = min(tq, max(HEAD_DIM, seq // 4))
    nq = seq // tq
    assert tq % HEAD_DIM == 0 and seq % tq == 0 and heads % group == 0
    hg = heads // group
    w = group * HEAD_DIM
    return pl.pallas_call(
        functools.partial(_fox_kernel, tq=tq, seq=seq, group=group),
        grid=(batch, hg, nq),
        in_specs=[
            pl.BlockSpec((tq, w), lambda b, h, qi: (b * nq + qi, h)),
            pl.BlockSpec((seq, w), lambda b, h, qi: (b, hg + h)),
            pl.BlockSpec((seq, w), lambda b, h, qi: (b, 2 * hg + h)),
            pl.BlockSpec((group, seq, HEAD_DIM), lambda b, h, qi: (b * hg + h, 0, 0)),
        ],
        out_specs=pl.BlockSpec((tq, w), lambda b, h, qi: (b * nq + qi, h)),
        out_shape=jax.ShapeDtypeStruct((t, heads * HEAD_DIM), BF16),
        scratch_shapes=[pltpu.VMEM((group, seq, 2 * HEAD_DIM), BF16),
                        pltpu.VMEM((group, seq, 2 * HEAD_DIM), BF16)]
        + [pltpu.VMEM((group, tq, HEAD_DIM), F32)] * 3,
        compiler_params=_cparams(3),
        name="fox_attention",
    )(qkv, qkv, qkv, kc)


def _forget_key_bias(pieces, batch, seq, heads):
    kc = jnp.stack([p.reshape(batch, seq, LANES)[:, :, :heads] for p in pieces], axis=-1)
    kc = jnp.transpose(kc, (0, 2, 1, 3)).reshape(batch * heads, seq, FORGET_SPLIT)
    return jnp.pad(kc, ((0, 0), (0, 0), (0, HEAD_DIM - FORGET_SPLIT)))


def _router_kernel(h_ref, rw_ref, rb_ref, idx_ref, wts_ref, rank_ref, cnt_ref, *, tm):
    @pl.when(pl.program_id(0) == 0)
    def _():
        cnt_ref[...] = jnp.zeros_like(cnt_ref)

    h = h_ref[...]
    h_hi = _bf16_head(h)
    first_order = _dot(h_hi.astype(BF16), rw_ref[...])
    logits = (first_order[:, :LANES] + first_order[:, LANES:]
              + _dot((h - h_hi).astype(BF16), rw_ref[:, :LANES]))
    scores = jax.nn.sigmoid(logits)
    sel = scores + rb_ref[...]
    lane = lax.broadcasted_iota(jnp.int32, sel.shape, 1)
    lane_f = lane.astype(F32)
    idx = jnp.zeros(sel.shape, F32)
    wts = jnp.zeros(sel.shape, F32)
    chosen = jnp.zeros(sel.shape, F32)
    hits = []
    for k in range(TOP_K):
        top = jnp.max(sel, axis=-1, keepdims=True)
        first = jnp.min(jnp.where(sel == top, lane_f, float(LANES)), axis=-1, keepdims=True)
        hit = lane_f == first
        hits.append(hit)
        idx = jnp.where(lane == k, first, idx)
        wts = jnp.where(lane == k, jnp.sum(jnp.where(hit, scores, 0.0), axis=-1, keepdims=True), wts)
        chosen = jnp.where(hit, 1.0, chosen)
        sel = jnp.where(hit, -jnp.inf, sel)
    wts = wts / jnp.sum(wts, axis=-1, keepdims=True) * ROUTED_SCALE

    r = lax.broadcasted_iota(jnp.int32, (tm, tm), 0)
    c = lax.broadcasted_iota(jnp.int32, (tm, tm), 1)
    before = jnp.where(c < r, 1.0, 0.0).astype(BF16)
    rank_all = _dot(before, chosen.astype(BF16)) + cnt_ref[...]
    rank = jnp.zeros(sel.shape, F32)
    for k in range(TOP_K):
        rank = jnp.where(lane == k, jnp.sum(jnp.where(hits[k], rank_all, 0.0), axis=-1, keepdims=True),
                         rank)
    cnt_ref[...] = cnt_ref[...] + jnp.sum(chosen, axis=0, keepdims=True)
    idx_ref[...] = idx.astype(jnp.int32)
    wts_ref[...] = wts
    rank_ref[...] = rank.astype(jnp.int32)


def _router(h, router_w, router_bias, tm=256):
    t, d = h.shape
    e = router_w.shape[1]
    tm = min(tm, t)
    rw = jnp.pad(router_w, ((0, 0), (0, LANES - e)))
    rw_hi = _bf16_head(rw)
    rw = jnp.concatenate([rw_hi.astype(BF16), (rw - rw_hi).astype(BF16)], axis=1)
    rb = jnp.pad(router_bias.astype(F32), (0, LANES - e), constant_values=-jnp.inf).reshape(1, LANES)
    tile = pl.BlockSpec((tm, LANES), lambda i: (i, 0))
    return pl.pallas_call(
        functools.partial(_router_kernel, tm=tm),
        grid=(t // tm,),
        in_specs=[pl.BlockSpec((tm, d), lambda i: (i, 0)),
                  pl.BlockSpec((d, 2 * LANES), lambda i: (0, 0)),
                  pl.BlockSpec((1, LANES), lambda i: (0, 0))],
        out_specs=[tile, tile, tile, pl.BlockSpec((1, LANES), lambda i: (0, 0))],
        out_shape=[jax.ShapeDtypeStruct((t, LANES), jnp.int32),
                   jax.ShapeDtypeStruct((t, LANES), F32),
                   jax.ShapeDtypeStruct((t, LANES), jnp.int32),
                   jax.ShapeDtypeStruct((1, LANES), F32)],
        compiler_params=_cparams(1),
        name="router_topk",
    )(h, rw, rb)


def _dispatch_kernel(pstart_ref, cnt_ref, dest_ref, x_ref, xs_ref, zero_ref, sem, *, tm, n_exp, nc,
                     n_blocks):
    i = pl.program_id(0)
    blk = EXPERT_BLOCK * nc

    def row_copy(r, dst):
        return pltpu.make_async_copy(x_ref.at[pl.ds(pl.multiple_of(r * nc, nc), nc), :],
                                     xs_ref.at[pl.ds(pl.multiple_of(dst * nc, nc), nc), :], sem)

    def zero_run(dst, n):
        return pltpu.make_async_copy(zero_ref.at[pl.ds(0, n * nc), :],
                                     xs_ref.at[pl.ds(pl.multiple_of(dst * nc, nc), n * nc), :], sem)

    def zero_block(b):
        return pltpu.make_async_copy(zero_ref, xs_ref.at[pl.ds(pl.multiple_of(b * blk, blk), blk), :],
                                     sem)

    def pad_rows(e, do):
        first = pstart_ref[e] + cnt_ref[e]
        n = pstart_ref[e + 1] - first
        bit = EXPERT_BLOCK // 2
        while bit:
            pl.when((n & bit) != 0)(functools.partial(do, first, bit))
            first = first + (n & bit)
            bit //= 2

    def first_unused_block():
        return lax.div(pstart_ref[n_exp], EXPERT_BLOCK)

    @pl.when(i == 0)
    def _():
        zero_ref[...] = jnp.zeros_like(zero_ref)

        def per_expert(e, carry):
            pad_rows(e, lambda first, n: zero_run(first, n).start())
            return carry

        lax.fori_loop(0, n_exp, per_expert, 0)
        lax.fori_loop(first_unused_block(), n_blocks, lambda b, c: (zero_block(b).start(), c)[1], 0)

    def issue(r, carry):
        for k in range(TOP_K):
            row_copy(r, dest_ref[0, 0, r * TOP_K + k]).start(priority=k % 2)
        return carry

    lax.fori_loop(0, tm, issue, 0)
    all_rows = xs_ref.at[pl.ds(0, tm * TOP_K * nc), :]
    pltpu.make_async_copy(all_rows, all_rows, sem).wait()

    @pl.when(i == 0)
    def _():
        def per_expert(e, carry):
            pad_rows(e, lambda first, n: zero_run(0, n).wait())
            return carry

        lax.fori_loop(0, n_exp, per_expert, 0)
        lax.fori_loop(first_unused_block(), n_blocks, lambda b, c: (zero_block(0).wait(), c)[1], 0)


def _dispatch(xp, dest, pstart, counts, n_rows, nc, tm=256):
    t = xp.shape[0] // nc
    tm = min(tm, t)
    n_exp = counts.shape[0]
    assert n_rows >= tm * TOP_K
    grid_spec = pltpu.PrefetchScalarGridSpec(
        num_scalar_prefetch=2,
        grid=(t // tm,),
        in_specs=[pl.BlockSpec((1, 1, tm * TOP_K), lambda i, ps, cn: (i, 0, 0),
                               memory_space=pltpu.SMEM),
                  pl.BlockSpec((tm * nc, LANES), lambda i, ps, cn: (i, 0))],
        out_specs=pl.BlockSpec(memory_space=pl.ANY),
        scratch_shapes=[pltpu.VMEM((EXPERT_BLOCK * nc, LANES), xp.dtype),
                        pltpu.SemaphoreType.DMA(())],
    )
    assert n_rows % EXPERT_BLOCK == 0
    return pl.pallas_call(
        functools.partial(_dispatch_kernel, tm=tm, n_exp=n_exp, nc=nc,
                          n_blocks=n_rows // EXPERT_BLOCK),
        grid_spec=grid_spec,
        out_shape=jax.ShapeDtypeStruct((n_rows * nc, LANES), xp.dtype),
        compiler_params=_cparams(1),
        name="moe_dispatch",
    )(pstart, counts, dest.reshape(t // tm, 1, tm * TOP_K), xp)


def _expert_ffn_kernel(be_ref, nb_ref, x_ref, wg_ref, wu_ref, wd_ref, o_ref, wgb_ref, wub_ref,
                       wdb_ref, *, nc):
    i = pl.program_id(0)
    used = i < nb_ref[0]
    new_expert = jnp.logical_or(i == 0, be_ref[i] != be_ref[jnp.maximum(i - 1, 0)])

    @pl.when(jnp.logical_and(used, new_expert))
    def _():
        wgb_ref[...] = wg_ref[...].astype(BF16)
        wub_ref[...] = wu_ref[...].astype(BF16)
        wdb_ref[...] = wd_ref[...].astype(BF16)

    @pl.when(used)
    def _():
        halves = [_unpack_pairs(x_ref[pl.ds(s, EXPERT_BLOCK, stride=nc), :]) for s in range(nc)]
        x = jnp.concatenate([h[0].astype(BF16) for h in halves] + [h[1].astype(BF16) for h in halves],
                            axis=1)
        hb = (_silu(_dot(x, wgb_ref[...])) * _dot(x, wub_ref[...])).astype(BF16)
        _store_slabs(o_ref, _pack_pairs(_dot(hb, wdb_ref[...])), EXPERT_BLOCK, nc)

    @pl.when(jnp.logical_not(used))
    def _():
        o_ref[...] = jnp.zeros_like(o_ref)


def _expert_ffn(xs, block_e, n_used, w_gate, w_up, w_down, layer, nc):
    _, _, d, f = w_gate.shape
    nb = xs.shape[0] // (EXPERT_BLOCK * nc)
    wspec = lambda a, b: pl.BlockSpec((None, None, a, b), lambda i, be, nu: (layer, be[i], 0, 0))
    grid_spec = pltpu.PrefetchScalarGridSpec(
        num_scalar_prefetch=2,
        grid=(nb,),
        in_specs=[
            pl.BlockSpec((EXPERT_BLOCK * nc, LANES), lambda i, be, nu: (jnp.minimum(i, nu[0] - 1), 0)),
            wspec(d, f), wspec(d, f), wspec(f, d),
        ],
        out_specs=pl.BlockSpec((EXPERT_BLOCK * nc, LANES), lambda i, be, nu: (i, 0)),
        scratch_shapes=[pltpu.VMEM((d, f), BF16), pltpu.VMEM((d, f), BF16), pltpu.VMEM((f, d), BF16)],
    )
    return pl.pallas_call(
        functools.partial(_expert_ffn_kernel, nc=nc),
        grid_spec=grid_spec,
        out_shape=jax.ShapeDtypeStruct(xs.shape, xs.dtype),
        compiler_params=_cparams(1),
        name="expert_ffn",
    )(block_e, n_used, xs, w_gate, w_up, w_down)


def _combine_kernel(dest_ref, destn_ref, wts_ref, hf_ref, hb_ref, sg_ref, su_ref, sd_ref, g_ref,
                    b_ref, ys_ref, of_ref, ob_ref, buf_ref, sems, *, tm, nc, alpha):
    i = pl.program_id(0)
    slot = i % 2

    def slab_copy(sl, k, r, src):
        return pltpu.make_async_copy(ys_ref.at[src], buf_ref.at[sl, :, k * tm + r, :], sems.at[sl])

    def gather(sl, rows_ref):
        def issue(r, carry):
            for k in range(TOP_K):
                slab_copy(sl, k, r, rows_ref[0, 0, r * TOP_K + k]).start(priority=k % 2)
            return carry

        lax.fori_loop(0, tm, issue, 0)

    @pl.when(i == 0)
    def _():
        gather(slot, dest_ref)

    @pl.when(i + 1 < pl.num_programs(0))
    def _():
        gather(1 - slot, destn_ref)

    xb = hb_ref[...]
    shared = _dot((_silu(_dot(xb, sg_ref[...])) * _dot(xb, su_ref[...])).astype(BF16), sd_ref[...])
    z = alpha * hf_ref[...] + shared

    pltpu.make_async_copy(buf_ref.at[slot], buf_ref.at[slot], sems.at[slot]).wait()
    wts = wts_ref[...]
    his = [None] * nc
    los = [None] * nc
    for k in range(TOP_K):
        wk = wts[:, k:k + 1]
        for s in range(nc):
            hi, lo = _unpack_pairs(buf_ref[slot, s, pl.ds(k * tm, tm), :])
            his[s] = wk * hi if k == 0 else his[s] + wk * hi
            los[s] = wk * lo if k == 0 else los[s] + wk * lo
    out = _layer_norm(z + jnp.concatenate(his + los, axis=1), g_ref[...], b_ref[...])
    of_ref[...] = out
    ob_ref[...] = out.astype(BF16)


def _combine(ys, dest, wts, h_f32, h_bf16, sh_gate, sh_up, sh_down, g, b, alpha, nc, tm=128):
    t, d = h_f32.shape
    f = sh_gate.shape[1]
    tm = min(tm, t)
    n = t // tm
    smem = lambda im: pl.BlockSpec((1, 1, tm * TOP_K), im, memory_space=pltpu.SMEM)
    tile = lambda w: pl.BlockSpec((tm, w), lambda i: (i, 0))
    const = lambda shape: pl.BlockSpec(shape, lambda i: (0, 0))
    rows = dest.reshape(n, 1, tm * TOP_K)
    return pl.pallas_call(
        functools.partial(_combine_kernel, tm=tm, nc=nc, alpha=alpha),
        grid=(n,),
        in_specs=[smem(lambda i: (i, 0, 0)), smem(lambda i: (jnp.minimum(i + 1, n - 1), 0, 0)),
                  tile(LANES), tile(d), tile(d),
                  const((d, f)), const((d, f)), const((f, d)), const((1, d)), const((1, d)),
                  pl.BlockSpec(memory_space=pl.ANY)],
        out_specs=[tile(d), tile(d)],
        out_shape=[jax.ShapeDtypeStruct((t, d), F32), jax.ShapeDtypeStruct((t, d), BF16)],
        scratch_shapes=[pltpu.VMEM((2, nc, TOP_K * tm, LANES), ys.dtype),
                        pltpu.SemaphoreType.DMA((2,))],
        compiler_params=_cparams(1),
        name="moe_combine",
    )(rows, rows, wts, h_f32, h_bf16, sh_gate, sh_up, sh_down, g.reshape(1, d), b.reshape(1, d),
      ys.reshape(-1, nc, LANES))


def _moe_ln(h_f32, h_bf16, h_slabs, router_w, router_bias, w_gate, w_up, w_down, layer, sh_gate,
            sh_up, sh_down, g, b, alpha):
    t, d = h_f32.shape
    n_exp = router_w.shape[1]
    nc = _slab_rows(d)
    idx, wts, rank, counts = _router(h_f32, router_w, router_bias)
    idx, rank = idx[:, :TOP_K], rank[:, :TOP_K]

    counts = counts[0, :n_exp].astype(jnp.int32)
    padded = (counts + EXPERT_BLOCK - 1) // EXPERT_BLOCK * EXPERT_BLOCK
    pends = jnp.cumsum(padded)
    pstart = jnp.concatenate([jnp.zeros((1,), jnp.int32), pends]).astype(jnp.int32)
    n_blocks = t * TOP_K // EXPERT_BLOCK + n_exp
    block_start = jnp.arange(n_blocks, dtype=jnp.int32) * EXPERT_BLOCK
    block_e = jnp.sum((pends[None, :] <= block_start[:, None]).astype(jnp.int32), axis=1)
    block_e = jnp.minimum(block_e, n_exp - 1)
    n_used = (pends[-1:] // EXPERT_BLOCK).astype(jnp.int32)
    group_start = jnp.sum(jnp.where(idx[:, :, None] == jnp.arange(n_exp, dtype=jnp.int32),
                                    pstart[:n_exp], 0), axis=-1)
    dest = group_start + rank

    xs = _dispatch(h_slabs, dest, pstart, counts, n_blocks * EXPERT_BLOCK, nc)
    ys = _expert_ffn(xs, block_e, n_used, w_gate, w_up, w_down, layer, nc)
    return _combine(ys, dest, wts, h_f32, h_bf16, sh_gate.astype(BF16), sh_up.astype(BF16),
                    sh_down.astype(BF16), g, b, alpha, nc)


def kernel(x, even_w_in, ret_gn_g, rg_conv_w, rg_conv_b, rg_w_a, rg_b_a, rg_w_i, rg_b_i, rg_lambda,
           even_w_out, fox_w_in, fox_b_f, fox_w_out, ln_g, ln_b, router_w, router_bias, exp_w_gate,
           exp_w_up, exp_w_down, sh_w_gate, sh_w_up, sh_w_down):
    batch, seq, d = x.shape
    depth = ln_g.shape[0]
    alpha = (2 * depth) ** 0.25
    ret_width = ret_gn_g.shape[1]
    fox_heads = fox_b_f.shape[1]

    h_f32 = x.reshape(batch * seq, d)
    h_bf16 = h_f32.astype(BF16)
    for layer in range(depth):
        j = layer // 2
        if layer % 2 == 0:
            proj = _matmul(h_bf16, even_w_in[j].astype(BF16), BF16)
            y_ret = _retention(proj, ret_gn_g[j], batch, seq, chunk=256)
            y_rnn = _rglru(proj, rg_conv_w[j], rg_conv_b[j], rg_w_a[j], rg_b_a[j], rg_w_i[j],
                           rg_b_i[j], rg_lambda[j], batch, seq, tt=256)
            w_out = even_w_out[j].astype(BF16)
            ys, ws = [y_ret, y_rnn], [w_out[:ret_width], w_out[ret_width:]]
        else:
            w_in = fox_w_in[j]
            qkv = _matmul(h_bf16, w_in[:, :3 * d].astype(BF16), BF16)
            w_f = jnp.pad(w_in[:, 3 * d:], ((0, 0), (0, LANES - fox_heads))).astype(BF16)
            fl = _matmul(h_bf16, w_f, F32)
            b_f = jnp.pad(fox_b_f[j], (0, LANES - fox_heads)).reshape(1, LANES)
            kc = _forget_key_bias(_logf_cumsum(fl, b_f, batch, seq), batch, seq, fox_heads)
            o = _fox_attention(qkv, kc, batch, seq, fox_heads, tq=512)
            ys, ws = [o], [fox_w_out[j].astype(BF16)]
        h_f32, h_bf16, h_slabs = _proj_ln(ys, ws, h_f32, ln_g[layer, 0], ln_b[layer, 0], alpha)
        h_f32, h_bf16 = _moe_ln(h_f32, h_bf16, h_slabs, router_w[layer], router_bias[layer],
                                exp_w_gate, exp_w_up, exp_w_down, layer,
                                sh_w_gate[layer], sh_w_up[layer], sh_w_down[layer],
                                ln_g[layer, 1], ln_b[layer, 1], alpha)
    return h_f32.reshape(batch, seq, d)
```

```python
import functools
import math

import numpy as np
import jax
import jax.numpy as jnp
from jax import lax
from jax.experimental import pallas as pl
from jax.experimental.pallas import tpu as pltpu

F32 = jnp.float32
BF16 = jnp.bfloat16

HEAD_DIM = 128
TOP_K = 8
ROUTED_SCALE = 2.5
ROPE_BASE = 10000.0
RG_C = 8.0
LN_EPS = 1e-5
CONV_TAIL = 8

V7X_VMEM_BYTES = 64 * 1024 * 1024
VMEM_LIMIT = V7X_VMEM_BYTES - 8 * 1024 * 1024
LANES = 128
SUBLANES = 8

EXPERT_BLOCK = 512


def _cparams(grid_rank):
    return pltpu.CompilerParams(dimension_semantics=("arbitrary",) * grid_rank,
                                vmem_limit_bytes=VMEM_LIMIT)


def _dot(a, b):
    return jnp.dot(a, b, preferred_element_type=F32)


def _silu(x):
    return x * jax.nn.sigmoid(x)


def _softplus(x):
    return jnp.maximum(x, 0.0) + jnp.log1p(jnp.exp(-jnp.abs(x)))


def _layer_norm(z, g, b):
    mu = jnp.mean(z, axis=-1, keepdims=True)
    zc = z - mu
    var = jnp.mean(zc * zc, axis=-1, keepdims=True)
    return zc * lax.rsqrt(var + LN_EPS) * g + b


def _mm_kernel(x_ref, w_ref, o_ref):
    o_ref[...] = _dot(x_ref[...], w_ref[...]).astype(o_ref.dtype)


def _tile(n, target):
    best = LANES
    for c in range(LANES, min(n, target) + 1, LANES):
        if n % c == 0:
            best = c
    assert n % best == 0
    return best


def _matmul(x, w, out_dtype, tm=1024, tn=1024):
    m, k = x.shape
    n = w.shape[1]
    tm, tn = _tile(m, tm), _tile(n, tn)
    return pl.pallas_call(
        _mm_kernel,
        grid=(m // tm, n // tn),
        in_specs=[pl.BlockSpec((tm, k), lambda i, j: (i, 0)),
                  pl.BlockSpec((k, tn), lambda i, j: (0, j))],
        out_specs=pl.BlockSpec((tm, tn), lambda i, j: (i, j)),
        out_shape=jax.ShapeDtypeStruct((m, n), out_dtype),
        compiler_params=_cparams(2),
        name="in_proj",
    )(x, w)


def _slab_rows(d):
    assert d % (2 * LANES) == 0
    return d // (2 * LANES)


def _pack_pairs(x):
    half = x.shape[1] // 2
    hi = lax.bitcast_convert_type(x[:, :half].astype(BF16).astype(F32), jnp.uint32)
    lo = lax.bitcast_convert_type(x[:, half:].astype(BF16).astype(F32), jnp.uint32)
    return hi | (lo >> 16)


def _unpack_pairs(w):
    hi = lax.bitcast_convert_type(w & jnp.uint32(0xFFFF0000), F32)
    lo = lax.bitcast_convert_type(w << 16, F32)
    return hi, lo


def _store_slabs(ref, words, rows, nc):
    for s in range(nc):
        ref[pl.ds(s, rows, stride=nc), :] = words[:, s * LANES:(s + 1) * LANES]


def _proj_ln_kernel(*refs, n_in, alpha, tm, nc):
    ys, ws = refs[:n_in], refs[n_in:2 * n_in]
    h_ref, g_ref, b_ref, of_ref, ob_ref, op_ref = refs[2 * n_in:]
    acc = _dot(ys[0][...], ws[0][...])
    for y_ref, w_ref in zip(ys[1:], ws[1:]):
        acc = acc + _dot(y_ref[...], w_ref[...])
    out = _layer_norm(alpha * h_ref[...] + acc, g_ref[...], b_ref[...])
    of_ref[...] = out
    ob_ref[...] = out.astype(BF16)
    _store_slabs(op_ref, _pack_pairs(out), tm, nc)


def _proj_ln(ys, ws, h, g, b, alpha, tm=512):
    t, d = h.shape
    tm = min(tm, t)
    n_in = len(ys)
    nc = _slab_rows(d)
    in_specs = ([pl.BlockSpec((tm, y.shape[1]), lambda i: (i, 0)) for y in ys]
                + [pl.BlockSpec(w.shape, lambda i: (0, 0), pipeline_mode=pl.Buffered(1)) for w in ws]
                + [pl.BlockSpec((tm, d), lambda i: (i, 0)),
                   pl.BlockSpec((1, d), lambda i: (0, 0)),
                   pl.BlockSpec((1, d), lambda i: (0, 0))])
    return pl.pallas_call(
        functools.partial(_proj_ln_kernel, n_in=n_in, alpha=alpha, tm=tm, nc=nc),
        grid=(t // tm,),
        in_specs=in_specs,
        out_specs=[pl.BlockSpec((tm, d), lambda i: (i, 0)),
                   pl.BlockSpec((tm, d), lambda i: (i, 0)),
                   pl.BlockSpec((tm * nc, LANES), lambda i: (i, 0))],
        out_shape=[jax.ShapeDtypeStruct((t, d), F32), jax.ShapeDtypeStruct((t, d), BF16),
                   jax.ShapeDtypeStruct((t * nc, LANES), jnp.uint32)],
        compiler_params=_cparams(1),
        name="out_proj_ln",
    )(*ys, *ws, h, g.reshape(1, d), b.reshape(1, d))


def _retention_kernel(q_ref, k_ref, v_ref, g_ref, cos_ref, sin_ref, intra_ref, qdec_ref, kdec_ref,
                      gn_ref, o_ref, state_ref, *, heads, chunk_decay):
    @pl.when(pl.program_id(1) == 0)
    def _():
        state_ref[...] = jnp.zeros_like(state_ref)

    cos, sin = cos_ref[...], sin_ref[...]
    half = HEAD_DIM // 2
    for h in range(heads):
        sl = slice(h * HEAD_DIM, (h + 1) * HEAD_DIM)
        q = q_ref[:, sl].astype(F32)
        k = k_ref[:, sl].astype(F32)
        q = q * cos + pltpu.roll(q, half, 1) * sin
        k = (k * cos + pltpu.roll(k, half, 1) * sin) * (HEAD_DIM ** -0.5)
        v = v_ref[:, sl]
        s = lax.dot_general(q.astype(BF16), k.astype(BF16), (((1,), (1,)), ((), ())),
                            preferred_element_type=F32) * intra_ref[h]
        state = state_ref[h]
        y = _dot(s.astype(BF16), v) + _dot((q * qdec_ref[h]).astype(BF16), state.astype(BF16))
        kv = lax.dot_general((k * kdec_ref[h]).astype(BF16), v, (((0,), (0,)), ((), ())),
                             preferred_element_type=F32)
        state_ref[h] = state * chunk_decay[h] + kv
        mu = jnp.mean(y, axis=-1, keepdims=True)
        yc = y - mu
        var = jnp.mean(yc * yc, axis=-1, keepdims=True)
        yn = yc * lax.rsqrt(var + LN_EPS) * gn_ref[:, sl]
        o_ref[:, sl] = (yn * _silu(g_ref[:, sl].astype(F32))).astype(BF16)


def _retention(proj, gn_g, batch, seq, chunk):
    t = proj.shape[0]
    width = gn_g.shape[0]
    heads = width // HEAD_DIM
    chunk = min(chunk, seq)
    n_chunks = seq // chunk
    half = HEAD_DIM // 2

    pos = jnp.arange(seq, dtype=F32)
    inv = ROPE_BASE ** (-jnp.arange(half, dtype=F32) / half)
    ang = pos[:, None] * inv[None, :]
    cos2 = jnp.concatenate([jnp.cos(ang), jnp.cos(ang)], axis=1)
    sin2 = jnp.concatenate([-jnp.sin(ang), jnp.sin(ang)], axis=1)

    log_g = jnp.log1p(-jnp.exp2(-5.0 - jnp.arange(heads, dtype=F32)))
    i = jnp.arange(chunk, dtype=F32)
    rel = i[:, None] - i[None, :]
    intra = jnp.where(rel >= 0, jnp.exp(log_g[:, None, None] * jnp.maximum(rel, 0.0)), 0.0)
    kdec = jnp.exp(log_g[:, None] * (chunk - 1.0 - i)[None, :])
    qdec = jnp.exp(log_g[:, None] * (i + 1.0)[None, :])
    kdec = jnp.broadcast_to(kdec[:, :, None], (heads, chunk, HEAD_DIM))
    qdec = jnp.broadcast_to(qdec[:, :, None], (heads, chunk, HEAD_DIM))
    chunk_decay = tuple(math.exp(math.log1p(-2.0 ** (-5.0 - h)) * chunk) for h in range(heads))

    def col(c):
        return pl.BlockSpec((chunk, width), lambda b, n: (b * n_chunks + n, c))

    const3 = lambda shape: pl.BlockSpec(shape, lambda b, n: (0, 0, 0))
    return pl.pallas_call(
        functools.partial(_retention_kernel, heads=heads, chunk_decay=chunk_decay),
        grid=(batch, n_chunks),
        in_specs=[col(0), col(1), col(2), col(3),
                  pl.BlockSpec((chunk, HEAD_DIM), lambda b, n: (n, 0)),
                  pl.BlockSpec((chunk, HEAD_DIM), lambda b, n: (n, 0)),
                  const3((heads, chunk, chunk)),
                  const3((heads, chunk, HEAD_DIM)),
                  const3((heads, chunk, HEAD_DIM)),
                  pl.BlockSpec((1, width), lambda b, n: (0, 0))],
        out_specs=pl.BlockSpec((chunk, width), lambda b, n: (b * n_chunks + n, 0)),
        out_shape=jax.ShapeDtypeStruct((t, width), BF16),
        scratch_shapes=[pltpu.VMEM((heads, HEAD_DIM, HEAD_DIM), F32)],
        compiler_params=_cparams(2),
        name="retention",
    )(proj, proj, proj, proj, cos2, sin2, intra, qdec, kdec, gn_g.reshape(1, width))


def _rglru_kernel(u_ref, gate_ref, cw_ref, cb_ref, wa_ref, ba_ref, wi_ref, bi_ref, lam_ref,
                  o_ref, uext_ref, hc_ref, *, tt, nblk):
    n = pl.program_id(1)

    @pl.when(n == 0)
    def _():
        uext_ref[0:CONV_TAIL, :] = jnp.zeros((CONV_TAIL, uext_ref.shape[1]), F32)
        hc_ref[...] = jnp.zeros_like(hc_ref)

    @pl.when(n > 0)
    def _():
        uext_ref[0:CONV_TAIL, :] = uext_ref[tt:tt + CONV_TAIL, :]

    u = u_ref[...].astype(F32)
    uext_ref[CONV_TAIL:CONV_TAIL + tt, :] = u
    uc = (cb_ref[...] + cw_ref[3:4, :] * u
          + cw_ref[2:3, :] * uext_ref[CONV_TAIL - 1:CONV_TAIL - 1 + tt, :]
          + cw_ref[1:2, :] * uext_ref[CONV_TAIL - 2:CONV_TAIL - 2 + tt, :]
          + cw_ref[0:1, :] * uext_ref[CONV_TAIL - 3:CONV_TAIL - 3 + tt, :])
    ucb = uc.astype(BF16)
    ra = jnp.concatenate(
        [_dot(ucb[:, i * HEAD_DIM:(i + 1) * HEAD_DIM], wa_ref[i]) for i in range(nblk)], axis=1)
    ia = jnp.concatenate(
        [_dot(ucb[:, i * HEAD_DIM:(i + 1) * HEAD_DIM], wi_ref[i]) for i in range(nblk)], axis=1)
    r = jax.nn.sigmoid(ra + ba_ref[...])
    ig = jax.nn.sigmoid(ia + bi_ref[...])
    log_a = (-RG_C) * r * _softplus(-lam_ref[...])
    a = jnp.exp(log_a)
    bt = jnp.sqrt(-jnp.tanh(log_a) * (a * a + 1.0)) * (ig * uc)

    row = lax.broadcasted_iota(jnp.int32, a.shape, 0) & (SUBLANES - 1)
    d = 1
    while d < SUBLANES:
        keep = row >= d
        bt = jnp.where(keep, a * pltpu.roll(bt, d, 0) + bt, bt)
        a = jnp.where(keep, a * pltpu.roll(a, d, 0), a)
        d *= 2
    carry = hc_ref[...]
    groups = []
    for g in range(tt // SUBLANES):
        rows = slice(g * SUBLANES, (g + 1) * SUBLANES)
        hg = a[rows] * carry + bt[rows]
        groups.append(hg)
        carry = hg[SUBLANES - 1:SUBLANES, :]
    h = jnp.concatenate(groups, axis=0)
    hc_ref[...] = carry

    g = gate_ref[...].astype(F32)
    gelu = 0.5 * g * (1.0 + jnp.tanh(math.sqrt(2.0 / math.pi) * (g + 0.044715 * (g * g * g))))
    o_ref[...] = (h * gelu).astype(BF16)


def _rglru(proj, conv_w, conv_b, w_a, b_a, w_i, b_i, lam, batch, seq, tt):
    t = proj.shape[0]
    width = conv_w.shape[1]
    nblk = w_a.shape[0]
    tt = min(tt, seq)
    n_t = seq // tt
    row = lambda x: x.reshape(1, width)
    vec = pl.BlockSpec((1, width), lambda b, n: (0, 0))
    blk = pl.BlockSpec((nblk, HEAD_DIM, HEAD_DIM), lambda b, n: (0, 0, 0))
    return pl.pallas_call(
        functools.partial(_rglru_kernel, tt=tt, nblk=nblk),
        grid=(batch, n_t),
        in_specs=[pl.BlockSpec((tt, width), lambda b, n: (b * n_t + n, 4)),
                  pl.BlockSpec((tt, width), lambda b, n: (b * n_t + n, 5)),
                  pl.BlockSpec((4, width), lambda b, n: (0, 0)),
                  vec, blk, vec, blk, vec, vec],
        out_specs=pl.BlockSpec((tt, width), lambda b, n: (b * n_t + n, 0)),
        out_shape=jax.ShapeDtypeStruct((t, width), BF16),
        scratch_shapes=[pltpu.VMEM((tt + CONV_TAIL, width), F32), pltpu.VMEM((1, width), F32)],
        compiler_params=_cparams(2),
        name="rglru",
    )(proj, proj, conv_w, row(conv_b), w_a.astype(BF16), row(b_a), w_i.astype(BF16), row(b_i),
      row(lam))


FORGET_SPLIT = 3


def _bf16_head(x):
    bits = lax.bitcast_convert_type(x, jnp.uint32) & jnp.uint32(0xFFFF0000)
    return lax.bitcast_convert_type(bits, F32)


def _logf_cumsum_kernel(fl_ref, bf_ref, *o_refs, seq):
    z = fl_ref[...] + bf_ref[...]
    c = jnp.minimum(z, 0.0) - jnp.log1p(jnp.exp(-jnp.abs(z)))
    row = lax.broadcasted_iota(jnp.int32, c.shape, 0)
    d = 1
    while d < seq:
        c = c + jnp.where(row >= d, pltpu.roll(c, d, 0), 0.0)
        d *= 2
    x = c * (-(HEAD_DIM ** 0.5))
    for o_ref in o_refs:
        piece = _bf16_head(x)
        o_ref[...] = piece.astype(BF16)
        x = x - piece


def _logf_cumsum(fl, b_f, batch, seq):
    tile = pl.BlockSpec((seq, LANES), lambda b: (b, 0))
    return pl.pallas_call(
        functools.partial(_logf_cumsum_kernel, seq=seq),
        grid=(batch,),
        in_specs=[tile, pl.BlockSpec((1, LANES), lambda b: (0, 0))],
        out_specs=[tile] * FORGET_SPLIT,
        out_shape=[jax.ShapeDtypeStruct(fl.shape, BF16)] * FORGET_SPLIT,
        compiler_params=_cparams(1),
        name="logf_cumsum",
    )(fl, b_f)


def _fox_kernel(q_ref, k_ref, v_ref, kc_ref, o_ref, kext_ref, vext_ref, m_ref, l_ref, acc_ref, *,
                tq, seq, group):
    qi = pl.program_id(2)
    head = lambda g: slice(g * HEAD_DIM, (g + 1) * HEAD_DIM)

    @pl.when(qi == 0)
    def _():
        for g in range(group):
            kext_ref[g, :, :HEAD_DIM] = k_ref[:, head(g)]
            kext_ref[g, :, HEAD_DIM:] = kc_ref[g]
            vext_ref[g, :, :HEAD_DIM] = v_ref[:, head(g)]
            vext_ref[g, :, HEAD_DIM:] = jnp.ones((seq, HEAD_DIM), BF16)

    lane = lax.broadcasted_iota(jnp.int32, (tq, HEAD_DIM), 1)
    ones = jnp.where(lane < FORGET_SPLIT, 1.0, 0.0).astype(BF16)
    q2 = [jnp.concatenate([q_ref[:, head(g)], ones], axis=1) for g in range(group)]
    c = (HEAD_DIM ** -0.5) * math.log2(math.e)
    reps = tq // HEAD_DIM

    def logits(g, kb):
        off = pl.multiple_of(kb * tq, tq)
        return lax.dot_general(q2[g], kext_ref[g, pl.ds(off, tq), :], (((1,), (1,)), ((), ())),
                               preferred_element_type=F32)

    def fold(g, kb, masked):
        u = logits(g, kb)
        if masked:
            row = lax.broadcasted_iota(jnp.int32, u.shape, 0)
            col = lax.broadcasted_iota(jnp.int32, u.shape, 1)
            u = jnp.where(col <= row, u, -jnp.inf)
        m = m_ref[g]
        m_new = jnp.maximum(m, jnp.broadcast_to(jnp.max(u, axis=-1, keepdims=True), m.shape))
        p = jnp.exp2((u - jnp.concatenate([m_new] * reps, axis=1)) * c)
        alpha = jnp.exp2((m - m_new) * c)
        off = pl.multiple_of(kb * tq, tq)
        pv = _dot(p.astype(BF16), vext_ref[g, pl.ds(off, tq), :])
        m_ref[g] = m_new
        l_ref[g] = alpha * l_ref[g] + pv[:, HEAD_DIM:]
        acc_ref[g] = alpha * acc_ref[g] + pv[:, :HEAD_DIM]

    m_ref[...] = jnp.full(m_ref.shape, -jnp.inf, F32)
    l_ref[...] = jnp.zeros(l_ref.shape, F32)
    acc_ref[...] = jnp.zeros(acc_ref.shape, F32)

    def body(kb, carry):
        for g in range(group):
            fold(g, kb, False)
        return carry

    lax.fori_loop(0, qi, body, 0)
    for g in range(group):
        fold(g, qi, True)
        o_ref[:, head(g)] = (acc_ref[g] / l_ref[g]).astype(BF16)


def _fox_attention(qkv, kc, batch, seq, heads, tq, group=4):
    t = qkv.shape[0]
    tq = min(tq, max(HEAD_DIM, seq // 4))
    nq = seq // tq
    assert tq % HEAD_DIM == 0 and seq % tq == 0 and heads % group == 0
    hg = heads // group
    w = group * HEAD_DIM
    return pl.pallas_call(
        functools.partial(_fox_kernel, tq=tq, seq=seq, group=group),
        grid=(batch, hg, nq),
        in_specs=[
            pl.BlockSpec((tq, w), lambda b, h, qi: (b * nq + qi, h)),
            pl.BlockSpec((seq, w), lambda b, h, qi: (b, hg + h)),
            pl.BlockSpec((seq, w), lambda b, h, qi: (b, 2 * hg + h)),
            pl.BlockSpec((group, seq, HEAD_DIM), lambda b, h, qi: (b * hg + h, 0, 0)),
        ],
        out_specs=pl.BlockSpec((tq, w), lambda b, h, qi: (b * nq + qi, h)),
        out_shape=jax.ShapeDtypeStruct((t, heads * HEAD_DIM), BF16),
        scratch_shapes=[pltpu.VMEM((group, seq, 2 * HEAD_DIM), BF16),
                        pltpu.VMEM((group, seq, 2 * HEAD_DIM), BF16)]
        + [pltpu.VMEM((group, tq, HEAD_DIM), F32)] * 3,
        compiler_params=_cparams(3),
        name="fox_attention",
    )(qkv, qkv, qkv, kc)


def _forget_key_bias(pieces, batch, seq, heads):
    kc = jnp.stack([p.reshape(batch, seq, LANES)[:, :, :heads] for p in pieces], axis=-1)
    kc = jnp.transpose(kc, (0, 2, 1, 3)).reshape(batch * heads, seq, FORGET_SPLIT)
    return jnp.pad(kc, ((0, 0), (0, 0), (0, HEAD_DIM - FORGET_SPLIT)))


def _router_kernel(h_ref, rw_ref, rb_ref, idx_ref, wts_ref, rank_ref, cnt_ref, *, tm):
    @pl.when(pl.program_id(0) == 0)
    def _():
        cnt_ref[...] = jnp.zeros_like(cnt_ref)

    h = h_ref[...]
    h_hi = _bf16_head(h)
    first_order = _dot(h_hi.astype(BF16), rw_ref[...])
    logits = (first_order[:, :LANES] + first_order[:, LANES:]
              + _dot((h - h_hi).astype(BF16), rw_ref[:, :LANES]))
    scores = jax.nn.sigmoid(logits)
    sel = scores + rb_ref[...]
    lane = lax.broadcasted_iota(jnp.int32, sel.shape, 1)
    lane_f = lane.astype(F32)
    idx = jnp.zeros(sel.shape, F32)
    wts = jnp.zeros(sel.shape, F32)
    chosen = jnp.zeros(sel.shape, F32)
    hits = []
    for k in range(TOP_K):
        top = jnp.max(sel, axis=-1, keepdims=True)
        first = jnp.min(jnp.where(sel == top, lane_f, float(LANES)), axis=-1, keepdims=True)
        hit = lane_f == first
        hits.append(hit)
        idx = jnp.where(lane == k, first, idx)
        wts = jnp.where(lane == k, jnp.sum(jnp.where(hit, scores, 0.0), axis=-1, keepdims=True), wts)
        chosen = jnp.where(hit, 1.0, chosen)
        sel = jnp.where(hit, -jnp.inf, sel)
    wts = wts / jnp.sum(wts, axis=-1, keepdims=True) * ROUTED_SCALE

    r = lax.broadcasted_iota(jnp.int32, (tm, tm), 0)
    c = lax.broadcasted_iota(jnp.int32, (tm, tm), 1)
    before = jnp.where(c < r, 1.0, 0.0).astype(BF16)
    rank_all = _dot(before, chosen.astype(BF16)) + cnt_ref[...]
    rank = jnp.zeros(sel.shape, F32)
    for k in range(TOP_K):
        rank = jnp.where(lane == k, jnp.sum(jnp.where(hits[k], rank_all, 0.0), axis=-1, keepdims=True),
                         rank)
    cnt_ref[...] = cnt_ref[...] + jnp.sum(chosen, axis=0, keepdims=True)
    idx_ref[...] = idx.astype(jnp.int32)
    wts_ref[...] = wts
    rank_ref[...] = rank.astype(jnp.int32)


def _router(h, router_w, router_bias, tm=256):
    t, d = h.shape
    e = router_w.shape[1]
    tm = min(tm, t)
    rw = jnp.pad(router_w, ((0, 0), (0, LANES - e)))
    rw_hi = _bf16_head(rw)
    rw = jnp.concatenate([rw_hi.astype(BF16), (rw - rw_hi).astype(BF16)], axis=1)
    rb = jnp.pad(router_bias.astype(F32), (0, LANES - e), constant_values=-jnp.inf).reshape(1, LANES)
    tile = pl.BlockSpec((tm, LANES), lambda i: (i, 0))
    return pl.pallas_call(
        functools.partial(_router_kernel, tm=tm),
        grid=(t // tm,),
        in_specs=[pl.BlockSpec((tm, d), lambda i: (i, 0)),
                  pl.BlockSpec((d, 2 * LANES), lambda i: (0, 0)),
                  pl.BlockSpec((1, LANES), lambda i: (0, 0))],
        out_specs=[tile, tile, tile, pl.BlockSpec((1, LANES), lambda i: (0, 0))],
        out_shape=[jax.ShapeDtypeStruct((t, LANES), jnp.int32),
                   jax.ShapeDtypeStruct((t, LANES), F32),
                   jax.ShapeDtypeStruct((t, LANES), jnp.int32),
                   jax.ShapeDtypeStruct((1, LANES), F32)],
        compiler_params=_cparams(1),
        name="router_topk",
    )(h, rw, rb)


def _dispatch_kernel(pstart_ref, cnt_ref, dest_ref, x_ref, xs_ref, zero_ref, sem, *, tm, n_exp, nc,
                     n_blocks):
    i = pl.program_id(0)
    blk = EXPERT_BLOCK * nc

    def row_copy(r, dst):
        return pltpu.make_async_copy(x_ref.at[pl.ds(pl.multiple_of(r * nc, nc), nc), :],
                                     xs_ref.at[pl.ds(pl.multiple_of(dst * nc, nc), nc), :], sem)

    def zero_run(dst, n):
        return pltpu.make_async_copy(zero_ref.at[pl.ds(0, n * nc), :],
                                     xs_ref.at[pl.ds(pl.multiple_of(dst * nc, nc), n * nc), :], sem)

    def zero_block(b):
        return pltpu.make_async_copy(zero_ref, xs_ref.at[pl.ds(pl.multiple_of(b * blk, blk), blk), :],
                                     sem)

    def pad_rows(e, do):
        first = pstart_ref[e] + cnt_ref[e]
        n = pstart_ref[e + 1] - first
        bit = EXPERT_BLOCK // 2
        while bit:
            pl.when((n & bit) != 0)(functools.partial(do, first, bit))
            first = first + (n & bit)
            bit //= 2

    def first_unused_block():
        return lax.div(pstart_ref[n_exp], EXPERT_BLOCK)

    @pl.when(i == 0)
    def _():
        zero_ref[...] = jnp.zeros_like(zero_ref)

        def per_expert(e, carry):
            pad_rows(e, lambda first, n: zero_run(first, n).start())
            return carry

        lax.fori_loop(0, n_exp, per_expert, 0)
        lax.fori_loop(first_unused_block(), n_blocks, lambda b, c: (zero_block(b).start(), c)[1], 0)

    def issue(r, carry):
        for k in range(TOP_K):
            row_copy(r, dest_ref[0, 0, r * TOP_K + k]).start(priority=k % 2)
        return carry

    lax.fori_loop(0, tm, issue, 0)
    all_rows = xs_ref.at[pl.ds(0, tm * TOP_K * nc), :]
    pltpu.make_async_copy(all_rows, all_rows, sem).wait()

    @pl.when(i == 0)
    def _():
        def per_expert(e, carry):
            pad_rows(e, lambda first, n: zero_run(0, n).wait())
            return carry

        lax.fori_loop(0, n_exp, per_expert, 0)
        lax.fori_loop(first_unused_block(), n_blocks, lambda b, c: (zero_block(0).wait(), c)[1], 0)


def _dispatch(xp, dest, pstart, counts, n_rows, nc, tm=256):
    t = xp.shape[0] // nc
    tm = min(tm, t)
    n_exp = counts.shape[0]
    assert n_rows >= tm * TOP_K
    grid_spec = pltpu.PrefetchScalarGridSpec(
        num_scalar_prefetch=2,
        grid=(t // tm,),
        in_specs=[pl.BlockSpec((1, 1, tm * TOP_K), lambda i, ps, cn: (i, 0, 0),
                               memory_space=pltpu.SMEM),
                  pl.BlockSpec((tm * nc, LANES), lambda i, ps, cn: (i, 0))],
        out_specs=pl.BlockSpec(memory_space=pl.ANY),
        scratch_shapes=[pltpu.VMEM((EXPERT_BLOCK * nc, LANES), xp.dtype),
                        pltpu.SemaphoreType.DMA(())],
    )
    assert n_rows % EXPERT_BLOCK == 0
    return pl.pallas_call(
        functools.partial(_dispatch_kernel, tm=tm, n_exp=n_exp, nc=nc,
                          n_blocks=n_rows // EXPERT_BLOCK),
        grid_spec=grid_spec,
        out_shape=jax.ShapeDtypeStruct((n_rows * nc, LANES), xp.dtype),
        compiler_params=_cparams(1),
        name="moe_dispatch",
    )(pstart, counts, dest.reshape(t // tm, 1, tm * TOP_K), xp)


def _expert_ffn_kernel(be_ref, nb_ref, x_ref, wg_ref, wu_ref, wd_ref, o_ref, wgb_ref, wub_ref,
                       wdb_ref, *, nc):
    i = pl.program_id(0)
    used = i < nb_ref[0]
    new_expert = jnp.logical_or(i == 0, be_ref[i] != be_ref[jnp.maximum(i - 1, 0)])

    @pl.when(jnp.logical_and(used, new_expert))
    def _():
        wgb_ref[...] = wg_ref[...].astype(BF16)
        wub_ref[...] = wu_ref[...].astype(BF16)
        wdb_ref[...] = wd_ref[...].astype(BF16)

    @pl.when(used)
    def _():
        xt = jnp.swapaxes(x_ref[...].reshape(EXPERT_BLOCK // nc, nc, nc, LANES), 1, 2)
        halves = [_unpack_pairs(xt[:, s].reshape(EXPERT_BLOCK, LANES)) for s in range(nc)]
        x = jnp.concatenate([h[0].astype(BF16) for h in halves] + [h[1].astype(BF16) for h in halves],
                            axis=1)
        hb = (_silu(_dot(x, wgb_ref[...])) * _dot(x, wub_ref[...])).astype(BF16)
        _store_slabs(o_ref, _pack_pairs(_dot(hb, wdb_ref[...])), EXPERT_BLOCK, nc)

    @pl.when(jnp.logical_not(used))
    def _():
        o_ref[...] = jnp.zeros_like(o_ref)


def _expert_ffn(xs, block_e, n_used, w_gate, w_up, w_down, layer, nc):
    _, _, d, f = w_gate.shape
    nb = xs.shape[0] // (EXPERT_BLOCK * nc)
    wspec = lambda a, b: pl.BlockSpec((None, None, a, b), lambda i, be, nu: (layer, be[i], 0, 0))
    grid_spec = pltpu.PrefetchScalarGridSpec(
        num_scalar_prefetch=2,
        grid=(nb,),
        in_specs=[
            pl.BlockSpec((EXPERT_BLOCK * nc, LANES), lambda i, be, nu: (jnp.minimum(i, nu[0] - 1), 0)),
            wspec(d, f), wspec(d, f), wspec(f, d),
        ],
        out_specs=pl.BlockSpec((EXPERT_BLOCK * nc, LANES), lambda i, be, nu: (i, 0)),
        scratch_shapes=[pltpu.VMEM((d, f), BF16), pltpu.VMEM((d, f), BF16), pltpu.VMEM((f, d), BF16)],
    )
    return pl.pallas_call(
        functools.partial(_expert_ffn_kernel, nc=nc),
        grid_spec=grid_spec,
        out_shape=jax.ShapeDtypeStruct(xs.shape, xs.dtype),
        compiler_params=_cparams(1),
        name="expert_ffn",
    )(block_e, n_used, xs, w_gate, w_up, w_down)


def _combine_kernel(dest_ref, destn_ref, wts_ref, hf_ref, hb_ref, sg_ref, su_ref, sd_ref, g_ref,
                    b_ref, ys_ref, of_ref, ob_ref, buf_ref, sems, *, tm, nc, alpha):
    i = pl.program_id(0)
    slot = i % 2

    def slab_copy(sl, k, r, src):
        dst_row = pl.multiple_of((k * tm + r) * nc, nc)
        return pltpu.make_async_copy(ys_ref.at[pl.ds(pl.multiple_of(src * nc, nc), nc), :],
                                     buf_ref.at[sl, pl.ds(dst_row, nc), :], sems.at[sl])

    def gather(sl, rows_ref):
        def issue(r, carry):
            for k in range(TOP_K):
                slab_copy(sl, k, r, rows_ref[0, 0, r * TOP_K + k]).start(priority=k % 2)
            return carry

        lax.fori_loop(0, tm, issue, 0)

    @pl.when(i == 0)
    def _():
        gather(slot, dest_ref)

    @pl.when(i + 1 < pl.num_programs(0))
    def _():
        gather(1 - slot, destn_ref)

    xb = hb_ref[...]
    shared = _dot((_silu(_dot(xb, sg_ref[...])) * _dot(xb, su_ref[...])).astype(BF16), sd_ref[...])
    z = alpha * hf_ref[...] + shared

    pltpu.make_async_copy(buf_ref.at[slot], buf_ref.at[slot], sems.at[slot]).wait()
    wts = wts_ref[...]
    his = [None] * nc
    los = [None] * nc
    for k in range(TOP_K):
        wk = wts[:, k:k + 1]
        for s in range(nc):
            hi, lo = _unpack_pairs(buf_ref[slot, pl.ds(k * tm * nc + s, tm, stride=nc), :])
            his[s] = wk * hi if k == 0 else his[s] + wk * hi
            los[s] = wk * lo if k == 0 else los[s] + wk * lo
    out = _layer_norm(z + jnp.concatenate(his + los, axis=1), g_ref[...], b_ref[...])
    of_ref[...] = out
    ob_ref[...] = out.astype(BF16)


def _combine(ys, dest, wts, h_f32, h_bf16, sh_gate, sh_up, sh_down, g, b, alpha, nc, tm=256):
    t, d = h_f32.shape
    f = sh_gate.shape[1]
    tm = min(tm, t)
    n = t // tm
    smem = lambda im: pl.BlockSpec((1, 1, tm * TOP_K), im, memory_space=pltpu.SMEM)
    tile = lambda w: pl.BlockSpec((tm, w), lambda i: (i, 0))
    const = lambda shape: pl.BlockSpec(shape, lambda i: (0, 0))
    rows = dest.reshape(n, 1, tm * TOP_K)
    return pl.pallas_call(
        functools.partial(_combine_kernel, tm=tm, nc=nc, alpha=alpha),
        grid=(n,),
        in_specs=[smem(lambda i: (i, 0, 0)), smem(lambda i: (jnp.minimum(i + 1, n - 1), 0, 0)),
                  tile(LANES), tile(d), tile(d),
                  const((d, f)), const((d, f)), const((f, d)), const((1, d)), const((1, d)),
                  pl.BlockSpec(memory_space=pl.ANY)],
        out_specs=[tile(d), tile(d)],
        out_shape=[jax.ShapeDtypeStruct((t, d), F32), jax.ShapeDtypeStruct((t, d), BF16)],
        scratch_shapes=[pltpu.VMEM((2, TOP_K * tm * nc, LANES), ys.dtype),
                        pltpu.SemaphoreType.DMA((2,))],
        compiler_params=_cparams(1),
        name="moe_combine",
    )(rows, rows, wts, h_f32, h_bf16, sh_gate, sh_up, sh_down, g.reshape(1, d), b.reshape(1, d), ys)


def _moe_ln(h_f32, h_bf16, h_slabs, router_w, router_bias, w_gate, w_up, w_down, layer, sh_gate,
            sh_up, sh_down, g, b, alpha):
    t, d = h_f32.shape
    n_exp = router_w.shape[1]
    nc = _slab_rows(d)
    idx, wts, rank, counts = _router(h_f32, router_w, router_bias)
    idx, rank = idx[:, :TOP_K], rank[:, :TOP_K]

    counts = counts[0, :n_exp].astype(jnp.int32)
    padded = (counts + EXPERT_BLOCK - 1) // EXPERT_BLOCK * EXPERT_BLOCK
    pends = jnp.cumsum(padded)
    pstart = jnp.concatenate([jnp.zeros((1,), jnp.int32), pends]).astype(jnp.int32)
    n_blocks = t * TOP_K // EXPERT_BLOCK + n_exp
    block_start = jnp.arange(n_blocks, dtype=jnp.int32) * EXPERT_BLOCK
    block_e = jnp.sum((pends[None, :] <= block_start[:, None]).astype(jnp.int32), axis=1)
    block_e = jnp.minimum(block_e, n_exp - 1)
    n_used = (pends[-1:] // EXPERT_BLOCK).astype(jnp.int32)
    group_start = jnp.sum(jnp.where(idx[:, :, None] == jnp.arange(n_exp, dtype=jnp.int32),
                                    pstart[:n_exp], 0), axis=-1)
    dest = group_start + rank

    xs = _dispatch(h_slabs, dest, pstart, counts, n_blocks * EXPERT_BLOCK, nc)
    ys = _expert_ffn(xs, block_e, n_used, w_gate, w_up, w_down, layer, nc)
    return _combine(ys, dest, wts, h_f32, h_bf16, sh_gate.astype(BF16), sh_up.astype(BF16),
                    sh_down.astype(BF16), g, b, alpha, nc)


def kernel(x, even_w_in, ret_gn_g, rg_conv_w, rg_conv_b, rg_w_a, rg_b_a, rg_w_i, rg_b_i, rg_lambda,
           even_w_out, fox_w_in, fox_b_f, fox_w_out, ln_g, ln_b, router_w, router_bias, exp_w_gate,
           exp_w_up, exp_w_down, sh_w_gate, sh_w_up, sh_w_down):
    batch, seq, d = x.shape
    depth = ln_g.shape[0]
    alpha = (2 * depth) ** 0.25
    ret_width = ret_gn_g.shape[1]
    fox_heads = fox_b_f.shape[1]

    h_f32 = x.reshape(batch * seq, d)
    h_bf16 = h_f32.astype(BF16)
    for layer in range(depth):
        j = layer // 2
        if layer % 2 == 0:
            proj = _matmul(h_bf16, even_w_in[j].astype(BF16), BF16)
            y_ret = _retention(proj, ret_gn_g[j], batch, seq, chunk=256)
            y_rnn = _rglru(proj, rg_conv_w[j], rg_conv_b[j], rg_w_a[j], rg_b_a[j], rg_w_i[j],
                           rg_b_i[j], rg_lambda[j], batch, seq, tt=256)
            w_out = even_w_out[j].astype(BF16)
            ys, ws = [y_ret, y_rnn], [w_out[:ret_width], w_out[ret_width:]]
        else:
            w_in = fox_w_in[j]
            qkv = _matmul(h_bf16, w_in[:, :3 * d].astype(BF16), BF16)
            w_f = jnp.pad(w_in[:, 3 * d:], ((0, 0), (0, LANES - fox_heads))).astype(BF16)
            fl = _matmul(h_bf16, w_f, F32)
            b_f = jnp.pad(fox_b_f[j], (0, LANES - fox_heads)).reshape(1, LANES)
            kc = _forget_key_bias(_logf_cumsum(fl, b_f, batch, seq), batch, seq, fox_heads)
            o = _fox_attention(qkv, kc, batch, seq, fox_heads, tq=512)
            ys, ws = [o], [fox_w_out[j].astype(BF16)]
        h_f32, h_bf16, h_slabs = _proj_ln(ys, ws, h_f32, ln_g[layer, 0], ln_b[layer, 0], alpha)
        h_f32, h_bf16 = _moe_ln(h_f32, h_bf16, h_slabs, router_w[layer], router_bias[layer],
                                exp_w_gate, exp_w_up, exp_w_down, layer,
                                sh_w_gate[layer], sh_w_up[layer], sh_w_down[layer],
                                ln_g[layer, 1], ln_b[layer, 1], alpha)
    return h_f32.reshape(batch, seq, d)
```

```python
import functools
import math

import numpy as np
import jax
import jax.numpy as jnp
from jax import lax
from jax.experimental import pallas as pl
from jax.experimental.pallas import tpu as pltpu

F32 = jnp.float32
BF16 = jnp.bfloat16

HEAD_DIM = 128
TOP_K = 8
ROUTED_SCALE = 2.5
ROPE_BASE = 10000.0
RG_C = 8.0
LN_EPS = 1e-5
CONV_TAIL = 8

V7X_VMEM_BYTES = 64 * 1024 * 1024
VMEM_LIMIT = V7X_VMEM_BYTES - 8 * 1024 * 1024
LANES = 128
SUBLANES = 8

EXPERT_BLOCK = 512


def _cparams(grid_rank):
    return pltpu.CompilerParams(dimension_semantics=("arbitrary",) * grid_rank,
                                vmem_limit_bytes=VMEM_LIMIT)


def _dot(a, b):
    return jnp.dot(a, b, preferred_element_type=F32)


def _silu(x):
    return x * jax.nn.sigmoid(x)


def _softplus(x):
    return jnp.maximum(x, 0.0) + jnp.log1p(jnp.exp(-jnp.abs(x)))


def _layer_norm(z, g, b):
    mu = jnp.mean(z, axis=-1, keepdims=True)
    zc = z - mu
    var = jnp.mean(zc * zc, axis=-1, keepdims=True)
    return zc * lax.rsqrt(var + LN_EPS) * g + b


def _mm_kernel(x_ref, w_ref, o_ref):
    o_ref[...] = _dot(x_ref[...], w_ref[...]).astype(o_ref.dtype)


def _tile(n, target):
    best = LANES
    for c in range(LANES, min(n, target) + 1, LANES):
        if n % c == 0:
            best = c
    assert n % best == 0
    return best


def _matmul(x, w, out_dtype, tm=1024, tn=1024):
    m, k = x.shape
    n = w.shape[1]
    tm, tn = _tile(m, tm), _tile(n, tn)
    return pl.pallas_call(
        _mm_kernel,
        grid=(m // tm, n // tn),
        in_specs=[pl.BlockSpec((tm, k), lambda i, j: (i, 0)),
                  pl.BlockSpec((k, tn), lambda i, j: (0, j))],
        out_specs=pl.BlockSpec((tm, tn), lambda i, j: (i, j)),
        out_shape=jax.ShapeDtypeStruct((m, n), out_dtype),
        compiler_params=_cparams(2),
        name="in_proj",
    )(x, w)


def _slab_rows(d):
    assert d % (2 * LANES) == 0
    return d // (2 * LANES)


def _pack_pairs(x):
    half = x.shape[1] // 2
    hi = lax.bitcast_convert_type(x[:, :half].astype(BF16).astype(F32), jnp.uint32)
    lo = lax.bitcast_convert_type(x[:, half:].astype(BF16).astype(F32), jnp.uint32)
    return hi | (lo >> 16)


def _unpack_pairs(w):
    hi = lax.bitcast_convert_type(w & jnp.uint32(0xFFFF0000), F32)
    lo = lax.bitcast_convert_type(w << 16, F32)
    return hi, lo


def _store_slabs(ref, words, rows, nc):
    for s in range(nc):
        ref[pl.ds(s, rows, stride=nc), :] = words[:, s * LANES:(s + 1) * LANES]


def _proj_ln_kernel(*refs, n_in, alpha, tm, nc):
    ys, ws = refs[:n_in], refs[n_in:2 * n_in]
    h_ref, g_ref, b_ref, of_ref, ob_ref, op_ref = refs[2 * n_in:]
    acc = _dot(ys[0][...], ws[0][...])
    for y_ref, w_ref in zip(ys[1:], ws[1:]):
        acc = acc + _dot(y_ref[...], w_ref[...])
    out = _layer_norm(alpha * h_ref[...] + acc, g_ref[...], b_ref[...])
    of_ref[...] = out
    ob_ref[...] = out.astype(BF16)
    _store_slabs(op_ref, _pack_pairs(out), tm, nc)


def _proj_ln(ys, ws, h, g, b, alpha, tm=512):
    t, d = h.shape
    tm = min(tm, t)
    n_in = len(ys)
    nc = _slab_rows(d)
    in_specs = ([pl.BlockSpec((tm, y.shape[1]), lambda i: (i, 0)) for y in ys]
                + [pl.BlockSpec(w.shape, lambda i: (0, 0), pipeline_mode=pl.Buffered(1)) for w in ws]
                + [pl.BlockSpec((tm, d), lambda i: (i, 0)),
                   pl.BlockSpec((1, d), lambda i: (0, 0)),
                   pl.BlockSpec((1, d), lambda i: (0, 0))])
    return pl.pallas_call(
        functools.partial(_proj_ln_kernel, n_in=n_in, alpha=alpha, tm=tm, nc=nc),
        grid=(t // tm,),
        in_specs=in_specs,
        out_specs=[pl.BlockSpec((tm, d), lambda i: (i, 0)),
                   pl.BlockSpec((tm, d), lambda i: (i, 0)),
                   pl.BlockSpec((tm * nc, LANES), lambda i: (i, 0))],
        out_shape=[jax.ShapeDtypeStruct((t, d), F32), jax.ShapeDtypeStruct((t, d), BF16),
                   jax.ShapeDtypeStruct((t * nc, LANES), jnp.uint32)],
        compiler_params=_cparams(1),
        name="out_proj_ln",
    )(*ys, *ws, h, g.reshape(1, d), b.reshape(1, d))


def _retention_kernel(q_ref, k_ref, v_ref, g_ref, cos_ref, sin_ref, intra_ref, qdec_ref, kdec_ref,
                      gn_ref, o_ref, state_ref, *, heads, chunk_decay):
    @pl.when(pl.program_id(1) == 0)
    def _():
        state_ref[...] = jnp.zeros_like(state_ref)

    cos, sin = cos_ref[...], sin_ref[...]
    half = HEAD_DIM // 2
    for h in range(heads):
        sl = slice(h * HEAD_DIM, (h + 1) * HEAD_DIM)
        q = q_ref[:, sl].astype(F32)
        k = k_ref[:, sl].astype(F32)
        q = q * cos + pltpu.roll(q, half, 1) * sin
        k = (k * cos + pltpu.roll(k, half, 1) * sin) * (HEAD_DIM ** -0.5)
        v = v_ref[:, sl]
        s = lax.dot_general(q.astype(BF16), k.astype(BF16), (((1,), (1,)), ((), ())),
                            preferred_element_type=F32) * intra_ref[h]
        state = state_ref[h]
        y = _dot(s.astype(BF16), v) + _dot((q * qdec_ref[h]).astype(BF16), state.astype(BF16))
        kv = lax.dot_general((k * kdec_ref[h]).astype(BF16), v, (((0,), (0,)), ((), ())),
                             preferred_element_type=F32)
        state_ref[h] = state * chunk_decay[h] + kv
        mu = jnp.mean(y, axis=-1, keepdims=True)
        yc = y - mu
        var = jnp.mean(yc * yc, axis=-1, keepdims=True)
        yn = yc * lax.rsqrt(var + LN_EPS) * gn_ref[:, sl]
        o_ref[:, sl] = (yn * _silu(g_ref[:, sl].astype(F32))).astype(BF16)


def _retention(proj, gn_g, batch, seq, chunk):
    t = proj.shape[0]
    width = gn_g.shape[0]
    heads = width // HEAD_DIM
    chunk = min(chunk, seq)
    n_chunks = seq // chunk
    half = HEAD_DIM // 2

    pos = jnp.arange(seq, dtype=F32)
    inv = ROPE_BASE ** (-jnp.arange(half, dtype=F32) / half)
    ang = pos[:, None] * inv[None, :]
    cos2 = jnp.concatenate([jnp.cos(ang), jnp.cos(ang)], axis=1)
    sin2 = jnp.concatenate([-jnp.sin(ang), jnp.sin(ang)], axis=1)

    log_g = jnp.log1p(-jnp.exp2(-5.0 - jnp.arange(heads, dtype=F32)))
    i = jnp.arange(chunk, dtype=F32)
    rel = i[:, None] - i[None, :]
    intra = jnp.where(rel >= 0, jnp.exp(log_g[:, None, None] * jnp.maximum(rel, 0.0)), 0.0)
    kdec = jnp.exp(log_g[:, None] * (chunk - 1.0 - i)[None, :])
    qdec = jnp.exp(log_g[:, None] * (i + 1.0)[None, :])
    kdec = jnp.broadcast_to(kdec[:, :, None], (heads, chunk, HEAD_DIM))
    qdec = jnp.broadcast_to(qdec[:, :, None], (heads, chunk, HEAD_DIM))
    chunk_decay = tuple(math.exp(math.log1p(-2.0 ** (-5.0 - h)) * chunk) for h in range(heads))

    def col(c):
        return pl.BlockSpec((chunk, width), lambda b, n: (b * n_chunks + n, c))

    const3 = lambda shape: pl.BlockSpec(shape, lambda b, n: (0, 0, 0))
    return pl.pallas_call(
        functools.partial(_retention_kernel, heads=heads, chunk_decay=chunk_decay),
        grid=(batch, n_chunks),
        in_specs=[col(0), col(1), col(2), col(3),
                  pl.BlockSpec((chunk, HEAD_DIM), lambda b, n: (n, 0)),
                  pl.BlockSpec((chunk, HEAD_DIM), lambda b, n: (n, 0)),
                  const3((heads, chunk, chunk)),
                  const3((heads, chunk, HEAD_DIM)),
                  const3((heads, chunk, HEAD_DIM)),
                  pl.BlockSpec((1, width), lambda b, n: (0, 0))],
        out_specs=pl.BlockSpec((chunk, width), lambda b, n: (b * n_chunks + n, 0)),
        out_shape=jax.ShapeDtypeStruct((t, width), BF16),
        scratch_shapes=[pltpu.VMEM((heads, HEAD_DIM, HEAD_DIM), F32)],
        compiler_params=_cparams(2),
        name="retention",
    )(proj, proj, proj, proj, cos2, sin2, intra, qdec, kdec, gn_g.reshape(1, width))


def _rglru_kernel(u_ref, gate_ref, cw_ref, cb_ref, wa_ref, ba_ref, wi_ref, bi_ref, lam_ref,
                  o_ref, uext_ref, hc_ref, *, tt, nblk):
    n = pl.program_id(1)

    @pl.when(n == 0)
    def _():
        uext_ref[0:CONV_TAIL, :] = jnp.zeros((CONV_TAIL, uext_ref.shape[1]), F32)
        hc_ref[...] = jnp.zeros_like(hc_ref)

    @pl.when(n > 0)
    def _():
        uext_ref[0:CONV_TAIL, :] = uext_ref[tt:tt + CONV_TAIL, :]

    u = u_ref[...].astype(F32)
    uext_ref[CONV_TAIL:CONV_TAIL + tt, :] = u
    uc = (cb_ref[...] + cw_ref[3:4, :] * u
          + cw_ref[2:3, :] * uext_ref[CONV_TAIL - 1:CONV_TAIL - 1 + tt, :]
          + cw_ref[1:2, :] * uext_ref[CONV_TAIL - 2:CONV_TAIL - 2 + tt, :]
          + cw_ref[0:1, :] * uext_ref[CONV_TAIL - 3:CONV_TAIL - 3 + tt, :])
    ucb = uc.astype(BF16)
    ra = jnp.concatenate(
        [_dot(ucb[:, i * HEAD_DIM:(i + 1) * HEAD_DIM], wa_ref[i]) for i in range(nblk)], axis=1)
    ia = jnp.concatenate(
        [_dot(ucb[:, i * HEAD_DIM:(i + 1) * HEAD_DIM], wi_ref[i]) for i in range(nblk)], axis=1)
    r = jax.nn.sigmoid(ra + ba_ref[...])
    ig = jax.nn.sigmoid(ia + bi_ref[...])
    log_a = (-RG_C) * r * _softplus(-lam_ref[...])
    a = jnp.exp(log_a)
    bt = jnp.sqrt(-jnp.tanh(log_a) * (a * a + 1.0)) * (ig * uc)

    row = lax.broadcasted_iota(jnp.int32, a.shape, 0) & (SUBLANES - 1)
    d = 1
    while d < SUBLANES:
        keep = row >= d
        bt = jnp.where(keep, a * pltpu.roll(bt, d, 0) + bt, bt)
        a = jnp.where(keep, a * pltpu.roll(a, d, 0), a)
        d *= 2
    carry = hc_ref[...]
    groups = []
    for g in range(tt // SUBLANES):
        rows = slice(g * SUBLANES, (g + 1) * SUBLANES)
        hg = a[rows] * carry + bt[rows]
        groups.append(hg)
        carry = hg[SUBLANES - 1:SUBLANES, :]
    h = jnp.concatenate(groups, axis=0)
    hc_ref[...] = carry

    g = gate_ref[...].astype(F32)
    gelu = 0.5 * g * (1.0 + jnp.tanh(math.sqrt(2.0 / math.pi) * (g + 0.044715 * (g * g * g))))
    o_ref[...] = (h * gelu).astype(BF16)


def _rglru(proj, conv_w, conv_b, w_a, b_a, w_i, b_i, lam, batch, seq, tt):
    t = proj.shape[0]
    width = conv_w.shape[1]
    nblk = w_a.shape[0]
    tt = min(tt, seq)
    n_t = seq // tt
    row = lambda x: x.reshape(1, width)
    vec = pl.BlockSpec((1, width), lambda b, n: (0, 0))
    blk = pl.BlockSpec((nblk, HEAD_DIM, HEAD_DIM), lambda b, n: (0, 0, 0))
    return pl.pallas_call(
        functools.partial(_rglru_kernel, tt=tt, nblk=nblk),
        grid=(batch, n_t),
        in_specs=[pl.BlockSpec((tt, width), lambda b, n: (b * n_t + n, 4)),
                  pl.BlockSpec((tt, width), lambda b, n: (b * n_t + n, 5)),
                  pl.BlockSpec((4, width), lambda b, n: (0, 0)),
                  vec, blk, vec, blk, vec, vec],
        out_specs=pl.BlockSpec((tt, width), lambda b, n: (b * n_t + n, 0)),
        out_shape=jax.ShapeDtypeStruct((t, width), BF16),
        scratch_shapes=[pltpu.VMEM((tt + CONV_TAIL, width), F32), pltpu.VMEM((1, width), F32)],
        compiler_params=_cparams(2),
        name="rglru",
    )(proj, proj, conv_w, row(conv_b), w_a.astype(BF16), row(b_a), w_i.astype(BF16), row(b_i),
      row(lam))


FORGET_SPLIT = 3


def _bf16_head(x):
    bits = lax.bitcast_convert_type(x, jnp.uint32) & jnp.uint32(0xFFFF0000)
    return lax.bitcast_convert_type(bits, F32)


def _logf_cumsum_kernel(fl_ref, bf_ref, *o_refs, seq):
    z = fl_ref[...] + bf_ref[...]
    c = jnp.minimum(z, 0.0) - jnp.log1p(jnp.exp(-jnp.abs(z)))
    row = lax.broadcasted_iota(jnp.int32, c.shape, 0)
    d = 1
    while d < seq:
        c = c + jnp.where(row >= d, pltpu.roll(c, d, 0), 0.0)
        d *= 2
    x = c * (-(HEAD_DIM ** 0.5))
    for o_ref in o_refs:
        piece = _bf16_head(x)
        o_ref[...] = piece.astype(BF16)
        x = x - piece


def _logf_cumsum(fl, b_f, batch, seq):
    tile = pl.BlockSpec((seq, LANES), lambda b: (b, 0))
    return pl.pallas_call(
        functools.partial(_logf_cumsum_kernel, seq=seq),
        grid=(batch,),
        in_specs=[tile, pl.BlockSpec((1, LANES), lambda b: (0, 0))],
        out_specs=[tile] * FORGET_SPLIT,
        out_shape=[jax.ShapeDtypeStruct(fl.shape, BF16)] * FORGET_SPLIT,
        compiler_params=_cparams(1),
        name="logf_cumsum",
    )(fl, b_f)


def _fox_kernel(q_ref, k_ref, v_ref, kc_ref, o_ref, kext_ref, vext_ref, m_ref, l_ref, acc_ref, *,
                tq, seq, group):
    qi = pl.program_id(2)
    head = lambda g: slice(g * HEAD_DIM, (g + 1) * HEAD_DIM)

    @pl.when(qi == 0)
    def _():
        for g in range(group):
            kext_ref[g, :, :HEAD_DIM] = k_ref[:, head(g)]
            kext_ref[g, :, HEAD_DIM:] = kc_ref[g]
            vext_ref[g, :, :HEAD_DIM] = v_ref[:, head(g)]
            vext_ref[g, :, HEAD_DIM:] = jnp.ones((seq, HEAD_DIM), BF16)

    lane = lax.broadcasted_iota(jnp.int32, (tq, HEAD_DIM), 1)
    ones = jnp.where(lane < FORGET_SPLIT, 1.0, 0.0).astype(BF16)
    q2 = [jnp.concatenate([q_ref[:, head(g)], ones], axis=1) for g in range(group)]
    c = (HEAD_DIM ** -0.5) * math.log2(math.e)
    reps = tq // HEAD_DIM

    def logits(g, kb):
        off = pl.multiple_of(kb * tq, tq)
        return lax.dot_general(q2[g], kext_ref[g, pl.ds(off, tq), :], (((1,), (1,)), ((), ())),
                               preferred_element_type=F32)

    def fold(g, kb, masked):
        u = logits(g, kb)
        if masked:
            row = lax.broadcasted_iota(jnp.int32, u.shape, 0)
            col = lax.broadcasted_iota(jnp.int32, u.shape, 1)
            u = jnp.where(col <= row, u, -jnp.inf)
        m = m_ref[g]
        m_new = jnp.maximum(m, jnp.broadcast_to(jnp.max(u, axis=-1, keepdims=True), m.shape))
        p = jnp.exp2((u - jnp.concatenate([m_new] * reps, axis=1)) * c)
        alpha = jnp.exp2((m - m_new) * c)
        off = pl.multiple_of(kb * tq, tq)
        pv = _dot(p.astype(BF16), vext_ref[g, pl.ds(off, tq), :])
        m_ref[g] = m_new
        l_ref[g] = alpha * l_ref[g] + pv[:, HEAD_DIM:]
        acc_ref[g] = alpha * acc_ref[g] + pv[:, :HEAD_DIM]

    m_ref[...] = jnp.full(m_ref.shape, -jnp.inf, F32)
    l_ref[...] = jnp.zeros(l_ref.shape, F32)
    acc_ref[...] = jnp.zeros(acc_ref.shape, F32)

    def body(kb, carry):
        for g in range(group):
            fold(g, kb, False)
        return carry

    lax.fori_loop(0, qi, body, 0)
    for g in range(group):
        fold(g, qi, True)
        o_ref[:, head(g)] = (acc_ref[g] / l_ref[g]).astype(BF16)


def _fox_attention(qkv, kc, batch, seq, heads, tq, group=4):
    t = qkv.shape[0]
    tq = min(tq, max(HEAD_DIM, seq // 4))
    nq = seq // tq
    assert tq % HEAD_DIM == 0 and seq % tq == 0 and heads % group == 0
    hg = heads // group
    w = group * HEAD_DIM
    return pl.pallas_call(
        functools.partial(_fox_kernel, tq=tq, seq=seq, group=group),
        grid=(batch, hg, nq),
        in_specs=[
            pl.BlockSpec((tq, w), lambda b, h, qi: (b * nq + qi, h)),
            pl.BlockSpec((seq, w), lambda b, h, qi: (b, hg + h)),
            pl.BlockSpec((seq, w), lambda b, h, qi: (b, 2 * hg + h)),
            pl.BlockSpec((group, seq, HEAD_DIM), lambda b, h, qi: (b * hg + h, 0, 0)),
        ],
        out_specs=pl.BlockSpec((tq, w), lambda b, h, qi: (b * nq + qi, h)),
        out_shape=jax.ShapeDtypeStruct((t, heads * HEAD_DIM), BF16),
        scratch_shapes=[pltpu.VMEM((group, seq, 2 * HEAD_DIM), BF16),
                        pltpu.VMEM((group, seq, 2 * HEAD_DIM), BF16)]
        + [pltpu.VMEM((group, tq, HEAD_DIM), F32)] * 3,
        compiler_params=_cparams(3),
        name="fox_attention",
    )(qkv, qkv, qkv, kc)


def _forget_key_bias(pieces, batch, seq, heads):
    kc = jnp.stack([p.reshape(batch, seq, LANES)[:, :, :heads] for p in pieces], axis=-1)
    kc = jnp.transpose(kc, (0, 2, 1, 3)).reshape(batch * heads, seq, FORGET_SPLIT)
    return jnp.pad(kc, ((0, 0), (0, 0), (0, HEAD_DIM - FORGET_SPLIT)))


def _router_kernel(h_ref, rw_ref, rb_ref, idx_ref, wts_ref, rank_ref, cnt_ref, *, tm):
    @pl.when(pl.program_id(0) == 0)
    def _():
        cnt_ref[...] = jnp.zeros_like(cnt_ref)

    h = h_ref[...]
    h_hi = _bf16_head(h)
    first_order = _dot(h_hi.astype(BF16), rw_ref[...])
    logits = (first_order[:, :LANES] + first_order[:, LANES:]
              + _dot((h - h_hi).astype(BF16), rw_ref[:, :LANES]))
    scores = jax.nn.sigmoid(logits)
    sel = scores + rb_ref[...]
    lane = lax.broadcasted_iota(jnp.int32, sel.shape, 1)
    lane_f = lane.astype(F32)
    idx = jnp.zeros(sel.shape, F32)
    wts = jnp.zeros(sel.shape, F32)
    chosen = jnp.zeros(sel.shape, F32)
    hits = []
    for k in range(TOP_K):
        top = jnp.max(sel, axis=-1, keepdims=True)
        first = jnp.min(jnp.where(sel == top, lane_f, float(LANES)), axis=-1, keepdims=True)
        hit = lane_f == first
        hits.append(hit)
        idx = jnp.where(lane == k, first, idx)
        wts = jnp.where(lane == k, jnp.sum(jnp.where(hit, scores, 0.0), axis=-1, keepdims=True), wts)
        chosen = jnp.where(hit, 1.0, chosen)
        sel = jnp.where(hit, -jnp.inf, sel)
    wts = wts / jnp.sum(wts, axis=-1, keepdims=True) * ROUTED_SCALE

    r = lax.broadcasted_iota(jnp.int32, (tm, tm), 0)
    c = lax.broadcasted_iota(jnp.int32, (tm, tm), 1)
    before = jnp.where(c < r, 1.0, 0.0).astype(BF16)
    rank_all = _dot(before, chosen.astype(BF16)) + cnt_ref[...]
    rank = jnp.zeros(sel.shape, F32)
    for k in range(TOP_K):
        rank = jnp.where(lane == k, jnp.sum(jnp.where(hits[k], rank_all, 0.0), axis=-1, keepdims=True),
                         rank)
    cnt_ref[...] = cnt_ref[...] + jnp.sum(chosen, axis=0, keepdims=True)
    idx_ref[...] = idx.astype(jnp.int32)
    wts_ref[...] = wts
    rank_ref[...] = rank.astype(jnp.int32)


def _router(h, router_w, router_bias, tm=256):
    t, d = h.shape
    e = router_w.shape[1]
    tm = min(tm, t)
    rw = jnp.pad(router_w, ((0, 0), (0, LANES - e)))
    rw_hi = _bf16_head(rw)
    rw = jnp.concatenate([rw_hi.astype(BF16), (rw - rw_hi).astype(BF16)], axis=1)
    rb = jnp.pad(router_bias.astype(F32), (0, LANES - e), constant_values=-jnp.inf).reshape(1, LANES)
    tile = pl.BlockSpec((tm, LANES), lambda i: (i, 0))
    return pl.pallas_call(
        functools.partial(_router_kernel, tm=tm),
        grid=(t // tm,),
        in_specs=[pl.BlockSpec((tm, d), lambda i: (i, 0)),
                  pl.BlockSpec((d, 2 * LANES), lambda i: (0, 0)),
                  pl.BlockSpec((1, LANES), lambda i: (0, 0))],
        out_specs=[tile, tile, tile, pl.BlockSpec((1, LANES), lambda i: (0, 0))],
        out_shape=[jax.ShapeDtypeStruct((t, LANES), jnp.int32),
                   jax.ShapeDtypeStruct((t, LANES), F32),
                   jax.ShapeDtypeStruct((t, LANES), jnp.int32),
                   jax.ShapeDtypeStruct((1, LANES), F32)],
        compiler_params=_cparams(1),
        name="router_topk",
    )(h, rw, rb)


def _dispatch_kernel(pstart_ref, cnt_ref, dest_ref, x_ref, xs_ref, zero_ref, sem, *, tm, n_exp, nc,
                     n_blocks):
    i = pl.program_id(0)
    blk = EXPERT_BLOCK * nc

    def row_copy(r, dst):
        return pltpu.make_async_copy(x_ref.at[pl.ds(pl.multiple_of(r * nc, nc), nc), :],
                                     xs_ref.at[pl.ds(pl.multiple_of(dst * nc, nc), nc), :], sem)

    def zero_run(dst, n):
        return pltpu.make_async_copy(zero_ref.at[pl.ds(0, n * nc), :],
                                     xs_ref.at[pl.ds(pl.multiple_of(dst * nc, nc), n * nc), :], sem)

    def zero_block(b):
        return pltpu.make_async_copy(zero_ref, xs_ref.at[pl.ds(pl.multiple_of(b * blk, blk), blk), :],
                                     sem)

    def pad_rows(e, do):
        first = pstart_ref[e] + cnt_ref[e]
        n = pstart_ref[e + 1] - first
        bit = EXPERT_BLOCK // 2
        while bit:
            pl.when((n & bit) != 0)(functools.partial(do, first, bit))
            first = first + (n & bit)
            bit //= 2

    def first_unused_block():
        return lax.div(pstart_ref[n_exp], EXPERT_BLOCK)

    @pl.when(i == 0)
    def _():
        zero_ref[...] = jnp.zeros_like(zero_ref)

        def per_expert(e, carry):
            pad_rows(e, lambda first, n: zero_run(first, n).start())
            return carry

        lax.fori_loop(0, n_exp, per_expert, 0)
        lax.fori_loop(first_unused_block(), n_blocks, lambda b, c: (zero_block(b).start(), c)[1], 0)

    def issue(r, carry):
        for k in range(TOP_K):
            row_copy(r, dest_ref[0, 0, r * TOP_K + k]).start(priority=k % 2)
        return carry

    lax.fori_loop(0, tm, issue, 0)
    all_rows = xs_ref.at[pl.ds(0, tm * TOP_K * nc), :]
    pltpu.make_async_copy(all_rows, all_rows, sem).wait()

    @pl.when(i == 0)
    def _():
        def per_expert(e, carry):
            pad_rows(e, lambda first, n: zero_run(0, n).wait())
            return carry

        lax.fori_loop(0, n_exp, per_expert, 0)
        lax.fori_loop(first_unused_block(), n_blocks, lambda b, c: (zero_block(0).wait(), c)[1], 0)


def _dispatch(xp, dest, pstart, counts, n_rows, nc, tm=256):
    t = xp.shape[0] // nc
    tm = min(tm, t)
    n_exp = counts.shape[0]
    assert n_rows >= tm * TOP_K
    grid_spec = pltpu.PrefetchScalarGridSpec(
        num_scalar_prefetch=2,
        grid=(t // tm,),
        in_specs=[pl.BlockSpec((1, 1, tm * TOP_K), lambda i, ps, cn: (i, 0, 0),
                               memory_space=pltpu.SMEM),
                  pl.BlockSpec((tm * nc, LANES), lambda i, ps, cn: (i, 0))],
        out_specs=pl.BlockSpec(memory_space=pl.ANY),
        scratch_shapes=[pltpu.VMEM((EXPERT_BLOCK * nc, LANES), xp.dtype),
                        pltpu.SemaphoreType.DMA(())],
    )
    assert n_rows % EXPERT_BLOCK == 0
    return pl.pallas_call(
        functools.partial(_dispatch_kernel, tm=tm, n_exp=n_exp, nc=nc,
                          n_blocks=n_rows // EXPERT_BLOCK),
        grid_spec=grid_spec,
        out_shape=jax.ShapeDtypeStruct((n_rows * nc, LANES), xp.dtype),
        compiler_params=_cparams(1),
        name="moe_dispatch",
    )(pstart, counts, dest.reshape(t // tm, 1, tm * TOP_K), xp)


def _expert_ffn_kernel(be_ref, nb_ref, nv_ref, x_ref, wg_ref, wu_ref, wd_ref, o_ref, wgb_ref,
                       wub_ref, wdb_ref, *, nc):
    i = pl.program_id(0)
    used = i < nb_ref[0]
    new_expert = jnp.logical_or(i == 0, be_ref[i] != be_ref[jnp.maximum(i - 1, 0)])
    half = EXPERT_BLOCK // 2
    short = nv_ref[i] <= half

    @pl.when(jnp.logical_and(used, new_expert))
    def _():
        wgb_ref[...] = wg_ref[...].astype(BF16)
        wub_ref[...] = wu_ref[...].astype(BF16)
        wdb_ref[...] = wd_ref[...].astype(BF16)

    def ffn(rows):
        xt = jnp.swapaxes(x_ref[0:rows * nc, :].reshape(rows // nc, nc, nc, LANES), 1, 2)
        halves = [_unpack_pairs(xt[:, s].reshape(rows, LANES)) for s in range(nc)]
        x = jnp.concatenate([h[0].astype(BF16) for h in halves] + [h[1].astype(BF16) for h in halves],
                            axis=1)
        hb = (_silu(_dot(x, wgb_ref[...])) * _dot(x, wub_ref[...])).astype(BF16)
        _store_slabs(o_ref, _pack_pairs(_dot(hb, wdb_ref[...])), rows, nc)

    @pl.when(jnp.logical_and(used, jnp.logical_not(short)))
    def _():
        ffn(EXPERT_BLOCK)

    @pl.when(jnp.logical_and(used, short))
    def _():
        ffn(half)
        o_ref[half * nc:, :] = jnp.zeros((half * nc, LANES), o_ref.dtype)

    @pl.when(jnp.logical_not(used))
    def _():
        o_ref[...] = jnp.zeros_like(o_ref)


def _expert_ffn(xs, block_e, n_used, n_valid, w_gate, w_up, w_down, layer, nc):
    _, _, d, f = w_gate.shape
    nb = xs.shape[0] // (EXPERT_BLOCK * nc)
    wspec = lambda a, b: pl.BlockSpec((None, None, a, b), lambda i, be, nu, nv: (layer, be[i], 0, 0))
    grid_spec = pltpu.PrefetchScalarGridSpec(
        num_scalar_prefetch=3,
        grid=(nb,),
        in_specs=[
            pl.BlockSpec((EXPERT_BLOCK * nc, LANES),
                         lambda i, be, nu, nv: (jnp.minimum(i, nu[0] - 1), 0)),
            wspec(d, f), wspec(d, f), wspec(f, d),
        ],
        out_specs=pl.BlockSpec((EXPERT_BLOCK * nc, LANES), lambda i, be, nu, nv: (i, 0)),
        scratch_shapes=[pltpu.VMEM((d, f), BF16), pltpu.VMEM((d, f), BF16), pltpu.VMEM((f, d), BF16)],
    )
    return pl.pallas_call(
        functools.partial(_expert_ffn_kernel, nc=nc),
        grid_spec=grid_spec,
        out_shape=jax.ShapeDtypeStruct(xs.shape, xs.dtype),
        compiler_params=_cparams(1),
        name="expert_ffn",
    )(block_e, n_used, n_valid, xs, w_gate, w_up, w_down)


def _combine_kernel(dest_ref, destn_ref, wts_ref, hf_ref, hb_ref, sg_ref, su_ref, sd_ref, g_ref,
                    b_ref, ys_ref, of_ref, ob_ref, buf_ref, sems, *, tm, nc, alpha):
    i = pl.program_id(0)
    slot = i % 2

    def slab_copy(sl, k, r, src):
        dst_row = pl.multiple_of((k * tm + r) * nc, nc)
        return pltpu.make_async_copy(ys_ref.at[pl.ds(pl.multiple_of(src * nc, nc), nc), :],
                                     buf_ref.at[sl, pl.ds(dst_row, nc), :], sems.at[sl])

    def gather(sl, rows_ref):
        def issue(r, carry):
            for k in range(TOP_K):
                slab_copy(sl, k, r, rows_ref[0, 0, r * TOP_K + k]).start(priority=k % 2)
            return carry

        lax.fori_loop(0, tm, issue, 0)

    @pl.when(i == 0)
    def _():
        gather(slot, dest_ref)

    @pl.when(i + 1 < pl.num_programs(0))
    def _():
        gather(1 - slot, destn_ref)

    xb = hb_ref[...]
    shared = _dot((_silu(_dot(xb, sg_ref[...])) * _dot(xb, su_ref[...])).astype(BF16), sd_ref[...])
    z = alpha * hf_ref[...] + shared

    pltpu.make_async_copy(buf_ref.at[slot], buf_ref.at[slot], sems.at[slot]).wait()
    wts = wts_ref[...]
    his = [None] * nc
    los = [None] * nc
    for k in range(TOP_K):
        wk = wts[:, k:k + 1]
        for s in range(nc):
            hi, lo = _unpack_pairs(buf_ref[slot, pl.ds(k * tm * nc + s, tm, stride=nc), :])
            his[s] = wk * hi if k == 0 else his[s] + wk * hi
            los[s] = wk * lo if k == 0 else los[s] + wk * lo
    out = _layer_norm(z + jnp.concatenate(his + los, axis=1), g_ref[...], b_ref[...])
    of_ref[...] = out
    ob_ref[...] = out.astype(BF16)


def _combine(ys, dest, wts, h_f32, h_bf16, sh_gate, sh_up, sh_down, g, b, alpha, nc, tm=256):
    t, d = h_f32.shape
    f = sh_gate.shape[1]
    tm = min(tm, t)
    n = t // tm
    smem = lambda im: pl.BlockSpec((1, 1, tm * TOP_K), im, memory_space=pltpu.SMEM)
    tile = lambda w: pl.BlockSpec((tm, w), lambda i: (i, 0))
    const = lambda shape: pl.BlockSpec(shape, lambda i: (0, 0))
    rows = dest.reshape(n, 1, tm * TOP_K)
    return pl.pallas_call(
        functools.partial(_combine_kernel, tm=tm, nc=nc, alpha=alpha),
        grid=(n,),
        in_specs=[smem(lambda i: (i, 0, 0)), smem(lambda i: (jnp.minimum(i + 1, n - 1), 0, 0)),
                  tile(LANES), tile(d), tile(d),
                  const((d, f)), const((d, f)), const((f, d)), const((1, d)), const((1, d)),
                  pl.BlockSpec(memory_space=pl.ANY)],
        out_specs=[tile(d), tile(d)],
        out_shape=[jax.ShapeDtypeStruct((t, d), F32), jax.ShapeDtypeStruct((t, d), BF16)],
        scratch_shapes=[pltpu.VMEM((2, TOP_K * tm * nc, LANES), ys.dtype),
                        pltpu.SemaphoreType.DMA((2,))],
        compiler_params=_cparams(1),
        name="moe_combine",
    )(rows, rows, wts, h_f32, h_bf16, sh_gate, sh_up, sh_down, g.reshape(1, d), b.reshape(1, d), ys)


def _moe_ln(h_f32, h_bf16, h_slabs, router_w, router_bias, w_gate, w_up, w_down, layer, sh_gate,
            sh_up, sh_down, g, b, alpha):
    t, d = h_f32.shape
    n_exp = router_w.shape[1]
    nc = _slab_rows(d)
    idx, wts, rank, counts = _router(h_f32, router_w, router_bias)
    idx, rank = idx[:, :TOP_K], rank[:, :TOP_K]

    counts = counts[0, :n_exp].astype(jnp.int32)
    padded = (counts + EXPERT_BLOCK - 1) // EXPERT_BLOCK * EXPERT_BLOCK
    pends = jnp.cumsum(padded)
    pstart = jnp.concatenate([jnp.zeros((1,), jnp.int32), pends]).astype(jnp.int32)
    n_blocks = t * TOP_K // EXPERT_BLOCK + n_exp
    block_start = jnp.arange(n_blocks, dtype=jnp.int32) * EXPERT_BLOCK
    block_e = jnp.sum((pends[None, :] <= block_start[:, None]).astype(jnp.int32), axis=1)
    block_e = jnp.minimum(block_e, n_exp - 1)
    n_used = (pends[-1:] // EXPERT_BLOCK).astype(jnp.int32)
    n_valid = jnp.clip(counts[block_e] - (block_start - pstart[block_e]), 0, EXPERT_BLOCK)
    group_start = jnp.sum(jnp.where(idx[:, :, None] == jnp.arange(n_exp, dtype=jnp.int32),
                                    pstart[:n_exp], 0), axis=-1)
    dest = group_start + rank

    xs = _dispatch(h_slabs, dest, pstart, counts, n_blocks * EXPERT_BLOCK, nc)
    ys = _expert_ffn(xs, block_e, n_used, n_valid, w_gate, w_up, w_down, layer, nc)
    return _combine(ys, dest, wts, h_f32, h_bf16, sh_gate.astype(BF16), sh_up.astype(BF16),
                    sh_down.astype(BF16), g, b, alpha, nc)


def kernel(x, even_w_in, ret_gn_g, rg_conv_w, rg_conv_b, rg_w_a, rg_b_a, rg_w_i, rg_b_i, rg_lambda,
           even_w_out, fox_w_in, fox_b_f, fox_w_out, ln_g, ln_b, router_w, router_bias, exp_w_gate,
           exp_w_up, exp_w_down, sh_w_gate, sh_w_up, sh_w_down):
    batch, seq, d = x.shape
    depth = ln_g.shape[0]
    alpha = (2 * depth) ** 0.25
    ret_width = ret_gn_g.shape[1]
    fox_heads = fox_b_f.shape[1]

    h_f32 = x.reshape(batch * seq, d)
    h_bf16 = h_f32.astype(BF16)
    for layer in range(depth):
        j = layer // 2
        if layer % 2 == 0:
            proj = _matmul(h_bf16, even_w_in[j].astype(BF16), BF16)
            y_ret = _retention(proj, ret_gn_g[j], batch, seq, chunk=256)
            y_rnn = _rglru(proj, rg_conv_w[j], rg_conv_b[j], rg_w_a[j], rg_b_a[j], rg_w_i[j],
                           rg_b_i[j], rg_lambda[j], batch, seq, tt=256)
            w_out = even_w_out[j].astype(BF16)
            ys, ws = [y_ret, y_rnn], [w_out[:ret_width], w_out[ret_width:]]
        else:
            w_in = fox_w_in[j]
            qkv = _matmul(h_bf16, w_in[:, :3 * d].astype(BF16), BF16)
            w_f = jnp.pad(w_in[:, 3 * d:], ((0, 0), (0, LANES - fox_heads))).astype(BF16)
            fl = _matmul(h_bf16, w_f, F32)
            b_f = jnp.pad(fox_b_f[j], (0, LANES - fox_heads)).reshape(1, LANES)
            kc = _forget_key_bias(_logf_cumsum(fl, b_f, batch, seq), batch, seq, fox_heads)
            o = _fox_attention(qkv, kc, batch, seq, fox_heads, tq=512)
            ys, ws = [o], [fox_w_out[j].astype(BF16)]
        h_f32, h_bf16, h_slabs = _proj_ln(ys, ws, h_f32, ln_g[layer, 0], ln_b[layer, 0], alpha)
        h_f32, h_bf16 = _moe_ln(h_f32, h_bf16, h_slabs, router_w[layer], router_bias[layer],
                                exp_w_gate, exp_w_up, exp_w_down, layer,
                                sh_w_gate[layer], sh_w_up[layer], sh_w_down[layer],
                                ln_g[layer, 1], ln_b[layer, 1], alpha)
    return h_f32.reshape(batch, seq, d)
```

```python
import functools
import math

import numpy as np
import jax
import jax.numpy as jnp
from jax import lax
from jax.experimental import pallas as pl
from jax.experimental.pallas import tpu as pltpu

F32 = jnp.float32
BF16 = jnp.bfloat16

HEAD_DIM = 128
TOP_K = 8
ROUTED_SCALE = 2.5
ROPE_BASE = 10000.0
RG_C = 8.0
LN_EPS = 1e-5
CONV_TAIL = 8

V7X_VMEM_BYTES = 64 * 1024 * 1024
VMEM_LIMIT = V7X_VMEM_BYTES - 8 * 1024 * 1024
LANES = 128
SUBLANES = 8

EXPERT_BLOCK = 512


def _cparams(grid_rank):
    return pltpu.CompilerParams(dimension_semantics=("arbitrary",) * grid_rank,
                                vmem_limit_bytes=VMEM_LIMIT)


def _dot(a, b):
    return jnp.dot(a, b, preferred_element_type=F32)


def _silu(x):
    return x * jax.nn.sigmoid(x)


def _softplus(x):
    return jnp.maximum(x, 0.0) + jnp.log1p(jnp.exp(-jnp.abs(x)))


def _layer_norm(z, g, b):
    mu = jnp.mean(z, axis=-1, keepdims=True)
    zc = z - mu
    var = jnp.mean(zc * zc, axis=-1, keepdims=True)
    return zc * lax.rsqrt(var + LN_EPS) * g + b


def _mm_kernel(x_ref, w_ref, o_ref):
    o_ref[...] = _dot(x_ref[...], w_ref[...]).astype(o_ref.dtype)


def _tile(n, target):
    best = LANES
    for c in range(LANES, min(n, target) + 1, LANES):
        if n % c == 0:
            best = c
    assert n % best == 0
    return best


def _matmul(x, w, out_dtype, tm=1024, tn=1024):
    m, k = x.shape
    n = w.shape[1]
    tm, tn = _tile(m, tm), _tile(n, tn)
    return pl.pallas_call(
        _mm_kernel,
        grid=(m // tm, n // tn),
        in_specs=[pl.BlockSpec((tm, k), lambda i, j: (i, 0)),
                  pl.BlockSpec((k, tn), lambda i, j: (0, j))],
        out_specs=pl.BlockSpec((tm, tn), lambda i, j: (i, j)),
        out_shape=jax.ShapeDtypeStruct((m, n), out_dtype),
        compiler_params=_cparams(2),
        name="in_proj",
    )(x, w)


def _slab_rows(d):
    assert d % (2 * LANES) == 0
    return d // (2 * LANES)


def _pack_pairs(x):
    half = x.shape[1] // 2
    hi = lax.bitcast_convert_type(x[:, :half].astype(BF16).astype(F32), jnp.uint32)
    lo = lax.bitcast_convert_type(x[:, half:].astype(BF16).astype(F32), jnp.uint32)
    return hi | (lo >> 16)


def _unpack_pairs(w):
    hi = lax.bitcast_convert_type(w & jnp.uint32(0xFFFF0000), F32)
    lo = lax.bitcast_convert_type(w << 16, F32)
    return hi, lo


def _store_slabs(ref, words, rows, nc):
    for s in range(nc):
        ref[pl.ds(s, rows, stride=nc), :] = words[:, s * LANES:(s + 1) * LANES]


def _proj_ln_kernel(*refs, n_in, alpha, tm, nc):
    ys, ws = refs[:n_in], refs[n_in:2 * n_in]
    h_ref, g_ref, b_ref, of_ref, ob_ref, op_ref = refs[2 * n_in:]
    acc = _dot(ys[0][...], ws[0][...])
    for y_ref, w_ref in zip(ys[1:], ws[1:]):
        acc = acc + _dot(y_ref[...], w_ref[...])
    out = _layer_norm(alpha * h_ref[...] + acc, g_ref[...], b_ref[...])
    of_ref[...] = out
    ob_ref[...] = out.astype(BF16)
    _store_slabs(op_ref, _pack_pairs(out), tm, nc)


def _proj_ln(ys, ws, h, g, b, alpha, tm=512):
    t, d = h.shape
    tm = min(tm, t)
    n_in = len(ys)
    nc = _slab_rows(d)
    in_specs = ([pl.BlockSpec((tm, y.shape[1]), lambda i: (i, 0)) for y in ys]
                + [pl.BlockSpec(w.shape, lambda i: (0, 0), pipeline_mode=pl.Buffered(1)) for w in ws]
                + [pl.BlockSpec((tm, d), lambda i: (i, 0)),
                   pl.BlockSpec((1, d), lambda i: (0, 0)),
                   pl.BlockSpec((1, d), lambda i: (0, 0))])
    return pl.pallas_call(
        functools.partial(_proj_ln_kernel, n_in=n_in, alpha=alpha, tm=tm, nc=nc),
        grid=(t // tm,),
        in_specs=in_specs,
        out_specs=[pl.BlockSpec((tm, d), lambda i: (i, 0)),
                   pl.BlockSpec((tm, d), lambda i: (i, 0)),
                   pl.BlockSpec((tm * nc, LANES), lambda i: (i, 0))],
        out_shape=[jax.ShapeDtypeStruct((t, d), F32), jax.ShapeDtypeStruct((t, d), BF16),
                   jax.ShapeDtypeStruct((t * nc, LANES), jnp.uint32)],
        compiler_params=_cparams(1),
        name="out_proj_ln",
    )(*ys, *ws, h, g.reshape(1, d), b.reshape(1, d))


def _retention_kernel(q_ref, k_ref, v_ref, g_ref, cos_ref, sin_ref, intra_ref, qdec_ref, kdec_ref,
                      gn_ref, o_ref, state_ref, *, heads, chunk_decay):
    @pl.when(pl.program_id(1) == 0)
    def _():
        state_ref[...] = jnp.zeros_like(state_ref)

    cos, sin = cos_ref[...], sin_ref[...]
    half = HEAD_DIM // 2
    for h in range(heads):
        sl = slice(h * HEAD_DIM, (h + 1) * HEAD_DIM)
        q = q_ref[:, sl].astype(F32)
        k = k_ref[:, sl].astype(F32)
        q = q * cos + pltpu.roll(q, half, 1) * sin
        k = (k * cos + pltpu.roll(k, half, 1) * sin) * (HEAD_DIM ** -0.5)
        v = v_ref[:, sl]
        s = lax.dot_general(q.astype(BF16), k.astype(BF16), (((1,), (1,)), ((), ())),
                            preferred_element_type=F32) * intra_ref[h]
        state = state_ref[h]
        y = _dot(s.astype(BF16), v) + _dot((q * qdec_ref[h]).astype(BF16), state.astype(BF16))
        kv = lax.dot_general((k * kdec_ref[h]).astype(BF16), v, (((0,), (0,)), ((), ())),
                             preferred_element_type=F32)
        state_ref[h] = state * chunk_decay[h] + kv
        mu = jnp.mean(y, axis=-1, keepdims=True)
        yc = y - mu
        var = jnp.mean(yc * yc, axis=-1, keepdims=True)
        yn = yc * lax.rsqrt(var + LN_EPS) * gn_ref[:, sl]
        o_ref[:, sl] = (yn * _silu(g_ref[:, sl].astype(F32))).astype(BF16)


def _retention(proj, gn_g, batch, seq, chunk):
    t = proj.shape[0]
    width = gn_g.shape[0]
    heads = width // HEAD_DIM
    chunk = min(chunk, seq)
    n_chunks = seq // chunk
    half = HEAD_DIM // 2

    pos = jnp.arange(seq, dtype=F32)
    inv = ROPE_BASE ** (-jnp.arange(half, dtype=F32) / half)
    ang = pos[:, None] * inv[None, :]
    cos2 = jnp.concatenate([jnp.cos(ang), jnp.cos(ang)], axis=1)
    sin2 = jnp.concatenate([-jnp.sin(ang), jnp.sin(ang)], axis=1)

    log_g = jnp.log1p(-jnp.exp2(-5.0 - jnp.arange(heads, dtype=F32)))
    i = jnp.arange(chunk, dtype=F32)
    rel = i[:, None] - i[None, :]
    intra = jnp.where(rel >= 0, jnp.exp(log_g[:, None, None] * jnp.maximum(rel, 0.0)), 0.0)
    kdec = jnp.exp(log_g[:, None] * (chunk - 1.0 - i)[None, :])
    qdec = jnp.exp(log_g[:, None] * (i + 1.0)[None, :])
    kdec = jnp.broadcast_to(kdec[:, :, None], (heads, chunk, HEAD_DIM))
    qdec = jnp.broadcast_to(qdec[:, :, None], (heads, chunk, HEAD_DIM))
    chunk_decay = tuple(math.exp(math.log1p(-2.0 ** (-5.0 - h)) * chunk) for h in range(heads))

    def col(c):
        return pl.BlockSpec((chunk, width), lambda b, n: (b * n_chunks + n, c))

    const3 = lambda shape: pl.BlockSpec(shape, lambda b, n: (0, 0, 0))
    return pl.pallas_call(
        functools.partial(_retention_kernel, heads=heads, chunk_decay=chunk_decay),
        grid=(batch, n_chunks),
        in_specs=[col(0), col(1), col(2), col(3),
                  pl.BlockSpec((chunk, HEAD_DIM), lambda b, n: (n, 0)),
                  pl.BlockSpec((chunk, HEAD_DIM), lambda b, n: (n, 0)),
                  const3((heads, chunk, chunk)),
                  const3((heads, chunk, HEAD_DIM)),
                  const3((heads, chunk, HEAD_DIM)),
                  pl.BlockSpec((1, width), lambda b, n: (0, 0))],
        out_specs=pl.BlockSpec((chunk, width), lambda b, n: (b * n_chunks + n, 0)),
        out_shape=jax.ShapeDtypeStruct((t, width), BF16),
        scratch_shapes=[pltpu.VMEM((heads, HEAD_DIM, HEAD_DIM), F32)],
        compiler_params=_cparams(2),
        name="retention",
    )(proj, proj, proj, proj, cos2, sin2, intra, qdec, kdec, gn_g.reshape(1, width))


def _rglru_kernel(u_ref, gate_ref, cw_ref, cb_ref, wa_ref, ba_ref, wi_ref, bi_ref, lam_ref,
                  o_ref, uext_ref, hc_ref, *, tt, nblk):
    n = pl.program_id(1)

    @pl.when(n == 0)
    def _():
        uext_ref[0:CONV_TAIL, :] = jnp.zeros((CONV_TAIL, uext_ref.shape[1]), F32)
        hc_ref[...] = jnp.zeros_like(hc_ref)

    @pl.when(n > 0)
    def _():
        uext_ref[0:CONV_TAIL, :] = uext_ref[tt:tt + CONV_TAIL, :]

    u = u_ref[...].astype(F32)
    uext_ref[CONV_TAIL:CONV_TAIL + tt, :] = u
    uc = (cb_ref[...] + cw_ref[3:4, :] * u
          + cw_ref[2:3, :] * uext_ref[CONV_TAIL - 1:CONV_TAIL - 1 + tt, :]
          + cw_ref[1:2, :] * uext_ref[CONV_TAIL - 2:CONV_TAIL - 2 + tt, :]
          + cw_ref[0:1, :] * uext_ref[CONV_TAIL - 3:CONV_TAIL - 3 + tt, :])
    ucb = uc.astype(BF16)
    ra = jnp.concatenate(
        [_dot(ucb[:, i * HEAD_DIM:(i + 1) * HEAD_DIM], wa_ref[i]) for i in range(nblk)], axis=1)
    ia = jnp.concatenate(
        [_dot(ucb[:, i * HEAD_DIM:(i + 1) * HEAD_DIM], wi_ref[i]) for i in range(nblk)], axis=1)
    r = jax.nn.sigmoid(ra + ba_ref[...])
    ig = jax.nn.sigmoid(ia + bi_ref[...])
    log_a = (-RG_C) * r * _softplus(-lam_ref[...])
    a = jnp.exp(log_a)
    bt = jnp.sqrt(-jnp.tanh(log_a) * (a * a + 1.0)) * (ig * uc)

    row = lax.broadcasted_iota(jnp.int32, a.shape, 0) & (SUBLANES - 1)
    d = 1
    while d < SUBLANES:
        keep = row >= d
        bt = jnp.where(keep, a * pltpu.roll(bt, d, 0) + bt, bt)
        a = jnp.where(keep, a * pltpu.roll(a, d, 0), a)
        d *= 2
    carry = hc_ref[...]
    groups = []
    for g in range(tt // SUBLANES):
        rows = slice(g * SUBLANES, (g + 1) * SUBLANES)
        hg = a[rows] * carry + bt[rows]
        groups.append(hg)
        carry = hg[SUBLANES - 1:SUBLANES, :]
    h = jnp.concatenate(groups, axis=0)
    hc_ref[...] = carry

    g = gate_ref[...].astype(F32)
    gelu = 0.5 * g * (1.0 + jnp.tanh(math.sqrt(2.0 / math.pi) * (g + 0.044715 * (g * g * g))))
    o_ref[...] = (h * gelu).astype(BF16)


def _rglru(proj, conv_w, conv_b, w_a, b_a, w_i, b_i, lam, batch, seq, tt):
    t = proj.shape[0]
    width = conv_w.shape[1]
    nblk = w_a.shape[0]
    tt = min(tt, seq)
    n_t = seq // tt
    row = lambda x: x.reshape(1, width)
    vec = pl.BlockSpec((1, width), lambda b, n: (0, 0))
    blk = pl.BlockSpec((nblk, HEAD_DIM, HEAD_DIM), lambda b, n: (0, 0, 0))
    return pl.pallas_call(
        functools.partial(_rglru_kernel, tt=tt, nblk=nblk),
        grid=(batch, n_t),
        in_specs=[pl.BlockSpec((tt, width), lambda b, n: (b * n_t + n, 4)),
                  pl.BlockSpec((tt, width), lambda b, n: (b * n_t + n, 5)),
                  pl.BlockSpec((4, width), lambda b, n: (0, 0)),
                  vec, blk, vec, blk, vec, vec],
        out_specs=pl.BlockSpec((tt, width), lambda b, n: (b * n_t + n, 0)),
        out_shape=jax.ShapeDtypeStruct((t, width), BF16),
        scratch_shapes=[pltpu.VMEM((tt + CONV_TAIL, width), F32), pltpu.VMEM((1, width), F32)],
        compiler_params=_cparams(2),
        name="rglru",
    )(proj, proj, conv_w, row(conv_b), w_a.astype(BF16), row(b_a), w_i.astype(BF16), row(b_i),
      row(lam))


FORGET_SPLIT = 3


def _bf16_head(x):
    bits = lax.bitcast_convert_type(x, jnp.uint32) & jnp.uint32(0xFFFF0000)
    return lax.bitcast_convert_type(bits, F32)


def _logf_cumsum_kernel(fl_ref, bf_ref, *o_refs, seq):
    z = fl_ref[...] + bf_ref[...]
    c = jnp.minimum(z, 0.0) - jnp.log1p(jnp.exp(-jnp.abs(z)))
    row = lax.broadcasted_iota(jnp.int32, c.shape, 0)
    d = 1
    while d < seq:
        c = c + jnp.where(row >= d, pltpu.roll(c, d, 0), 0.0)
        d *= 2
    x = c * (-(HEAD_DIM ** 0.5))
    for o_ref in o_refs:
        piece = _bf16_head(x)
        o_ref[...] = piece.astype(BF16)
        x = x - piece


def _logf_cumsum(fl, b_f, batch, seq):
    tile = pl.BlockSpec((seq, LANES), lambda b: (b, 0))
    return pl.pallas_call(
        functools.partial(_logf_cumsum_kernel, seq=seq),
        grid=(batch,),
        in_specs=[tile, pl.BlockSpec((1, LANES), lambda b: (0, 0))],
        out_specs=[tile] * FORGET_SPLIT,
        out_shape=[jax.ShapeDtypeStruct(fl.shape, BF16)] * FORGET_SPLIT,
        compiler_params=_cparams(1),
        name="logf_cumsum",
    )(fl, b_f)


def _fox_kernel(q_ref, k_ref, v_ref, kc_ref, o_ref, kext_ref, vext_ref, m_ref, l_ref, acc_ref, *,
                tq, seq, group):
    qi = pl.program_id(2)
    head = lambda g: slice(g * HEAD_DIM, (g + 1) * HEAD_DIM)

    @pl.when(qi == 0)
    def _():
        for g in range(group):
            kext_ref[g, :, :HEAD_DIM] = k_ref[:, head(g)]
            kext_ref[g, :, HEAD_DIM:] = kc_ref[g]
            vext_ref[g, :, :HEAD_DIM] = v_ref[:, head(g)]
            vext_ref[g, :, HEAD_DIM:] = jnp.ones((seq, HEAD_DIM), BF16)

    lane = lax.broadcasted_iota(jnp.int32, (tq, HEAD_DIM), 1)
    ones = jnp.where(lane < FORGET_SPLIT, 1.0, 0.0).astype(BF16)
    q2 = [jnp.concatenate([q_ref[:, head(g)], ones], axis=1) for g in range(group)]
    c = (HEAD_DIM ** -0.5) * math.log2(math.e)
    reps = tq // HEAD_DIM

    def logits(g, kb):
        off = pl.multiple_of(kb * tq, tq)
        return lax.dot_general(q2[g], kext_ref[g, pl.ds(off, tq), :], (((1,), (1,)), ((), ())),
                               preferred_element_type=F32)

    def fold(g, kb, masked):
        u = logits(g, kb)
        if masked:
            row = lax.broadcasted_iota(jnp.int32, u.shape, 0)
            col = lax.broadcasted_iota(jnp.int32, u.shape, 1)
            u = jnp.where(col <= row, u, -jnp.inf)
        m = m_ref[g]
        m_new = jnp.maximum(m, jnp.broadcast_to(jnp.max(u, axis=-1, keepdims=True), m.shape))
        p = jnp.exp2((u - jnp.concatenate([m_new] * reps, axis=1)) * c)
        alpha = jnp.exp2((m - m_new) * c)
        off = pl.multiple_of(kb * tq, tq)
        pv = _dot(p.astype(BF16), vext_ref[g, pl.ds(off, tq), :])
        m_ref[g] = m_new
        l_ref[g] = alpha * l_ref[g] + pv[:, HEAD_DIM:]
        acc_ref[g] = alpha * acc_ref[g] + pv[:, :HEAD_DIM]

    m_ref[...] = jnp.full(m_ref.shape, -jnp.inf, F32)
    l_ref[...] = jnp.zeros(l_ref.shape, F32)
    acc_ref[...] = jnp.zeros(acc_ref.shape, F32)

    def body(kb, carry):
        for g in range(group):
            fold(g, kb, False)
        return carry

    lax.fori_loop(0, qi, body, 0)
    for g in range(group):
        fold(g, qi, True)
        o_ref[:, head(g)] = (acc_ref[g] / l_ref[g]).astype(BF16)


def _fox_attention(qkv, kc, batch, seq, heads, tq, group=4):
    t = qkv.shape[0]
    tq = min(tq, max(HEAD_DIM, seq // 4))
    nq = seq // tq
    assert tq % HEAD_DIM == 0 and seq % tq == 0 and heads % group == 0
    hg = heads // group
    w = group * HEAD_DIM
    return pl.pallas_call(
        functools.partial(_fox_kernel, tq=tq, seq=seq, group=group),
        grid=(batch, hg, nq),
        in_specs=[
            pl.BlockSpec((tq, w), lambda b, h, qi: (b * nq + qi, h)),
            pl.BlockSpec((seq, w), lambda b, h, qi: (b, hg + h)),
            pl.BlockSpec((seq, w), lambda b, h, qi: (b, 2 * hg + h)),
            pl.BlockSpec((group, seq, HEAD_DIM), lambda b, h, qi: (b * hg + h, 0, 0)),
        ],
        out_specs=pl.BlockSpec((tq, w), lambda b, h, qi: (b * nq + qi, h)),
        out_shape=jax.ShapeDtypeStruct((t, heads * HEAD_DIM), BF16),
        scratch_shapes=[pltpu.VMEM((group, seq, 2 * HEAD_DIM), BF16),
                        pltpu.VMEM((group, seq, 2 * HEAD_DIM), BF16)]
        + [pltpu.VMEM((group, tq, HEAD_DIM), F32)] * 3,
        compiler_params=_cparams(3),
        name="fox_attention",
    )(qkv, qkv, qkv, kc)


def _forget_key_bias(pieces, batch, seq, heads):
    kc = jnp.stack([p.reshape(batch, seq, LANES)[:, :, :heads] for p in pieces], axis=-1)
    kc = jnp.transpose(kc, (0, 2, 1, 3)).reshape(batch * heads, seq, FORGET_SPLIT)
    return jnp.pad(kc, ((0, 0), (0, 0), (0, HEAD_DIM - FORGET_SPLIT)))


def _router_kernel(h_ref, rw_ref, rb_ref, idx_ref, wts_ref, rank_ref, cnt_ref, *, tm):
    @pl.when(pl.program_id(0) == 0)
    def _():
        cnt_ref[...] = jnp.zeros_like(cnt_ref)

    h = h_ref[...]
    h_hi = _bf16_head(h)
    first_order = _dot(h_hi.astype(BF16), rw_ref[...])
    logits = (first_order[:, :LANES] + first_order[:, LANES:]
              + _dot((h - h_hi).astype(BF16), rw_ref[:, :LANES]))
    scores = jax.nn.sigmoid(logits)
    sel = scores + rb_ref[...]
    lane = lax.broadcasted_iota(jnp.int32, sel.shape, 1)
    lane_f = lane.astype(F32)
    idx = jnp.zeros(sel.shape, F32)
    wts = jnp.zeros(sel.shape, F32)
    chosen = jnp.zeros(sel.shape, F32)
    hits = []
    for k in range(TOP_K):
        top = jnp.max(sel, axis=-1, keepdims=True)
        first = jnp.min(jnp.where(sel == top, lane_f, float(LANES)), axis=-1, keepdims=True)
        hit = lane_f == first
        hits.append(hit)
        idx = jnp.where(lane == k, first, idx)
        wts = jnp.where(lane == k, jnp.sum(jnp.where(hit, scores, 0.0), axis=-1, keepdims=True), wts)
        chosen = jnp.where(hit, 1.0, chosen)
        sel = jnp.where(hit, -jnp.inf, sel)
    wts = wts / jnp.sum(wts, axis=-1, keepdims=True) * ROUTED_SCALE

    r = lax.broadcasted_iota(jnp.int32, (tm, tm), 0)
    c = lax.broadcasted_iota(jnp.int32, (tm, tm), 1)
    before = jnp.where(c < r, 1.0, 0.0).astype(BF16)
    rank_all = _dot(before, chosen.astype(BF16)) + cnt_ref[...]
    rank = jnp.zeros(sel.shape, F32)
    for k in range(TOP_K):
        rank = jnp.where(lane == k, jnp.sum(jnp.where(hits[k], rank_all, 0.0), axis=-1, keepdims=True),
                         rank)
    cnt_ref[...] = cnt_ref[...] + jnp.sum(chosen, axis=0, keepdims=True)
    idx_ref[...] = idx.astype(jnp.int32)
    wts_ref[...] = wts
    rank_ref[...] = rank.astype(jnp.int32)


def _router(h, router_w, router_bias, tm=256):
    t, d = h.shape
    e = router_w.shape[1]
    tm = min(tm, t)
    rw = jnp.pad(router_w, ((0, 0), (0, LANES - e)))
    rw_hi = _bf16_head(rw)
    rw = jnp.concatenate([rw_hi.astype(BF16), (rw - rw_hi).astype(BF16)], axis=1)
    rb = jnp.pad(router_bias.astype(F32), (0, LANES - e), constant_values=-jnp.inf).reshape(1, LANES)
    tile = pl.BlockSpec((tm, LANES), lambda i: (i, 0))
    return pl.pallas_call(
        functools.partial(_router_kernel, tm=tm),
        grid=(t // tm,),
        in_specs=[pl.BlockSpec((tm, d), lambda i: (i, 0)),
                  pl.BlockSpec((d, 2 * LANES), lambda i: (0, 0)),
                  pl.BlockSpec((1, LANES), lambda i: (0, 0))],
        out_specs=[tile, tile, tile, pl.BlockSpec((1, LANES), lambda i: (0, 0))],
        out_shape=[jax.ShapeDtypeStruct((t, LANES), jnp.int32),
                   jax.ShapeDtypeStruct((t, LANES), F32),
                   jax.ShapeDtypeStruct((t, LANES), jnp.int32),
                   jax.ShapeDtypeStruct((1, LANES), F32)],
        compiler_params=_cparams(1),
        name="router_topk",
    )(h, rw, rb)


def _dispatch_kernel(pstart_ref, cnt_ref, dest_ref, x_ref, xs_ref, zero_ref, sem, *, tm, n_exp, nc,
                     n_blocks):
    i = pl.program_id(0)
    blk = EXPERT_BLOCK * nc

    def row_copy(r, dst):
        return pltpu.make_async_copy(x_ref.at[pl.ds(pl.multiple_of(r * nc, nc), nc), :],
                                     xs_ref.at[pl.ds(pl.multiple_of(dst * nc, nc), nc), :], sem)

    def zero_run(dst, n):
        return pltpu.make_async_copy(zero_ref.at[pl.ds(0, n * nc), :],
                                     xs_ref.at[pl.ds(pl.multiple_of(dst * nc, nc), n * nc), :], sem)

    def zero_block(b):
        return pltpu.make_async_copy(zero_ref, xs_ref.at[pl.ds(pl.multiple_of(b * blk, blk), blk), :],
                                     sem)

    def pad_rows(e, do):
        first = pstart_ref[e] + cnt_ref[e]
        n = pstart_ref[e + 1] - first
        bit = EXPERT_BLOCK // 2
        while bit:
            pl.when((n & bit) != 0)(functools.partial(do, first, bit))
            first = first + (n & bit)
            bit //= 2

    def first_unused_block():
        return lax.div(pstart_ref[n_exp], EXPERT_BLOCK)

    @pl.when(i == 0)
    def _():
        zero_ref[...] = jnp.zeros_like(zero_ref)

        def per_expert(e, carry):
            pad_rows(e, lambda first, n: zero_run(first, n).start())
            return carry

        lax.fori_loop(0, n_exp, per_expert, 0)
        lax.fori_loop(first_unused_block(), n_blocks, lambda b, c: (zero_block(b).start(), c)[1], 0)

    def issue(r, carry):
        for k in range(TOP_K):
            row_copy(r, dest_ref[0, 0, r * TOP_K + k]).start(priority=k % 2)
        return carry

    lax.fori_loop(0, tm, issue, 0)
    all_rows = xs_ref.at[pl.ds(0, tm * TOP_K * nc), :]
    pltpu.make_async_copy(all_rows, all_rows, sem).wait()

    @pl.when(i == 0)
    def _():
        def per_expert(e, carry):
            pad_rows(e, lambda first, n: zero_run(0, n).wait())
            return carry

        lax.fori_loop(0, n_exp, per_expert, 0)
        lax.fori_loop(first_unused_block(), n_blocks, lambda b, c: (zero_block(0).wait(), c)[1], 0)


def _dispatch(xp, dest, pstart, counts, n_rows, nc, tm=256):
    t = xp.shape[0] // nc
    tm = min(tm, t)
    n_exp = counts.shape[0]
    assert n_rows >= tm * TOP_K
    grid_spec = pltpu.PrefetchScalarGridSpec(
        num_scalar_prefetch=2,
        grid=(t // tm,),
        in_specs=[pl.BlockSpec((1, 1, tm * TOP_K), lambda i, ps, cn: (i, 0, 0),
                               memory_space=pltpu.SMEM),
                  pl.BlockSpec((tm * nc, LANES), lambda i, ps, cn: (i, 0))],
        out_specs=pl.BlockSpec(memory_space=pl.ANY),
        scratch_shapes=[pltpu.VMEM((EXPERT_BLOCK * nc, LANES), xp.dtype),
                        pltpu.SemaphoreType.DMA(())],
    )
    assert n_rows % EXPERT_BLOCK == 0
    return pl.pallas_call(
        functools.partial(_dispatch_kernel, tm=tm, n_exp=n_exp, nc=nc,
                          n_blocks=n_rows // EXPERT_BLOCK),
        grid_spec=grid_spec,
        out_shape=jax.ShapeDtypeStruct((n_rows * nc, LANES), xp.dtype),
        compiler_params=_cparams(1),
        name="moe_dispatch",
    )(pstart, counts, dest.reshape(t // tm, 1, tm * TOP_K), xp)


def _expert_ffn_kernel(be_ref, nb_ref, ord_ref, nxt_ref, x_ref, wg_hbm, wu_hbm, wd_hbm, o_ref,
                       wgf_ref, wuf_ref, wdf_ref, wgb_ref, wub_ref, wdb_ref, sems, *, nc, layer):
    i = pl.program_id(0)
    used = i < nb_ref[0]
    expert = be_ref[i]
    new_expert = jnp.logical_or(i == 0, expert != be_ref[jnp.maximum(i - 1, 0)])
    slot = ord_ref[i] % 2

    def weight_copies(e, sl):
        return [pltpu.make_async_copy(src.at[layer, e], dst.at[sl], sems.at[sl])
                for src, dst in ((wg_hbm, wgf_ref), (wu_hbm, wuf_ref), (wd_hbm, wdf_ref))]

    @pl.when(i == 0)
    def _():
        for cp in weight_copies(expert, slot):
            cp.start()

    @pl.when(jnp.logical_and(used, new_expert))
    def _():
        for cp in weight_copies(expert, slot):
            cp.wait()
        wgb_ref[...] = wgf_ref[slot].astype(BF16)
        wub_ref[...] = wuf_ref[slot].astype(BF16)
        wdb_ref[...] = wdf_ref[slot].astype(BF16)

        @pl.when(nxt_ref[i] >= 0)
        def _():
            for cp in weight_copies(nxt_ref[i], 1 - slot):
                cp.start()

    @pl.when(used)
    def _():
        xt = jnp.swapaxes(x_ref[...].reshape(EXPERT_BLOCK // nc, nc, nc, LANES), 1, 2)
        halves = [_unpack_pairs(xt[:, s].reshape(EXPERT_BLOCK, LANES)) for s in range(nc)]
        x = jnp.concatenate([h[0].astype(BF16) for h in halves] + [h[1].astype(BF16) for h in halves],
                            axis=1)
        hb = (_silu(_dot(x, wgb_ref[...])) * _dot(x, wub_ref[...])).astype(BF16)
        _store_slabs(o_ref, _pack_pairs(_dot(hb, wdb_ref[...])), EXPERT_BLOCK, nc)

    @pl.when(jnp.logical_not(used))
    def _():
        o_ref[...] = jnp.zeros_like(o_ref)


def _expert_ffn(xs, block_e, n_used, block_ord, block_next, w_gate, w_up, w_down, layer, nc):
    _, _, d, f = w_gate.shape
    nb = xs.shape[0] // (EXPERT_BLOCK * nc)
    any_space = pl.BlockSpec(memory_space=pl.ANY)
    grid_spec = pltpu.PrefetchScalarGridSpec(
        num_scalar_prefetch=4,
        grid=(nb,),
        in_specs=[
            pl.BlockSpec((EXPERT_BLOCK * nc, LANES),
                         lambda i, be, nu, od, nx: (jnp.minimum(i, nu[0] - 1), 0)),
            any_space, any_space, any_space,
        ],
        out_specs=pl.BlockSpec((EXPERT_BLOCK * nc, LANES), lambda i, be, nu, od, nx: (i, 0)),
        scratch_shapes=[pltpu.VMEM((2, d, f), F32), pltpu.VMEM((2, d, f), F32),
                        pltpu.VMEM((2, f, d), F32),
                        pltpu.VMEM((d, f), BF16), pltpu.VMEM((d, f), BF16), pltpu.VMEM((f, d), BF16),
                        pltpu.SemaphoreType.DMA((2,))],
    )
    return pl.pallas_call(
        functools.partial(_expert_ffn_kernel, nc=nc, layer=layer),
        grid_spec=grid_spec,
        out_shape=jax.ShapeDtypeStruct(xs.shape, xs.dtype),
        compiler_params=_cparams(1),
        name="expert_ffn",
    )(block_e, n_used, block_ord, block_next, xs, w_gate, w_up, w_down)


def _combine_kernel(dest_ref, destn_ref, wts_ref, hf_ref, hb_ref, sg_ref, su_ref, sd_ref, g_ref,
                    b_ref, ys_ref, of_ref, ob_ref, buf_ref, sems, *, tm, nc, alpha):
    i = pl.program_id(0)
    slot = i % 2

    def slab_copy(sl, k, r, src):
        dst_row = pl.multiple_of((k * tm + r) * nc, nc)
        return pltpu.make_async_copy(ys_ref.at[pl.ds(pl.multiple_of(src * nc, nc), nc), :],
                                     buf_ref.at[sl, pl.ds(dst_row, nc), :], sems.at[sl])

    def gather(sl, rows_ref):
        def issue(r, carry):
            for k in range(TOP_K):
                slab_copy(sl, k, r, rows_ref[0, 0, r * TOP_K + k]).start(priority=k % 2)
            return carry

        lax.fori_loop(0, tm, issue, 0)

    @pl.when(i == 0)
    def _():
        gather(slot, dest_ref)

    @pl.when(i + 1 < pl.num_programs(0))
    def _():
        gather(1 - slot, destn_ref)

    xb = hb_ref[...]
    shared = _dot((_silu(_dot(xb, sg_ref[...])) * _dot(xb, su_ref[...])).astype(BF16), sd_ref[...])
    z = alpha * hf_ref[...] + shared

    pltpu.make_async_copy(buf_ref.at[slot], buf_ref.at[slot], sems.at[slot]).wait()
    wts = wts_ref[...]
    his = [None] * nc
    los = [None] * nc
    for k in range(TOP_K):
        wk = wts[:, k:k + 1]
        for s in range(nc):
            hi, lo = _unpack_pairs(buf_ref[slot, pl.ds(k * tm * nc + s, tm, stride=nc), :])
            his[s] = wk * hi if k == 0 else his[s] + wk * hi
            los[s] = wk * lo if k == 0 else los[s] + wk * lo
    out = _layer_norm(z + jnp.concatenate(his + los, axis=1), g_ref[...], b_ref[...])
    of_ref[...] = out
    ob_ref[...] = out.astype(BF16)


def _combine(ys, dest, wts, h_f32, h_bf16, sh_gate, sh_up, sh_down, g, b, alpha, nc, tm=256):
    t, d = h_f32.shape
    f = sh_gate.shape[1]
    tm = min(tm, t)
    n = t // tm
    smem = lambda im: pl.BlockSpec((1, 1, tm * TOP_K), im, memory_space=pltpu.SMEM)
    tile = lambda w: pl.BlockSpec((tm, w), lambda i: (i, 0))
    const = lambda shape: pl.BlockSpec(shape, lambda i: (0, 0))
    rows = dest.reshape(n, 1, tm * TOP_K)
    return pl.pallas_call(
        functools.partial(_combine_kernel, tm=tm, nc=nc, alpha=alpha),
        grid=(n,),
        in_specs=[smem(lambda i: (i, 0, 0)), smem(lambda i: (jnp.minimum(i + 1, n - 1), 0, 0)),
                  tile(LANES), tile(d), tile(d),
                  const((d, f)), const((d, f)), const((f, d)), const((1, d)), const((1, d)),
                  pl.BlockSpec(memory_space=pl.ANY)],
        out_specs=[tile(d), tile(d)],
        out_shape=[jax.ShapeDtypeStruct((t, d), F32), jax.ShapeDtypeStruct((t, d), BF16)],
        scratch_shapes=[pltpu.VMEM((2, TOP_K * tm * nc, LANES), ys.dtype),
                        pltpu.SemaphoreType.DMA((2,))],
        compiler_params=_cparams(1),
        name="moe_combine",
    )(rows, rows, wts, h_f32, h_bf16, sh_gate, sh_up, sh_down, g.reshape(1, d), b.reshape(1, d), ys)


def _moe_ln(h_f32, h_bf16, h_slabs, router_w, router_bias, w_gate, w_up, w_down, layer, sh_gate,
            sh_up, sh_down, g, b, alpha):
    t, d = h_f32.shape
    n_exp = router_w.shape[1]
    nc = _slab_rows(d)
    idx, wts, rank, counts = _router(h_f32, router_w, router_bias)
    idx, rank = idx[:, :TOP_K], rank[:, :TOP_K]

    counts = counts[0, :n_exp].astype(jnp.int32)
    padded = (counts + EXPERT_BLOCK - 1) // EXPERT_BLOCK * EXPERT_BLOCK
    pends = jnp.cumsum(padded)
    pstart = jnp.concatenate([jnp.zeros((1,), jnp.int32), pends]).astype(jnp.int32)
    n_blocks = t * TOP_K // EXPERT_BLOCK + n_exp
    block_start = jnp.arange(n_blocks, dtype=jnp.int32) * EXPERT_BLOCK
    block_e = jnp.sum((pends[None, :] <= block_start[:, None]).astype(jnp.int32), axis=1)
    block_e = jnp.minimum(block_e, n_exp - 1)
    n_used = (pends[-1:] // EXPERT_BLOCK).astype(jnp.int32)
    experts = jnp.arange(n_exp, dtype=jnp.int32)
    owns = padded > 0
    order_e = jnp.cumsum(owns.astype(jnp.int32)) - 1
    later = jnp.logical_and(owns[None, :], experts[None, :] > experts[:, None])
    next_e = jnp.min(jnp.where(later, experts[None, :], n_exp), axis=1)
    next_e = jnp.where(next_e < n_exp, next_e, -1)
    of_block = (block_e[:, None] == experts[None, :]).astype(jnp.int32)
    block_ord = jnp.sum(of_block * order_e[None, :], axis=1)
    block_next = jnp.sum(of_block * next_e[None, :], axis=1)
    group_start = jnp.sum(jnp.where(idx[:, :, None] == jnp.arange(n_exp, dtype=jnp.int32),
                                    pstart[:n_exp], 0), axis=-1)
    dest = group_start + rank

    xs = _dispatch(h_slabs, dest, pstart, counts, n_blocks * EXPERT_BLOCK, nc)
    ys = _expert_ffn(xs, block_e, n_used, block_ord, block_next, w_gate, w_up, w_down, layer, nc)
    return _combine(ys, dest, wts, h_f32, h_bf16, sh_gate.astype(BF16), sh_up.astype(BF16),
                    sh_down.astype(BF16), g, b, alpha, nc)


def kernel(x, even_w_in, ret_gn_g, rg_conv_w, rg_conv_b, rg_w_a, rg_b_a, rg_w_i, rg_b_i, rg_lambda,
           even_w_out, fox_w_in, fox_b_f, fox_w_out, ln_g, ln_b, router_w, router_bias, exp_w_gate,
           exp_w_up, exp_w_down, sh_w_gate, sh_w_up, sh_w_down):
    batch, seq, d = x.shape
    depth = ln_g.shape[0]
    alpha = (2 * depth) ** 0.25
    ret_width = ret_gn_g.shape[1]
    fox_heads = fox_b_f.shape[1]

    h_f32 = x.reshape(batch * seq, d)
    h_bf16 = h_f32.astype(BF16)
    for layer in range(depth):
        j = layer // 2
        if layer % 2 == 0:
            proj = _matmul(h_bf16, even_w_in[j].astype(BF16), BF16)
            y_ret = _retention(proj, ret_gn_g[j], batch, seq, chunk=256)
            y_rnn = _rglru(proj, rg_conv_w[j], rg_conv_b[j], rg_w_a[j], rg_b_a[j], rg_w_i[j],
                           rg_b_i[j], rg_lambda[j], batch, seq, tt=256)
            w_out = even_w_out[j].astype(BF16)
            ys, ws = [y_ret, y_rnn], [w_out[:ret_width], w_out[ret_width:]]
        else:
            w_in = fox_w_in[j]
            qkv = _matmul(h_bf16, w_in[:, :3 * d].astype(BF16), BF16)
            w_f = jnp.pad(w_in[:, 3 * d:], ((0, 0), (0, LANES - fox_heads))).astype(BF16)
            fl = _matmul(h_bf16, w_f, F32)
            b_f = jnp.pad(fox_b_f[j], (0, LANES - fox_heads)).reshape(1, LANES)
            kc = _forget_key_bias(_logf_cumsum(fl, b_f, batch, seq), batch, seq, fox_heads)
            o = _fox_attention(qkv, kc, batch, seq, fox_heads, tq=512)
            ys, ws = [o], [fox_w_out[j].astype(BF16)]
        h_f32, h_bf16, h_slabs = _proj_ln(ys, ws, h_f32, ln_g[layer, 0], ln_b[layer, 0], alpha)
        h_f32, h_bf16 = _moe_ln(h_f32, h_bf16, h_slabs, router_w[layer], router_bias[layer],
                                exp_w_gate, exp_w_up, exp_w_down, layer,
                                sh_w_gate[layer], sh_w_up[layer], sh_w_down[layer],
                                ln_g[layer, 1], ln_b[layer, 1], alpha)
    return h_f32.reshape(batch, seq, d)
```

```python
import functools
import math

import numpy as np
import jax
import jax.numpy as jnp
from jax import lax
from jax.experimental import pallas as pl
from jax.experimental.pallas import tpu as pltpu

F32 = jnp.float32
BF16 = jnp.bfloat16

HEAD_DIM = 128
TOP_K = 8
ROUTED_SCALE = 2.5
ROPE_BASE = 10000.0
RG_C = 8.0
LN_EPS = 1e-5
CONV_TAIL = 8

V7X_VMEM_BYTES = 64 * 1024 * 1024
VMEM_LIMIT = V7X_VMEM_BYTES - 8 * 1024 * 1024
LANES = 128
SUBLANES = 8

EXPERT_BLOCK = 512


def _cparams(grid_rank):
    return pltpu.CompilerParams(dimension_semantics=("arbitrary",) * grid_rank,
                                vmem_limit_bytes=VMEM_LIMIT)


def _dot(a, b):
    return jnp.dot(a, b, preferred_element_type=F32)


def _silu(x):
    return x * jax.nn.sigmoid(x)


def _softplus(x):
    return jnp.maximum(x, 0.0) + jnp.log1p(jnp.exp(-jnp.abs(x)))


def _layer_norm(z, g, b):
    mu = jnp.mean(z, axis=-1, keepdims=True)
    zc = z - mu
    var = jnp.mean(zc * zc, axis=-1, keepdims=True)
    return zc * lax.rsqrt(var + LN_EPS) * g + b


def _mm_kernel(x_ref, w_ref, o_ref):
    o_ref[...] = _dot(x_ref[...], w_ref[...]).astype(o_ref.dtype)


def _tile(n, target):
    best = LANES
    for c in range(LANES, min(n, target) + 1, LANES):
        if n % c == 0:
            best = c
    assert n % best == 0
    return best


def _matmul(x, w, out_dtype, tm=1024, tn=1024):
    m, k = x.shape
    n = w.shape[1]
    tm, tn = _tile(m, tm), _tile(n, tn)
    return pl.pallas_call(
        _mm_kernel,
        grid=(m // tm, n // tn),
        in_specs=[pl.BlockSpec((tm, k), lambda i, j: (i, 0)),
                  pl.BlockSpec((k, tn), lambda i, j: (0, j))],
        out_specs=pl.BlockSpec((tm, tn), lambda i, j: (i, j)),
        out_shape=jax.ShapeDtypeStruct((m, n), out_dtype),
        compiler_params=_cparams(2),
        name="in_proj",
    )(x, w)


def _slab_rows(d):
    assert d % (2 * LANES) == 0
    return d // (2 * LANES)


def _pack_pairs(x):
    half = x.shape[1] // 2
    hi = lax.bitcast_convert_type(x[:, :half].astype(BF16).astype(F32), jnp.uint32)
    lo = lax.bitcast_convert_type(x[:, half:].astype(BF16).astype(F32), jnp.uint32)
    return hi | (lo >> 16)


def _unpack_pairs(w):
    hi = lax.bitcast_convert_type(w & jnp.uint32(0xFFFF0000), F32)
    lo = lax.bitcast_convert_type(w << 16, F32)
    return hi, lo


def _store_slabs(ref, words, rows, nc):
    for s in range(nc):
        ref[pl.ds(s, rows, stride=nc), :] = words[:, s * LANES:(s + 1) * LANES]


def _proj_ln_kernel(*refs, n_in, alpha, tm, nc):
    ys, ws = refs[:n_in], refs[n_in:2 * n_in]
    h_ref, g_ref, b_ref, of_ref, ob_ref, op_ref = refs[2 * n_in:]
    acc = _dot(ys[0][...], ws[0][...])
    for y_ref, w_ref in zip(ys[1:], ws[1:]):
        acc = acc + _dot(y_ref[...], w_ref[...])
    out = _layer_norm(alpha * h_ref[...] + acc, g_ref[...], b_ref[...])
    of_ref[...] = out
    ob_ref[...] = out.astype(BF16)
    _store_slabs(op_ref, _pack_pairs(out), tm, nc)


def _proj_ln(ys, ws, h, g, b, alpha, tm=512):
    t, d = h.shape
    tm = min(tm, t)
    n_in = len(ys)
    nc = _slab_rows(d)
    in_specs = ([pl.BlockSpec((tm, y.shape[1]), lambda i: (i, 0)) for y in ys]
                + [pl.BlockSpec(w.shape, lambda i: (0, 0), pipeline_mode=pl.Buffered(1)) for w in ws]
                + [pl.BlockSpec((tm, d), lambda i: (i, 0)),
                   pl.BlockSpec((1, d), lambda i: (0, 0)),
                   pl.BlockSpec((1, d), lambda i: (0, 0))])
    return pl.pallas_call(
        functools.partial(_proj_ln_kernel, n_in=n_in, alpha=alpha, tm=tm, nc=nc),
        grid=(t // tm,),
        in_specs=in_specs,
        out_specs=[pl.BlockSpec((tm, d), lambda i: (i, 0)),
                   pl.BlockSpec((tm, d), lambda i: (i, 0)),
                   pl.BlockSpec((tm * nc, LANES), lambda i: (i, 0))],
        out_shape=[jax.ShapeDtypeStruct((t, d), F32), jax.ShapeDtypeStruct((t, d), BF16),
                   jax.ShapeDtypeStruct((t * nc, LANES), jnp.uint32)],
        compiler_params=_cparams(1),
        name="out_proj_ln",
    )(*ys, *ws, h, g.reshape(1, d), b.reshape(1, d))


def _retention_kernel(q_ref, k_ref, v_ref, g_ref, cos_ref, sin_ref, intra_ref, qdec_ref, kdec_ref,
                      gn_ref, o_ref, state_ref, *, heads, chunk_decay):
    @pl.when(pl.program_id(1) == 0)
    def _():
        state_ref[...] = jnp.zeros_like(state_ref)

    cos, sin = cos_ref[...], sin_ref[...]
    half = HEAD_DIM // 2
    for h in range(heads):
        sl = slice(h * HEAD_DIM, (h + 1) * HEAD_DIM)
        q = q_ref[:, sl].astype(F32)
        k = k_ref[:, sl].astype(F32)
        q = q * cos + pltpu.roll(q, half, 1) * sin
        k = (k * cos + pltpu.roll(k, half, 1) * sin) * (HEAD_DIM ** -0.5)
        v = v_ref[:, sl]
        s = lax.dot_general(q.astype(BF16), k.astype(BF16), (((1,), (1,)), ((), ())),
                            preferred_element_type=F32) * intra_ref[h]
        state = state_ref[h]
        y = _dot(s.astype(BF16), v) + _dot((q * qdec_ref[h]).astype(BF16), state.astype(BF16))
        kv = lax.dot_general((k * kdec_ref[h]).astype(BF16), v, (((0,), (0,)), ((), ())),
                             preferred_element_type=F32)
        state_ref[h] = state * chunk_decay[h] + kv
        mu = jnp.mean(y, axis=-1, keepdims=True)
        yc = y - mu
        var = jnp.mean(yc * yc, axis=-1, keepdims=True)
        yn = yc * lax.rsqrt(var + LN_EPS) * gn_ref[:, sl]
        o_ref[:, sl] = (yn * _silu(g_ref[:, sl].astype(F32))).astype(BF16)


def _retention(proj, gn_g, batch, seq, chunk):
    t = proj.shape[0]
    width = gn_g.shape[0]
    heads = width // HEAD_DIM
    chunk = min(chunk, seq)
    n_chunks = seq // chunk
    half = HEAD_DIM // 2

    pos = jnp.arange(seq, dtype=F32)
    inv = ROPE_BASE ** (-jnp.arange(half, dtype=F32) / half)
    ang = pos[:, None] * inv[None, :]
    cos2 = jnp.concatenate([jnp.cos(ang), jnp.cos(ang)], axis=1)
    sin2 = jnp.concatenate([-jnp.sin(ang), jnp.sin(ang)], axis=1)

    log_g = jnp.log1p(-jnp.exp2(-5.0 - jnp.arange(heads, dtype=F32)))
    i = jnp.arange(chunk, dtype=F32)
    rel = i[:, None] - i[None, :]
    intra = jnp.where(rel >= 0, jnp.exp(log_g[:, None, None] * jnp.maximum(rel, 0.0)), 0.0)
    kdec = jnp.exp(log_g[:, None] * (chunk - 1.0 - i)[None, :])
    qdec = jnp.exp(log_g[:, None] * (i + 1.0)[None, :])
    kdec = jnp.broadcast_to(kdec[:, :, None], (heads, chunk, HEAD_DIM))
    qdec = jnp.broadcast_to(qdec[:, :, None], (heads, chunk, HEAD_DIM))
    chunk_decay = tuple(math.exp(math.log1p(-2.0 ** (-5.0 - h)) * chunk) for h in range(heads))

    def col(c):
        return pl.BlockSpec((chunk, width), lambda b, n: (b * n_chunks + n, c))

    const3 = lambda shape: pl.BlockSpec(shape, lambda b, n: (0, 0, 0))
    return pl.pallas_call(
        functools.partial(_retention_kernel, heads=heads, chunk_decay=chunk_decay),
        grid=(batch, n_chunks),
        in_specs=[col(0), col(1), col(2), col(3),
                  pl.BlockSpec((chunk, HEAD_DIM), lambda b, n: (n, 0)),
                  pl.BlockSpec((chunk, HEAD_DIM), lambda b, n: (n, 0)),
                  const3((heads, chunk, chunk)),
                  const3((heads, chunk, HEAD_DIM)),
                  const3((heads, chunk, HEAD_DIM)),
                  pl.BlockSpec((1, width), lambda b, n: (0, 0))],
        out_specs=pl.BlockSpec((chunk, width), lambda b, n: (b * n_chunks + n, 0)),
        out_shape=jax.ShapeDtypeStruct((t, width), BF16),
        scratch_shapes=[pltpu.VMEM((heads, HEAD_DIM, HEAD_DIM), F32)],
        compiler_params=_cparams(2),
        name="retention",
    )(proj, proj, proj, proj, cos2, sin2, intra, qdec, kdec, gn_g.reshape(1, width))


def _rglru_kernel(u_ref, gate_ref, cw_ref, cb_ref, wa_ref, ba_ref, wi_ref, bi_ref, lam_ref,
                  o_ref, uext_ref, hc_ref, *, tt, nblk):
    n = pl.program_id(1)

    @pl.when(n == 0)
    def _():
        uext_ref[0:CONV_TAIL, :] = jnp.zeros((CONV_TAIL, uext_ref.shape[1]), F32)
        hc_ref[...] = jnp.zeros_like(hc_ref)

    @pl.when(n > 0)
    def _():
        uext_ref[0:CONV_TAIL, :] = uext_ref[tt:tt + CONV_TAIL, :]

    u = u_ref[...].astype(F32)
    uext_ref[CONV_TAIL:CONV_TAIL + tt, :] = u
    uc = (cb_ref[...] + cw_ref[3:4, :] * u
          + cw_ref[2:3, :] * uext_ref[CONV_TAIL - 1:CONV_TAIL - 1 + tt, :]
          + cw_ref[1:2, :] * uext_ref[CONV_TAIL - 2:CONV_TAIL - 2 + tt, :]
          + cw_ref[0:1, :] * uext_ref[CONV_TAIL - 3:CONV_TAIL - 3 + tt, :])
    ucb = uc.astype(BF16)
    ra = jnp.concatenate(
        [_dot(ucb[:, i * HEAD_DIM:(i + 1) * HEAD_DIM], wa_ref[i]) for i in range(nblk)], axis=1)
    ia = jnp.concatenate(
        [_dot(ucb[:, i * HEAD_DIM:(i + 1) * HEAD_DIM], wi_ref[i]) for i in range(nblk)], axis=1)
    r = jax.nn.sigmoid(ra + ba_ref[...])
    ig = jax.nn.sigmoid(ia + bi_ref[...])
    log_a = (-RG_C) * r * _softplus(-lam_ref[...])
    a = jnp.exp(log_a)
    bt = jnp.sqrt(-jnp.tanh(log_a) * (a * a + 1.0)) * (ig * uc)

    row = lax.broadcasted_iota(jnp.int32, a.shape, 0) & (SUBLANES - 1)
    d = 1
    while d < SUBLANES:
        keep = row >= d
        bt = jnp.where(keep, a * pltpu.roll(bt, d, 0) + bt, bt)
        a = jnp.where(keep, a * pltpu.roll(a, d, 0), a)
        d *= 2
    carry = hc_ref[...]
    groups = []
    for g in range(tt // SUBLANES):
        rows = slice(g * SUBLANES, (g + 1) * SUBLANES)
        hg = a[rows] * carry + bt[rows]
        groups.append(hg)
        carry = hg[SUBLANES - 1:SUBLANES, :]
    h = jnp.concatenate(groups, axis=0)
    hc_ref[...] = carry

    g = gate_ref[...].astype(F32)
    gelu = 0.5 * g * (1.0 + jnp.tanh(math.sqrt(2.0 / math.pi) * (g + 0.044715 * (g * g * g))))
    o_ref[...] = (h * gelu).astype(BF16)


def _rglru(proj, conv_w, conv_b, w_a, b_a, w_i, b_i, lam, batch, seq, tt):
    t = proj.shape[0]
    width = conv_w.shape[1]
    nblk = w_a.shape[0]
    tt = min(tt, seq)
    n_t = seq // tt
    row = lambda x: x.reshape(1, width)
    vec = pl.BlockSpec((1, width), lambda b, n: (0, 0))
    blk = pl.BlockSpec((nblk, HEAD_DIM, HEAD_DIM), lambda b, n: (0, 0, 0))
    return pl.pallas_call(
        functools.partial(_rglru_kernel, tt=tt, nblk=nblk),
        grid=(batch, n_t),
        in_specs=[pl.BlockSpec((tt, width), lambda b, n: (b * n_t + n, 4)),
                  pl.BlockSpec((tt, width), lambda b, n: (b * n_t + n, 5)),
                  pl.BlockSpec((4, width), lambda b, n: (0, 0)),
                  vec, blk, vec, blk, vec, vec],
        out_specs=pl.BlockSpec((tt, width), lambda b, n: (b * n_t + n, 0)),
        out_shape=jax.ShapeDtypeStruct((t, width), BF16),
        scratch_shapes=[pltpu.VMEM((tt + CONV_TAIL, width), F32), pltpu.VMEM((1, width), F32)],
        compiler_params=_cparams(2),
        name="rglru",
    )(proj, proj, conv_w, row(conv_b), w_a.astype(BF16), row(b_a), w_i.astype(BF16), row(b_i),
      row(lam))


FORGET_SPLIT = 3


def _bf16_head(x):
    bits = lax.bitcast_convert_type(x, jnp.uint32) & jnp.uint32(0xFFFF0000)
    return lax.bitcast_convert_type(bits, F32)


def _logf_cumsum_kernel(fl_ref, bf_ref, *o_refs, seq):
    z = fl_ref[...] + bf_ref[...]
    c = jnp.minimum(z, 0.0) - jnp.log1p(jnp.exp(-jnp.abs(z)))
    row = lax.broadcasted_iota(jnp.int32, c.shape, 0)
    d = 1
    while d < seq:
        c = c + jnp.where(row >= d, pltpu.roll(c, d, 0), 0.0)
        d *= 2
    x = c * (-(HEAD_DIM ** 0.5))
    for o_ref in o_refs:
        piece = _bf16_head(x)
        o_ref[...] = piece.astype(BF16)
        x = x - piece


def _logf_cumsum(fl, b_f, batch, seq):
    tile = pl.BlockSpec((seq, LANES), lambda b: (b, 0))
    return pl.pallas_call(
        functools.partial(_logf_cumsum_kernel, seq=seq),
        grid=(batch,),
        in_specs=[tile, pl.BlockSpec((1, LANES), lambda b: (0, 0))],
        out_specs=[tile] * FORGET_SPLIT,
        out_shape=[jax.ShapeDtypeStruct(fl.shape, BF16)] * FORGET_SPLIT,
        compiler_params=_cparams(1),
        name="logf_cumsum",
    )(fl, b_f)


def _fox_kernel(q_ref, k_ref, v_ref, kc_ref, o_ref, kext_ref, vext_ref, m_ref, l_ref, acc_ref, *,
                tq, seq, group):
    qi = pl.program_id(2)
    head = lambda g: slice(g * HEAD_DIM, (g + 1) * HEAD_DIM)

    @pl.when(qi == 0)
    def _():
        for g in range(group):
            kext_ref[g, :, :HEAD_DIM] = k_ref[:, head(g)]
            kext_ref[g, :, HEAD_DIM:] = kc_ref[g]
            vext_ref[g, :, :HEAD_DIM] = v_ref[:, head(g)]
            vext_ref[g, :, HEAD_DIM:] = jnp.ones((seq, HEAD_DIM), BF16)

    lane = lax.broadcasted_iota(jnp.int32, (tq, HEAD_DIM), 1)
    ones = jnp.where(lane < FORGET_SPLIT, 1.0, 0.0).astype(BF16)
    q2 = [jnp.concatenate([q_ref[:, head(g)], ones], axis=1) for g in range(group)]
    c = (HEAD_DIM ** -0.5) * math.log2(math.e)
    reps = tq // HEAD_DIM

    def logits(g, kb):
        off = pl.multiple_of(kb * tq, tq)
        return lax.dot_general(q2[g], kext_ref[g, pl.ds(off, tq), :], (((1,), (1,)), ((), ())),
                               preferred_element_type=F32)

    def fold(g, kb, masked):
        u = logits(g, kb)
        if masked:
            row = lax.broadcasted_iota(jnp.int32, u.shape, 0)
            col = lax.broadcasted_iota(jnp.int32, u.shape, 1)
            u = jnp.where(col <= row, u, -jnp.inf)
        m = m_ref[g]
        m_new = jnp.maximum(m, jnp.broadcast_to(jnp.max(u, axis=-1, keepdims=True), m.shape))
        p = jnp.exp2((u - jnp.concatenate([m_new] * reps, axis=1)) * c)
        alpha = jnp.exp2((m - m_new) * c)
        off = pl.multiple_of(kb * tq, tq)
        pv = _dot(p.astype(BF16), vext_ref[g, pl.ds(off, tq), :])
        m_ref[g] = m_new
        l_ref[g] = alpha * l_ref[g] + pv[:, HEAD_DIM:]
        acc_ref[g] = alpha * acc_ref[g] + pv[:, :HEAD_DIM]

    m_ref[...] = jnp.full(m_ref.shape, -jnp.inf, F32)
    l_ref[...] = jnp.zeros(l_ref.shape, F32)
    acc_ref[...] = jnp.zeros(acc_ref.shape, F32)

    def body(kb, carry):
        for g in range(group):
            fold(g, kb, False)
        return carry

    lax.fori_loop(0, qi, body, 0)
    for g in range(group):
        fold(g, qi, True)
        o_ref[:, head(g)] = (acc_ref[g] / l_ref[g]).astype(BF16)


def _fox_attention(qkv, kc, batch, seq, heads, tq, group=4):
    t = qkv.shape[0]
    tq = min(tq, max(HEAD_DIM, seq // 4))
    nq = seq // tq
    assert tq % HEAD_DIM == 0 and seq % tq == 0 and heads % group == 0
    hg = heads // group
    w = group * HEAD_DIM
    return pl.pallas_call(
        functools.partial(_fox_kernel, tq=tq, seq=seq, group=group),
        grid=(batch, hg, nq),
        in_specs=[
            pl.BlockSpec((tq, w), lambda b, h, qi: (b * nq + qi, h)),
            pl.BlockSpec((seq, w), lambda b, h, qi: (b, hg + h)),
            pl.BlockSpec((seq, w), lambda b, h, qi: (b, 2 * hg + h)),
            pl.BlockSpec((group, seq, HEAD_DIM), lambda b, h, qi: (b * hg + h, 0, 0)),
        ],
        out_specs=pl.BlockSpec((tq, w), lambda b, h, qi: (b * nq + qi, h)),
        out_shape=jax.ShapeDtypeStruct((t, heads * HEAD_DIM), BF16),
        scratch_shapes=[pltpu.VMEM((group, seq, 2 * HEAD_DIM), BF16),
                        pltpu.VMEM((group, seq, 2 * HEAD_DIM), BF16)]
        + [pltpu.VMEM((group, tq, HEAD_DIM), F32)] * 3,
        compiler_params=_cparams(3),
        name="fox_attention",
    )(qkv, qkv, qkv, kc)


def _forget_key_bias(pieces, batch, seq, heads):
    kc = jnp.stack([p.reshape(batch, seq, LANES)[:, :, :heads] for p in pieces], axis=-1)
    kc = jnp.transpose(kc, (0, 2, 1, 3)).reshape(batch * heads, seq, FORGET_SPLIT)
    return jnp.pad(kc, ((0, 0), (0, 0), (0, HEAD_DIM - FORGET_SPLIT)))


def _router_kernel(h_ref, rw_ref, rb_ref, idx_ref, wts_ref, rank_ref, cnt_ref, *, tm):
    @pl.when(pl.program_id(0) == 0)
    def _():
        cnt_ref[...] = jnp.zeros_like(cnt_ref)

    h = h_ref[...]
    h_hi = _bf16_head(h)
    first_order = _dot(h_hi.astype(BF16), rw_ref[...])
    logits = (first_order[:, :LANES] + first_order[:, LANES:]
              + _dot((h - h_hi).astype(BF16), rw_ref[:, :LANES]))
    scores = jax.nn.sigmoid(logits)
    sel = scores + rb_ref[...]
    lane = lax.broadcasted_iota(jnp.int32, sel.shape, 1)
    lane_f = lane.astype(F32)
    idx = jnp.zeros(sel.shape, F32)
    wts = jnp.zeros(sel.shape, F32)
    chosen = jnp.zeros(sel.shape, F32)
    hits = []
    for k in range(TOP_K):
        top = jnp.max(sel, axis=-1, keepdims=True)
        first = jnp.min(jnp.where(sel == top, lane_f, float(LANES)), axis=-1, keepdims=True)
        hit = lane_f == first
        hits.append(hit)
        idx = jnp.where(lane == k, first, idx)
        wts = jnp.where(lane == k, jnp.sum(jnp.where(hit, scores, 0.0), axis=-1, keepdims=True), wts)
        chosen = jnp.where(hit, 1.0, chosen)
        sel = jnp.where(hit, -jnp.inf, sel)
    wts = wts / jnp.sum(wts, axis=-1, keepdims=True) * ROUTED_SCALE

    r = lax.broadcasted_iota(jnp.int32, (tm, tm), 0)
    c = lax.broadcasted_iota(jnp.int32, (tm, tm), 1)
    before = jnp.where(c < r, 1.0, 0.0).astype(BF16)
    rank_all = _dot(before, chosen.astype(BF16)) + cnt_ref[...]
    rank = jnp.zeros(sel.shape, F32)
    for k in range(TOP_K):
        rank = jnp.where(lane == k, jnp.sum(jnp.where(hits[k], rank_all, 0.0), axis=-1, keepdims=True),
                         rank)
    cnt_ref[...] = cnt_ref[...] + jnp.sum(chosen, axis=0, keepdims=True)
    idx_ref[...] = idx.astype(jnp.int32)
    wts_ref[...] = wts
    rank_ref[...] = rank.astype(jnp.int32)


def _router(h, router_w, router_bias, tm=256):
    t, d = h.shape
    e = router_w.shape[1]
    tm = min(tm, t)
    rw = jnp.pad(router_w, ((0, 0), (0, LANES - e)))
    rw_hi = _bf16_head(rw)
    rw = jnp.concatenate([rw_hi.astype(BF16), (rw - rw_hi).astype(BF16)], axis=1)
    rb = jnp.pad(router_bias.astype(F32), (0, LANES - e), constant_values=-jnp.inf).reshape(1, LANES)
    tile = pl.BlockSpec((tm, LANES), lambda i: (i, 0))
    return pl.pallas_call(
        functools.partial(_router_kernel, tm=tm),
        grid=(t // tm,),
        in_specs=[pl.BlockSpec((tm, d), lambda i: (i, 0)),
                  pl.BlockSpec((d, 2 * LANES), lambda i: (0, 0)),
                  pl.BlockSpec((1, LANES), lambda i: (0, 0))],
        out_specs=[tile, tile, tile, pl.BlockSpec((1, LANES), lambda i: (0, 0))],
        out_shape=[jax.ShapeDtypeStruct((t, LANES), jnp.int32),
                   jax.ShapeDtypeStruct((t, LANES), F32),
                   jax.ShapeDtypeStruct((t, LANES), jnp.int32),
                   jax.ShapeDtypeStruct((1, LANES), F32)],
        compiler_params=_cparams(1),
        name="router_topk",
    )(h, rw, rb)


def _dispatch_kernel(pstart_ref, cnt_ref, dest_ref, x_ref, xs_ref, zero_ref, sem, *, tm, n_exp, nc,
                     n_blocks):
    i = pl.program_id(0)
    blk = EXPERT_BLOCK * nc

    def row_copy(r, dst):
        return pltpu.make_async_copy(x_ref.at[pl.ds(pl.multiple_of(r * nc, nc), nc), :],
                                     xs_ref.at[pl.ds(pl.multiple_of(dst * nc, nc), nc), :], sem)

    def zero_run(dst, n):
        return pltpu.make_async_copy(zero_ref.at[pl.ds(0, n * nc), :],
                                     xs_ref.at[pl.ds(pl.multiple_of(dst * nc, nc), n * nc), :], sem)

    def zero_block(b):
        return pltpu.make_async_copy(zero_ref, xs_ref.at[pl.ds(pl.multiple_of(b * blk, blk), blk), :],
                                     sem)

    def pad_rows(e, do):
        first = pstart_ref[e] + cnt_ref[e]
        n = pstart_ref[e + 1] - first
        bit = EXPERT_BLOCK // 2
        while bit:
            pl.when((n & bit) != 0)(functools.partial(do, first, bit))
            first = first + (n & bit)
            bit //= 2

    def first_unused_block():
        return lax.div(pstart_ref[n_exp], EXPERT_BLOCK)

    @pl.when(i == 0)
    def _():
        zero_ref[...] = jnp.zeros_like(zero_ref)

        def per_expert(e, carry):
            pad_rows(e, lambda first, n: zero_run(first, n).start())
            return carry

        lax.fori_loop(0, n_exp, per_expert, 0)
        lax.fori_loop(first_unused_block(), n_blocks, lambda b, c: (zero_block(b).start(), c)[1], 0)

    def issue(r, carry):
        for k in range(TOP_K):
            row_copy(r, dest_ref[0, 0, r * TOP_K + k]).start(priority=k % 2)
        return carry

    lax.fori_loop(0, tm, issue, 0)
    all_rows = xs_ref.at[pl.ds(0, tm * TOP_K * nc), :]
    pltpu.make_async_copy(all_rows, all_rows, sem).wait()

    @pl.when(i == 0)
    def _():
        def per_expert(e, carry):
            pad_rows(e, lambda first, n: zero_run(0, n).wait())
            return carry

        lax.fori_loop(0, n_exp, per_expert, 0)
        lax.fori_loop(first_unused_block(), n_blocks, lambda b, c: (zero_block(0).wait(), c)[1], 0)


def _dispatch(xp, dest, pstart, counts, n_rows, nc, tm=256):
    t = xp.shape[0] // nc
    tm = min(tm, t)
    n_exp = counts.shape[0]
    assert n_rows >= tm * TOP_K
    grid_spec = pltpu.PrefetchScalarGridSpec(
        num_scalar_prefetch=2,
        grid=(t // tm,),
        in_specs=[pl.BlockSpec((1, 1, tm * TOP_K), lambda i, ps, cn: (i, 0, 0),
                               memory_space=pltpu.SMEM),
                  pl.BlockSpec((tm * nc, LANES), lambda i, ps, cn: (i, 0))],
        out_specs=pl.BlockSpec(memory_space=pl.ANY),
        scratch_shapes=[pltpu.VMEM((EXPERT_BLOCK * nc, LANES), xp.dtype),
                        pltpu.SemaphoreType.DMA(())],
    )
    assert n_rows % EXPERT_BLOCK == 0
    return pl.pallas_call(
        functools.partial(_dispatch_kernel, tm=tm, n_exp=n_exp, nc=nc,
                          n_blocks=n_rows // EXPERT_BLOCK),
        grid_spec=grid_spec,
        out_shape=jax.ShapeDtypeStruct((n_rows * nc, LANES), xp.dtype),
        compiler_params=_cparams(1),
        name="moe_dispatch",
    )(pstart, counts, dest.reshape(t // tm, 1, tm * TOP_K), xp)


def _expert_ffn_kernel(be_ref, nb_ref, ord_ref, nxt_ref, nv_ref, x_ref, wg_hbm, wu_hbm, wd_hbm, o_ref,
                       wgf_ref, wuf_ref, wdf_ref, wgb_ref, wub_ref, wdb_ref, sems, *, nc, layer):
    i = pl.program_id(0)
    used = i < nb_ref[0]
    expert = be_ref[i]
    new_expert = jnp.logical_or(i == 0, expert != be_ref[jnp.maximum(i - 1, 0)])
    slot = ord_ref[i] % 2
    half = EXPERT_BLOCK // 2
    short = nv_ref[i] <= half

    def weight_copies(e, sl):
        return [pltpu.make_async_copy(src.at[layer, e], dst.at[sl], sems.at[sl])
                for src, dst in ((wg_hbm, wgf_ref), (wu_hbm, wuf_ref), (wd_hbm, wdf_ref))]

    @pl.when(i == 0)
    def _():
        for cp in weight_copies(expert, slot):
            cp.start()

    @pl.when(jnp.logical_and(used, new_expert))
    def _():
        for cp in weight_copies(expert, slot):
            cp.wait()
        for sl in range(2):
            @pl.when(slot == sl)
            def _(sl=sl):
                wgb_ref[...] = wgf_ref[sl].astype(BF16)
                wub_ref[...] = wuf_ref[sl].astype(BF16)
                wdb_ref[...] = wdf_ref[sl].astype(BF16)

        @pl.when(nxt_ref[i] >= 0)
        def _():
            for cp in weight_copies(nxt_ref[i], 1 - slot):
                cp.start()

    def ffn(rows):
        xt = jnp.swapaxes(x_ref[0:rows * nc, :].reshape(rows // nc, nc, nc, LANES), 1, 2)
        halves = [_unpack_pairs(xt[:, s].reshape(rows, LANES)) for s in range(nc)]
        x = jnp.concatenate([h[0].astype(BF16) for h in halves] + [h[1].astype(BF16) for h in halves],
                            axis=1)
        hb = (_silu(_dot(x, wgb_ref[...])) * _dot(x, wub_ref[...])).astype(BF16)
        _store_slabs(o_ref, _pack_pairs(_dot(hb, wdb_ref[...])), rows, nc)

    @pl.when(jnp.logical_and(used, jnp.logical_not(short)))
    def _():
        ffn(EXPERT_BLOCK)

    @pl.when(jnp.logical_and(used, short))
    def _():
        ffn(half)
        o_ref[half * nc:, :] = jnp.zeros((half * nc, LANES), o_ref.dtype)

    @pl.when(jnp.logical_not(used))
    def _():
        o_ref[...] = jnp.zeros_like(o_ref)


def _expert_ffn(xs, block_e, n_used, block_ord, block_next, n_valid, w_gate, w_up, w_down, layer, nc):
    _, _, d, f = w_gate.shape
    nb = xs.shape[0] // (EXPERT_BLOCK * nc)
    any_space = pl.BlockSpec(memory_space=pl.ANY)
    grid_spec = pltpu.PrefetchScalarGridSpec(
        num_scalar_prefetch=5,
        grid=(nb,),
        in_specs=[
            pl.BlockSpec((EXPERT_BLOCK * nc, LANES),
                         lambda i, be, nu, od, nx, nv: (jnp.minimum(i, nu[0] - 1), 0)),
            any_space, any_space, any_space,
        ],
        out_specs=pl.BlockSpec((EXPERT_BLOCK * nc, LANES), lambda i, be, nu, od, nx, nv: (i, 0)),
        scratch_shapes=[pltpu.VMEM((2, d, f), F32), pltpu.VMEM((2, d, f), F32),
                        pltpu.VMEM((2, f, d), F32),
                        pltpu.VMEM((d, f), BF16), pltpu.VMEM((d, f), BF16), pltpu.VMEM((f, d), BF16),
                        pltpu.SemaphoreType.DMA((2,))],
    )
    return pl.pallas_call(
        functools.partial(_expert_ffn_kernel, nc=nc, layer=layer),
        grid_spec=grid_spec,
        out_shape=jax.ShapeDtypeStruct(xs.shape, xs.dtype),
        compiler_params=_cparams(1),
        name="expert_ffn",
    )(block_e, n_used, block_ord, block_next, n_valid, xs, w_gate, w_up, w_down)


def _combine_kernel(dest_ref, destn_ref, wts_ref, hf_ref, hb_ref, sg_ref, su_ref, sd_ref, g_ref,
                    b_ref, ys_ref, of_ref, ob_ref, buf_ref, sems, *, tm, nc, alpha):
    i = pl.program_id(0)
    slot = i % 2

    def slab_copy(sl, k, r, src):
        dst_row = pl.multiple_of((k * tm + r) * nc, nc)
        return pltpu.make_async_copy(ys_ref.at[pl.ds(pl.multiple_of(src * nc, nc), nc), :],
                                     buf_ref.at[sl, pl.ds(dst_row, nc), :], sems.at[sl])

    def gather(sl, rows_ref):
        def issue(j, carry):
            for r in (2 * j, 2 * j + 1):
                for k in range(TOP_K):
                    slab_copy(sl, k, r, rows_ref[0, 0, r * TOP_K + k]).start(priority=k % 2)
            return carry

        lax.fori_loop(0, tm // 2, issue, 0)

    @pl.when(i == 0)
    def _():
        gather(slot, dest_ref)

    @pl.when(i + 1 < pl.num_programs(0))
    def _():
        gather(1 - slot, destn_ref)

    xb = hb_ref[...]
    shared = _dot((_silu(_dot(xb, sg_ref[...])) * _dot(xb, su_ref[...])).astype(BF16), sd_ref[...])
    z = alpha * hf_ref[...] + shared

    pltpu.make_async_copy(buf_ref.at[slot], buf_ref.at[slot], sems.at[slot]).wait()
    wts = wts_ref[...]
    his = [None] * nc
    los = [None] * nc
    for k in range(TOP_K):
        wk = wts[:, k:k + 1]
        for s in range(nc):
            hi, lo = _unpack_pairs(buf_ref[slot, pl.ds(k * tm * nc + s, tm, stride=nc), :])
            his[s] = wk * hi if k == 0 else his[s] + wk * hi
            los[s] = wk * lo if k == 0 else los[s] + wk * lo
    out = _layer_norm(z + jnp.concatenate(his + los, axis=1), g_ref[...], b_ref[...])
    of_ref[...] = out
    ob_ref[...] = out.astype(BF16)


def _combine(ys, dest, wts, h_f32, h_bf16, sh_gate, sh_up, sh_down, g, b, alpha, nc, tm=256):
    t, d = h_f32.shape
    f = sh_gate.shape[1]
    tm = min(tm, t)
    n = t // tm
    smem = lambda im: pl.BlockSpec((1, 1, tm * TOP_K), im, memory_space=pltpu.SMEM)
    tile = lambda w: pl.BlockSpec((tm, w), lambda i: (i, 0))
    const = lambda shape: pl.BlockSpec(shape, lambda i: (0, 0))
    rows = dest.reshape(n, 1, tm * TOP_K)
    return pl.pallas_call(
        functools.partial(_combine_kernel, tm=tm, nc=nc, alpha=alpha),
        grid=(n,),
        in_specs=[smem(lambda i: (i, 0, 0)), smem(lambda i: (jnp.minimum(i + 1, n - 1), 0, 0)),
                  tile(LANES), tile(d), tile(d),
                  const((d, f)), const((d, f)), const((f, d)), const((1, d)), const((1, d)),
                  pl.BlockSpec(memory_space=pl.ANY)],
        out_specs=[tile(d), tile(d)],
        out_shape=[jax.ShapeDtypeStruct((t, d), F32), jax.ShapeDtypeStruct((t, d), BF16)],
        scratch_shapes=[pltpu.VMEM((2, TOP_K * tm * nc, LANES), ys.dtype),
                        pltpu.SemaphoreType.DMA((2,))],
        compiler_params=_cparams(1),
        name="moe_combine",
    )(rows, rows, wts, h_f32, h_bf16, sh_gate, sh_up, sh_down, g.reshape(1, d), b.reshape(1, d), ys)


def _moe_ln(h_f32, h_bf16, h_slabs, router_w, router_bias, w_gate, w_up, w_down, layer, sh_gate,
            sh_up, sh_down, g, b, alpha):
    t, d = h_f32.shape
    n_exp = router_w.shape[1]
    nc = _slab_rows(d)
    idx, wts, rank, counts = _router(h_f32, router_w, router_bias)
    idx, rank = idx[:, :TOP_K], rank[:, :TOP_K]

    counts = counts[0, :n_exp].astype(jnp.int32)
    padded = (counts + EXPERT_BLOCK - 1) // EXPERT_BLOCK * EXPERT_BLOCK
    pends = jnp.cumsum(padded)
    pstart = jnp.concatenate([jnp.zeros((1,), jnp.int32), pends]).astype(jnp.int32)
    n_blocks = t * TOP_K // EXPERT_BLOCK + n_exp
    block_start = jnp.arange(n_blocks, dtype=jnp.int32) * EXPERT_BLOCK
    block_e = jnp.sum((pends[None, :] <= block_start[:, None]).astype(jnp.int32), axis=1)
    block_e = jnp.minimum(block_e, n_exp - 1)
    n_used = (pends[-1:] // EXPERT_BLOCK).astype(jnp.int32)
    experts = jnp.arange(n_exp, dtype=jnp.int32)
    owns = padded > 0
    order_e = jnp.cumsum(owns.astype(jnp.int32)) - 1
    later = jnp.logical_and(owns[None, :], experts[None, :] > experts[:, None])
    next_e = jnp.min(jnp.where(later, experts[None, :], n_exp), axis=1)
    next_e = jnp.where(next_e < n_exp, next_e, -1)
    of_block = (block_e[:, None] == experts[None, :]).astype(jnp.int32)
    block_ord = jnp.sum(of_block * order_e[None, :], axis=1)
    block_next = jnp.sum(of_block * next_e[None, :], axis=1)
    rows_left = counts[None, :] - (block_start[:, None] - pstart[None, :n_exp])
    n_valid = jnp.clip(jnp.sum(of_block * rows_left, axis=1), 0, EXPERT_BLOCK)
    group_start = jnp.sum(jnp.where(idx[:, :, None] == jnp.arange(n_exp, dtype=jnp.int32),
                                    pstart[:n_exp], 0), axis=-1)
    dest = group_start + rank

    xs = _dispatch(h_slabs, dest, pstart, counts, n_blocks * EXPERT_BLOCK, nc)
    ys = _expert_ffn(xs, block_e, n_used, block_ord, block_next, n_valid, w_gate, w_up, w_down, layer,
                     nc)
    return _combine(ys, dest, wts, h_f32, h_bf16, sh_gate.astype(BF16), sh_up.astype(BF16),
                    sh_down.astype(BF16), g, b, alpha, nc)


def kernel(x, even_w_in, ret_gn_g, rg_conv_w, rg_conv_b, rg_w_a, rg_b_a, rg_w_i, rg_b_i, rg_lambda,
           even_w_out, fox_w_in, fox_b_f, fox_w_out, ln_g, ln_b, router_w, router_bias, exp_w_gate,
           exp_w_up, exp_w_down, sh_w_gate, sh_w_up, sh_w_down):
    batch, seq, d = x.shape
    depth = ln_g.shape[0]
    alpha = (2 * depth) ** 0.25
    ret_width = ret_gn_g.shape[1]
    fox_heads = fox_b_f.shape[1]

    h_f32 = x.reshape(batch * seq, d)
    h_bf16 = h_f32.astype(BF16)
    for layer in range(depth):
        j = layer // 2
        if layer % 2 == 0:
            proj = _matmul(h_bf16, even_w_in[j].astype(BF16), BF16)
            y_ret = _retention(proj, ret_gn_g[j], batch, seq, chunk=256)
            y_rnn = _rglru(proj, rg_conv_w[j], rg_conv_b[j], rg_w_a[j], rg_b_a[j], rg_w_i[j],
                           rg_b_i[j], rg_lambda[j], batch, seq, tt=256)
            w_out = even_w_out[j].astype(BF16)
            ys, ws = [y_ret, y_rnn], [w_out[:ret_width], w_out[ret_width:]]
        else:
            w_in = fox_w_in[j]
            qkv = _matmul(h_bf16, w_in[:, :3 * d].astype(BF16), BF16)
            w_f = jnp.pad(w_in[:, 3 * d:], ((0, 0), (0, LANES - fox_heads))).astype(BF16)
            fl = _matmul(h_bf16, w_f, F32)
            b_f = jnp.pad(fox_b_f[j], (0, LANES - fox_heads)).reshape(1, LANES)
            kc = _forget_key_bias(_logf_cumsum(fl, b_f, batch, seq), batch, seq, fox_heads)
            o = _fox_attention(qkv, kc, batch, seq, fox_heads, tq=512)
            ys, ws = [o], [fox_w_out[j].astype(BF16)]
        h_f32, h_bf16, h_slabs = _proj_ln(ys, ws, h_f32, ln_g[layer, 0], ln_b[layer, 0], alpha)
        h_f32, h_bf16 = _moe_ln(h_f32, h_bf16, h_slabs, router_w[layer], router_bias[layer],
                                exp_w_gate, exp_w_up, exp_w_down, layer,
                                sh_w_gate[layer], sh_w_up[layer], sh_w_down[layer],
                                ln_g[layer, 1], ln_b[layer, 1], alpha)
    return h_f32.reshape(batch, seq, d)
```

```python
import functools
import math

import numpy as np
import jax
import jax.numpy as jnp
from jax import lax
from jax.experimental import pallas as pl
from jax.experimental.pallas import tpu as pltpu

F32 = jnp.float32
BF16 = jnp.bfloat16

HEAD_DIM = 128
TOP_K = 8
ROUTED_SCALE = 2.5
ROPE_BASE = 10000.0
RG_C = 8.0
LN_EPS = 1e-5
CONV_TAIL = 8

V7X_VMEM_BYTES = 64 * 1024 * 1024
VMEM_LIMIT = V7X_VMEM_BYTES - 8 * 1024 * 1024
LANES = 128
SUBLANES = 8

EXPERT_BLOCK = 512


def _cparams(grid_rank):
    return pltpu.CompilerParams(dimension_semantics=("arbitrary",) * grid_rank,
                                vmem_limit_bytes=VMEM_LIMIT)


def _dot(a, b):
    return jnp.dot(a, b, preferred_element_type=F32)


def _silu(x):
    return x * jax.nn.sigmoid(x)


def _softplus(x):
    return jnp.maximum(x, 0.0) + jnp.log1p(jnp.exp(-jnp.abs(x)))


def _layer_norm(z, g, b):
    mu = jnp.mean(z, axis=-1, keepdims=True)
    zc = z - mu
    var = jnp.mean(zc * zc, axis=-1, keepdims=True)
    return zc * lax.rsqrt(var + LN_EPS) * g + b


def _mm_kernel(x_ref, w_ref, o_ref):
    o_ref[...] = _dot(x_ref[...].astype(BF16), w_ref[...]).astype(o_ref.dtype)


def _tile(n, target):
    best = LANES
    for c in range(LANES, min(n, target) + 1, LANES):
        if n % c == 0:
            best = c
    assert n % best == 0
    return best


def _matmul(x, w, out_dtype, tm=1024, tn=1024):
    m, k = x.shape
    n = w.shape[1]
    tm, tn = _tile(m, tm), _tile(n, tn)
    return pl.pallas_call(
        _mm_kernel,
        grid=(m // tm, n // tn),
        in_specs=[pl.BlockSpec((tm, k), lambda i, j: (i, 0)),
                  pl.BlockSpec((k, tn), lambda i, j: (0, j))],
        out_specs=pl.BlockSpec((tm, tn), lambda i, j: (i, j)),
        out_shape=jax.ShapeDtypeStruct((m, n), out_dtype),
        compiler_params=_cparams(2),
        name="in_proj",
    )(x, w)


def _slab_rows(d):
    assert d % (2 * LANES) == 0
    return d // (2 * LANES)


def _pack_pairs(x):
    half = x.shape[1] // 2
    hi = lax.bitcast_convert_type(x[:, :half].astype(BF16).astype(F32), jnp.uint32)
    lo = lax.bitcast_convert_type(x[:, half:].astype(BF16).astype(F32), jnp.uint32)
    return hi | (lo >> 16)


def _unpack_pairs(w):
    hi = lax.bitcast_convert_type(w & jnp.uint32(0xFFFF0000), F32)
    lo = lax.bitcast_convert_type(w << 16, F32)
    return hi, lo


def _store_slabs(ref, words, rows, nc):
    for s in range(nc):
        ref[pl.ds(s, rows, stride=nc), :] = words[:, s * LANES:(s + 1) * LANES]


def _proj_ln_kernel(*refs, n_in, alpha, tm, nc):
    ys, ws = refs[:n_in], refs[n_in:2 * n_in]
    h_ref, g_ref, b_ref, of_ref, ob_ref, op_ref = refs[2 * n_in:]
    acc = _dot(ys[0][...], ws[0][...])
    for y_ref, w_ref in zip(ys[1:], ws[1:]):
        acc = acc + _dot(y_ref[...], w_ref[...])
    out = _layer_norm(alpha * h_ref[...] + acc, g_ref[...], b_ref[...])
    of_ref[...] = out
    ob_ref[...] = out.astype(BF16)
    _store_slabs(op_ref, _pack_pairs(out), tm, nc)


def _proj_ln(ys, ws, h, g, b, alpha, tm=512):
    t, d = h.shape
    tm = min(tm, t)
    n_in = len(ys)
    nc = _slab_rows(d)
    in_specs = ([pl.BlockSpec((tm, y.shape[1]), lambda i: (i, 0)) for y in ys]
                + [pl.BlockSpec(w.shape, lambda i: (0, 0), pipeline_mode=pl.Buffered(1)) for w in ws]
                + [pl.BlockSpec((tm, d), lambda i: (i, 0)),
                   pl.BlockSpec((1, d), lambda i: (0, 0)),
                   pl.BlockSpec((1, d), lambda i: (0, 0))])
    return pl.pallas_call(
        functools.partial(_proj_ln_kernel, n_in=n_in, alpha=alpha, tm=tm, nc=nc),
        grid=(t // tm,),
        in_specs=in_specs,
        out_specs=[pl.BlockSpec((tm, d), lambda i: (i, 0)),
                   pl.BlockSpec((tm, d), lambda i: (i, 0)),
                   pl.BlockSpec((tm * nc, LANES), lambda i: (i, 0))],
        out_shape=[jax.ShapeDtypeStruct((t, d), F32), jax.ShapeDtypeStruct((t, d), BF16),
                   jax.ShapeDtypeStruct((t * nc, LANES), jnp.uint32)],
        compiler_params=_cparams(1),
        name="out_proj_ln",
    )(*ys, *ws, h, g.reshape(1, d), b.reshape(1, d))


def _retention_kernel(q_ref, k_ref, v_ref, g_ref, cos_ref, sin_ref, intra_ref, qdec_ref, kdec_ref,
                      gn_ref, o_ref, state_ref, *, heads, chunk_decay):
    @pl.when(pl.program_id(1) == 0)
    def _():
        state_ref[...] = jnp.zeros_like(state_ref)

    cos, sin = cos_ref[...], sin_ref[...]
    half = HEAD_DIM // 2
    for h in range(heads):
        sl = slice(h * HEAD_DIM, (h + 1) * HEAD_DIM)
        q = q_ref[:, sl].astype(F32)
        k = k_ref[:, sl].astype(F32)
        q = q * cos + pltpu.roll(q, half, 1) * sin
        k = (k * cos + pltpu.roll(k, half, 1) * sin) * (HEAD_DIM ** -0.5)
        v = v_ref[:, sl]
        s = lax.dot_general(q.astype(BF16), k.astype(BF16), (((1,), (1,)), ((), ())),
                            preferred_element_type=F32) * intra_ref[h]
        state = state_ref[h]
        y = _dot(s.astype(BF16), v) + _dot((q * qdec_ref[h]).astype(BF16), state.astype(BF16))
        kv = lax.dot_general((k * kdec_ref[h]).astype(BF16), v, (((0,), (0,)), ((), ())),
                             preferred_element_type=F32)
        state_ref[h] = state * chunk_decay[h] + kv
        mu = jnp.mean(y, axis=-1, keepdims=True)
        yc = y - mu
        var = jnp.mean(yc * yc, axis=-1, keepdims=True)
        yn = yc * lax.rsqrt(var + LN_EPS) * gn_ref[:, sl]
        o_ref[:, sl] = (yn * _silu(g_ref[:, sl].astype(F32))).astype(BF16)


def _retention(proj, gn_g, batch, seq, chunk):
    t = proj.shape[0]
    width = gn_g.shape[0]
    heads = width // HEAD_DIM
    chunk = min(chunk, seq)
    n_chunks = seq // chunk
    half = HEAD_DIM // 2

    pos = jnp.arange(seq, dtype=F32)
    inv = ROPE_BASE ** (-jnp.arange(half, dtype=F32) / half)
    ang = pos[:, None] * inv[None, :]
    cos2 = jnp.concatenate([jnp.cos(ang), jnp.cos(ang)], axis=1)
    sin2 = jnp.concatenate([-jnp.sin(ang), jnp.sin(ang)], axis=1)

    log_g = jnp.log1p(-jnp.exp2(-5.0 - jnp.arange(heads, dtype=F32)))
    i = jnp.arange(chunk, dtype=F32)
    rel = i[:, None] - i[None, :]
    intra = jnp.where(rel >= 0, jnp.exp(log_g[:, None, None] * jnp.maximum(rel, 0.0)), 0.0)
    kdec = jnp.exp(log_g[:, None] * (chunk - 1.0 - i)[None, :])
    qdec = jnp.exp(log_g[:, None] * (i + 1.0)[None, :])
    kdec = jnp.broadcast_to(kdec[:, :, None], (heads, chunk, HEAD_DIM))
    qdec = jnp.broadcast_to(qdec[:, :, None], (heads, chunk, HEAD_DIM))
    chunk_decay = tuple(math.exp(math.log1p(-2.0 ** (-5.0 - h)) * chunk) for h in range(heads))

    def col(c):
        return pl.BlockSpec((chunk, width), lambda b, n: (b * n_chunks + n, c))

    const3 = lambda shape: pl.BlockSpec(shape, lambda b, n: (0, 0, 0))
    return pl.pallas_call(
        functools.partial(_retention_kernel, heads=heads, chunk_decay=chunk_decay),
        grid=(batch, n_chunks),
        in_specs=[col(0), col(1), col(2), col(3),
                  pl.BlockSpec((chunk, HEAD_DIM), lambda b, n: (n, 0)),
                  pl.BlockSpec((chunk, HEAD_DIM), lambda b, n: (n, 0)),
                  const3((heads, chunk, chunk)),
                  const3((heads, chunk, HEAD_DIM)),
                  const3((heads, chunk, HEAD_DIM)),
                  pl.BlockSpec((1, width), lambda b, n: (0, 0))],
        out_specs=pl.BlockSpec((chunk, width), lambda b, n: (b * n_chunks + n, 0)),
        out_shape=jax.ShapeDtypeStruct((t, width), BF16),
        scratch_shapes=[pltpu.VMEM((heads, HEAD_DIM, HEAD_DIM), F32)],
        compiler_params=_cparams(2),
        name="retention",
    )(proj, proj, proj, proj, cos2, sin2, intra, qdec, kdec, gn_g.reshape(1, width))


def _rglru_kernel(u_ref, gate_ref, cw_ref, cb_ref, wa_ref, ba_ref, wi_ref, bi_ref, lam_ref,
                  o_ref, uext_ref, hc_ref, *, tt, nblk):
    n = pl.program_id(1)

    @pl.when(n == 0)
    def _():
        uext_ref[0:CONV_TAIL, :] = jnp.zeros((CONV_TAIL, uext_ref.shape[1]), F32)
        hc_ref[...] = jnp.zeros_like(hc_ref)

    @pl.when(n > 0)
    def _():
        uext_ref[0:CONV_TAIL, :] = uext_ref[tt:tt + CONV_TAIL, :]

    u = u_ref[...].astype(F32)
    uext_ref[CONV_TAIL:CONV_TAIL + tt, :] = u
    uc = (cb_ref[...] + cw_ref[3:4, :] * u
          + cw_ref[2:3, :] * uext_ref[CONV_TAIL - 1:CONV_TAIL - 1 + tt, :]
          + cw_ref[1:2, :] * uext_ref[CONV_TAIL - 2:CONV_TAIL - 2 + tt, :]
          + cw_ref[0:1, :] * uext_ref[CONV_TAIL - 3:CONV_TAIL - 3 + tt, :])
    ucb = uc.astype(BF16)
    ra = jnp.concatenate(
        [_dot(ucb[:, i * HEAD_DIM:(i + 1) * HEAD_DIM], wa_ref[i]) for i in range(nblk)], axis=1)
    ia = jnp.concatenate(
        [_dot(ucb[:, i * HEAD_DIM:(i + 1) * HEAD_DIM], wi_ref[i]) for i in range(nblk)], axis=1)
    r = jax.nn.sigmoid(ra + ba_ref[...])
    ig = jax.nn.sigmoid(ia + bi_ref[...])
    log_a = (-RG_C) * r * _softplus(-lam_ref[...])
    a = jnp.exp(log_a)
    bt = jnp.sqrt(-jnp.tanh(log_a) * (a * a + 1.0)) * (ig * uc)

    row = lax.broadcasted_iota(jnp.int32, a.shape, 0) & (SUBLANES - 1)
    d = 1
    while d < SUBLANES:
        keep = row >= d
        bt = jnp.where(keep, a * pltpu.roll(bt, d, 0) + bt, bt)
        a = jnp.where(keep, a * pltpu.roll(a, d, 0), a)
        d *= 2
    carry = hc_ref[...]
    groups = []
    for g in range(tt // SUBLANES):
        rows = slice(g * SUBLANES, (g + 1) * SUBLANES)
        hg = a[rows] * carry + bt[rows]
        groups.append(hg)
        carry = hg[SUBLANES - 1:SUBLANES, :]
    h = jnp.concatenate(groups, axis=0)
    hc_ref[...] = carry

    g = gate_ref[...].astype(F32)
    gelu = 0.5 * g * (1.0 + jnp.tanh(math.sqrt(2.0 / math.pi) * (g + 0.044715 * (g * g * g))))
    o_ref[...] = (h * gelu).astype(BF16)


def _rglru(proj, conv_w, conv_b, w_a, b_a, w_i, b_i, lam, batch, seq, tt):
    t = proj.shape[0]
    width = conv_w.shape[1]
    nblk = w_a.shape[0]
    tt = min(tt, seq)
    n_t = seq // tt
    row = lambda x: x.reshape(1, width)
    vec = pl.BlockSpec((1, width), lambda b, n: (0, 0))
    blk = pl.BlockSpec((nblk, HEAD_DIM, HEAD_DIM), lambda b, n: (0, 0, 0))
    return pl.pallas_call(
        functools.partial(_rglru_kernel, tt=tt, nblk=nblk),
        grid=(batch, n_t),
        in_specs=[pl.BlockSpec((tt, width), lambda b, n: (b * n_t + n, 4)),
                  pl.BlockSpec((tt, width), lambda b, n: (b * n_t + n, 5)),
                  pl.BlockSpec((4, width), lambda b, n: (0, 0)),
                  vec, blk, vec, blk, vec, vec],
        out_specs=pl.BlockSpec((tt, width), lambda b, n: (b * n_t + n, 0)),
        out_shape=jax.ShapeDtypeStruct((t, width), BF16),
        scratch_shapes=[pltpu.VMEM((tt + CONV_TAIL, width), F32), pltpu.VMEM((1, width), F32)],
        compiler_params=_cparams(2),
        name="rglru",
    )(proj, proj, conv_w, row(conv_b), w_a.astype(BF16), row(b_a), w_i.astype(BF16), row(b_i),
      row(lam))


FORGET_SPLIT = 3


def _bf16_head(x):
    bits = lax.bitcast_convert_type(x, jnp.uint32) & jnp.uint32(0xFFFF0000)
    return lax.bitcast_convert_type(bits, F32)


def _logf_cumsum_kernel(fl_ref, bf_ref, *o_refs, seq):
    z = fl_ref[...] + bf_ref[...]
    c = jnp.minimum(z, 0.0) - jnp.log1p(jnp.exp(-jnp.abs(z)))
    row = lax.broadcasted_iota(jnp.int32, c.shape, 0)
    d = 1
    while d < seq:
        c = c + jnp.where(row >= d, pltpu.roll(c, d, 0), 0.0)
        d *= 2
    x = c * (-(HEAD_DIM ** 0.5))
    for o_ref in o_refs:
        piece = _bf16_head(x)
        o_ref[...] = piece.astype(BF16)
        x = x - piece


def _logf_cumsum(fl, b_f, batch, seq):
    tile = pl.BlockSpec((seq, LANES), lambda b: (b, 0))
    return pl.pallas_call(
        functools.partial(_logf_cumsum_kernel, seq=seq),
        grid=(batch,),
        in_specs=[tile, pl.BlockSpec((1, LANES), lambda b: (0, 0))],
        out_specs=[tile] * FORGET_SPLIT,
        out_shape=[jax.ShapeDtypeStruct(fl.shape, BF16)] * FORGET_SPLIT,
        compiler_params=_cparams(1),
        name="logf_cumsum",
    )(fl, b_f)


def _fox_kernel(q_ref, k_ref, v_ref, kc_ref, o_ref, kext_ref, vext_ref, m_ref, l_ref, acc_ref, *,
                tq, seq, group):
    qi = pl.program_id(2)
    head = lambda g: slice(g * HEAD_DIM, (g + 1) * HEAD_DIM)

    @pl.when(qi == 0)
    def _():
        for g in range(group):
            kext_ref[g, :, :HEAD_DIM] = k_ref[:, head(g)]
            kext_ref[g, :, HEAD_DIM:] = kc_ref[g]
            vext_ref[g, :, :HEAD_DIM] = v_ref[:, head(g)]
            vext_ref[g, :, HEAD_DIM:] = jnp.ones((seq, HEAD_DIM), BF16)

    lane = lax.broadcasted_iota(jnp.int32, (tq, HEAD_DIM), 1)
    ones = jnp.where(lane < FORGET_SPLIT, 1.0, 0.0).astype(BF16)
    q2 = [jnp.concatenate([q_ref[:, head(g)], ones], axis=1) for g in range(group)]
    c = (HEAD_DIM ** -0.5) * math.log2(math.e)
    reps = tq // HEAD_DIM

    def logits(g, kb):
        off = pl.multiple_of(kb * tq, tq)
        return lax.dot_general(q2[g], kext_ref[g, pl.ds(off, tq), :], (((1,), (1,)), ((), ())),
                               preferred_element_type=F32)

    def fold(g, kb, masked):
        u = logits(g, kb)
        if masked:
            row = lax.broadcasted_iota(jnp.int32, u.shape, 0)
            col = lax.broadcasted_iota(jnp.int32, u.shape, 1)
            u = jnp.where(col <= row, u, -jnp.inf)
        m = m_ref[g]
        m_new = jnp.maximum(m, jnp.broadcast_to(jnp.max(u, axis=-1, keepdims=True), m.shape))
        p = jnp.exp2((u - jnp.concatenate([m_new] * reps, axis=1)) * c)
        alpha = jnp.exp2((m - m_new) * c)
        off = pl.multiple_of(kb * tq, tq)
        pv = _dot(p.astype(BF16), vext_ref[g, pl.ds(off, tq), :])
        m_ref[g] = m_new
        l_ref[g] = alpha * l_ref[g] + pv[:, HEAD_DIM:]
        acc_ref[g] = alpha * acc_ref[g] + pv[:, :HEAD_DIM]

    m_ref[...] = jnp.full(m_ref.shape, -jnp.inf, F32)
    l_ref[...] = jnp.zeros(l_ref.shape, F32)
    acc_ref[...] = jnp.zeros(acc_ref.shape, F32)

    def body(kb, carry):
        for g in range(group):
            fold(g, kb, False)
        return carry

    lax.fori_loop(0, qi, body, 0)
    for g in range(group):
        fold(g, qi, True)
        o_ref[:, head(g)] = (acc_ref[g] / l_ref[g]).astype(BF16)


def _fox_attention(qkv, kc, batch, seq, heads, tq, group=4):
    t = qkv.shape[0]
    tq = min(tq, max(HEAD_DIM, seq // 4))
    nq = seq // tq
    assert tq % HEAD_DIM == 0 and seq % tq == 0 and heads % group == 0
    hg = heads // group
    w = group * HEAD_DIM
    return pl.pallas_call(
        functools.partial(_fox_kernel, tq=tq, seq=seq, group=group),
        grid=(batch, hg, nq),
        in_specs=[
            pl.BlockSpec((tq, w), lambda b, h, qi: (b * nq + qi, h)),
            pl.BlockSpec((seq, w), lambda b, h, qi: (b, hg + h)),
            pl.BlockSpec((seq, w), lambda b, h, qi: (b, 2 * hg + h)),
            pl.BlockSpec((group, seq, HEAD_DIM), lambda b, h, qi: (b * hg + h, 0, 0)),
        ],
        out_specs=pl.BlockSpec((tq, w), lambda b, h, qi: (b * nq + qi, h)),
        out_shape=jax.ShapeDtypeStruct((t, heads * HEAD_DIM), BF16),
        scratch_shapes=[pltpu.VMEM((group, seq, 2 * HEAD_DIM), BF16),
                        pltpu.VMEM((group, seq, 2 * HEAD_DIM), BF16)]
        + [pltpu.VMEM((group, tq, HEAD_DIM), F32)] * 3,
        compiler_params=_cparams(3),
        name="fox_attention",
    )(qkv, qkv, qkv, kc)


def _forget_key_bias(pieces, batch, seq, heads):
    kc = jnp.stack([p.reshape(batch, seq, LANES)[:, :, :heads] for p in pieces], axis=-1)
    kc = jnp.transpose(kc, (0, 2, 1, 3)).reshape(batch * heads, seq, FORGET_SPLIT)
    return jnp.pad(kc, ((0, 0), (0, 0), (0, HEAD_DIM - FORGET_SPLIT)))


def _router_kernel(h_ref, rw_ref, rb_ref, idx_ref, wts_ref, rank_ref, cnt_ref, *, tm):
    @pl.when(pl.program_id(0) == 0)
    def _():
        cnt_ref[...] = jnp.zeros_like(cnt_ref)

    h = h_ref[...]
    h_hi = _bf16_head(h)
    first_order = _dot(h_hi.astype(BF16), rw_ref[...])
    logits = (first_order[:, :LANES] + first_order[:, LANES:]
              + _dot((h - h_hi).astype(BF16), rw_ref[:, :LANES]))
    scores = jax.nn.sigmoid(logits)
    sel = scores + rb_ref[...]
    lane = lax.broadcasted_iota(jnp.int32, sel.shape, 1)
    lane_f = lane.astype(F32)
    idx = jnp.zeros(sel.shape, F32)
    wts = jnp.zeros(sel.shape, F32)
    chosen = jnp.zeros(sel.shape, F32)
    hits = []
    for k in range(TOP_K):
        top = jnp.max(sel, axis=-1, keepdims=True)
        first = jnp.min(jnp.where(sel == top, lane_f, float(LANES)), axis=-1, keepdims=True)
        hit = lane_f == first
        hits.append(hit)
        idx = jnp.where(lane == k, first, idx)
        wts = jnp.where(lane == k, jnp.sum(jnp.where(hit, scores, 0.0), axis=-1, keepdims=True), wts)
        chosen = jnp.where(hit, 1.0, chosen)
        sel = jnp.where(hit, -jnp.inf, sel)
    wts = wts / jnp.sum(wts, axis=-1, keepdims=True) * ROUTED_SCALE

    r = lax.broadcasted_iota(jnp.int32, (tm, tm), 0)
    c = lax.broadcasted_iota(jnp.int32, (tm, tm), 1)
    before = jnp.where(c < r, 1.0, 0.0).astype(BF16)
    rank_all = _dot(before, chosen.astype(BF16)) + cnt_ref[...]
    rank = jnp.zeros(sel.shape, F32)
    for k in range(TOP_K):
        rank = jnp.where(lane == k, jnp.sum(jnp.where(hits[k], rank_all, 0.0), axis=-1, keepdims=True),
                         rank)
    cnt_ref[...] = cnt_ref[...] + jnp.sum(chosen, axis=0, keepdims=True)
    idx_ref[...] = idx.astype(jnp.int32)
    wts_ref[...] = wts
    rank_ref[...] = rank.astype(jnp.int32)


def _router(h, router_w, router_bias, tm=256):
    t, d = h.shape
    e = router_w.shape[1]
    tm = min(tm, t)
    rw = jnp.pad(router_w, ((0, 0), (0, LANES - e)))
    rw_hi = _bf16_head(rw)
    rw = jnp.concatenate([rw_hi.astype(BF16), (rw - rw_hi).astype(BF16)], axis=1)
    rb = jnp.pad(router_bias.astype(F32), (0, LANES - e), constant_values=-jnp.inf).reshape(1, LANES)
    tile = pl.BlockSpec((tm, LANES), lambda i: (i, 0))
    return pl.pallas_call(
        functools.partial(_router_kernel, tm=tm),
        grid=(t // tm,),
        in_specs=[pl.BlockSpec((tm, d), lambda i: (i, 0)),
                  pl.BlockSpec((d, 2 * LANES), lambda i: (0, 0)),
                  pl.BlockSpec((1, LANES), lambda i: (0, 0))],
        out_specs=[tile, tile, tile, pl.BlockSpec((1, LANES), lambda i: (0, 0))],
        out_shape=[jax.ShapeDtypeStruct((t, LANES), jnp.int32),
                   jax.ShapeDtypeStruct((t, LANES), F32),
                   jax.ShapeDtypeStruct((t, LANES), jnp.int32),
                   jax.ShapeDtypeStruct((1, LANES), F32)],
        compiler_params=_cparams(1),
        name="router_topk",
    )(h, rw, rb)


def _dispatch_kernel(pstart_ref, cnt_ref, dest_ref, x_ref, xs_ref, zero_ref, sem, *, tm, n_exp, nc,
                     n_blocks):
    i = pl.program_id(0)
    blk = EXPERT_BLOCK * nc

    def row_copy(r, dst):
        return pltpu.make_async_copy(x_ref.at[pl.ds(pl.multiple_of(r * nc, nc), nc), :],
                                     xs_ref.at[pl.ds(pl.multiple_of(dst * nc, nc), nc), :], sem)

    def zero_run(dst, n):
        return pltpu.make_async_copy(zero_ref.at[pl.ds(0, n * nc), :],
                                     xs_ref.at[pl.ds(pl.multiple_of(dst * nc, nc), n * nc), :], sem)

    def zero_block(b):
        return pltpu.make_async_copy(zero_ref, xs_ref.at[pl.ds(pl.multiple_of(b * blk, blk), blk), :],
                                     sem)

    def pad_rows(e, do):
        first = pstart_ref[e] + cnt_ref[e]
        n = pstart_ref[e + 1] - first
        bit = EXPERT_BLOCK // 2
        while bit:
            pl.when((n & bit) != 0)(functools.partial(do, first, bit))
            first = first + (n & bit)
            bit //= 2

    def first_unused_block():
        return lax.div(pstart_ref[n_exp], EXPERT_BLOCK)

    @pl.when(i == 0)
    def _():
        zero_ref[...] = jnp.zeros_like(zero_ref)

        def per_expert(e, carry):
            pad_rows(e, lambda first, n: zero_run(first, n).start())
            return carry

        lax.fori_loop(0, n_exp, per_expert, 0)
        lax.fori_loop(first_unused_block(), n_blocks, lambda b, c: (zero_block(b).start(), c)[1], 0)

    def issue(r, carry):
        for k in range(TOP_K):
            row_copy(r, dest_ref[0, 0, r * TOP_K + k]).start(priority=k % 2)
        return carry

    lax.fori_loop(0, tm, issue, 0)
    all_rows = xs_ref.at[pl.ds(0, tm * TOP_K * nc), :]
    pltpu.make_async_copy(all_rows, all_rows, sem).wait()

    @pl.when(i == 0)
    def _():
        def per_expert(e, carry):
            pad_rows(e, lambda first, n: zero_run(0, n).wait())
            return carry

        lax.fori_loop(0, n_exp, per_expert, 0)
        lax.fori_loop(first_unused_block(), n_blocks, lambda b, c: (zero_block(0).wait(), c)[1], 0)


def _dispatch(xp, dest, pstart, counts, n_rows, nc, tm=256):
    t = xp.shape[0] // nc
    tm = min(tm, t)
    n_exp = counts.shape[0]
    assert n_rows >= tm * TOP_K
    grid_spec = pltpu.PrefetchScalarGridSpec(
        num_scalar_prefetch=2,
        grid=(t // tm,),
        in_specs=[pl.BlockSpec((1, 1, tm * TOP_K), lambda i, ps, cn: (i, 0, 0),
                               memory_space=pltpu.SMEM),
                  pl.BlockSpec((tm * nc, LANES), lambda i, ps, cn: (i, 0))],
        out_specs=pl.BlockSpec(memory_space=pl.ANY),
        scratch_shapes=[pltpu.VMEM((EXPERT_BLOCK * nc, LANES), xp.dtype),
                        pltpu.SemaphoreType.DMA(())],
    )
    assert n_rows % EXPERT_BLOCK == 0
    return pl.pallas_call(
        functools.partial(_dispatch_kernel, tm=tm, n_exp=n_exp, nc=nc,
                          n_blocks=n_rows // EXPERT_BLOCK),
        grid_spec=grid_spec,
        out_shape=jax.ShapeDtypeStruct((n_rows * nc, LANES), xp.dtype),
        compiler_params=_cparams(1),
        name="moe_dispatch",
    )(pstart, counts, dest.reshape(t // tm, 1, tm * TOP_K), xp)


def _expert_ffn_kernel(be_ref, nb_ref, ord_ref, nxt_ref, nv_ref, x_ref, wg_hbm, wu_hbm, wd_hbm, o_ref,
                       wgf_ref, wuf_ref, wdf_ref, wgb_ref, wub_ref, wdb_ref, sems, *, nc, layer):
    i = pl.program_id(0)
    used = i < nb_ref[0]
    expert = be_ref[i]
    new_expert = jnp.logical_or(i == 0, expert != be_ref[jnp.maximum(i - 1, 0)])
    slot = ord_ref[i] % 2
    half = EXPERT_BLOCK // 2
    short = nv_ref[i] <= half

    def weight_copies(e, sl):
        return [pltpu.make_async_copy(src.at[layer, e], dst.at[sl], sems.at[sl])
                for src, dst in ((wg_hbm, wgf_ref), (wu_hbm, wuf_ref), (wd_hbm, wdf_ref))]

    @pl.when(i == 0)
    def _():
        for cp in weight_copies(expert, slot):
            cp.start()

    @pl.when(jnp.logical_and(used, new_expert))
    def _():
        for cp in weight_copies(expert, slot):
            cp.wait()
        for sl in range(2):
            @pl.when(slot == sl)
            def _(sl=sl):
                wgb_ref[...] = wgf_ref[sl].astype(BF16)
                wub_ref[...] = wuf_ref[sl].astype(BF16)
                wdb_ref[...] = wdf_ref[sl].astype(BF16)

        @pl.when(nxt_ref[i] >= 0)
        def _():
            for cp in weight_copies(nxt_ref[i], 1 - slot):
                cp.start()

    def ffn(rows):
        xt = jnp.swapaxes(x_ref[0:rows * nc, :].reshape(rows // nc, nc, nc, LANES), 1, 2)
        halves = [_unpack_pairs(xt[:, s].reshape(rows, LANES)) for s in range(nc)]
        x = jnp.concatenate([h[0].astype(BF16) for h in halves] + [h[1].astype(BF16) for h in halves],
                            axis=1)
        hb = (_silu(_dot(x, wgb_ref[...])) * _dot(x, wub_ref[...])).astype(BF16)
        _store_slabs(o_ref, _pack_pairs(_dot(hb, wdb_ref[...])), rows, nc)

    @pl.when(jnp.logical_and(used, jnp.logical_not(short)))
    def _():
        ffn(EXPERT_BLOCK)

    @pl.when(jnp.logical_and(used, short))
    def _():
        ffn(half)
        o_ref[half * nc:, :] = jnp.zeros((half * nc, LANES), o_ref.dtype)

    @pl.when(jnp.logical_not(used))
    def _():
        o_ref[...] = jnp.zeros_like(o_ref)


def _expert_ffn(xs, block_e, n_used, block_ord, block_next, n_valid, w_gate, w_up, w_down, layer, nc):
    _, _, d, f = w_gate.shape
    nb = xs.shape[0] // (EXPERT_BLOCK * nc)
    any_space = pl.BlockSpec(memory_space=pl.ANY)
    grid_spec = pltpu.PrefetchScalarGridSpec(
        num_scalar_prefetch=5,
        grid=(nb,),
        in_specs=[
            pl.BlockSpec((EXPERT_BLOCK * nc, LANES),
                         lambda i, be, nu, od, nx, nv: (jnp.minimum(i, nu[0] - 1), 0)),
            any_space, any_space, any_space,
        ],
        out_specs=pl.BlockSpec((EXPERT_BLOCK * nc, LANES), lambda i, be, nu, od, nx, nv: (i, 0)),
        scratch_shapes=[pltpu.VMEM((2, d, f), F32), pltpu.VMEM((2, d, f), F32),
                        pltpu.VMEM((2, f, d), F32),
                        pltpu.VMEM((d, f), BF16), pltpu.VMEM((d, f), BF16), pltpu.VMEM((f, d), BF16),
                        pltpu.SemaphoreType.DMA((2,))],
    )
    return pl.pallas_call(
        functools.partial(_expert_ffn_kernel, nc=nc, layer=layer),
        grid_spec=grid_spec,
        out_shape=jax.ShapeDtypeStruct(xs.shape, xs.dtype),
        compiler_params=_cparams(1),
        name="expert_ffn",
    )(block_e, n_used, block_ord, block_next, n_valid, xs, w_gate, w_up, w_down)


def _combine_kernel(dest_ref, destn_ref, wts_ref, hf_ref, hb_ref, sg_ref, su_ref, sd_ref, g_ref,
                    b_ref, ys_ref, of_ref, ob_ref, buf_ref, sems, *, tm, nc, alpha):
    i = pl.program_id(0)
    slot = i % 2

    def slab_copy(sl, k, r, src):
        dst_row = pl.multiple_of((k * tm + r) * nc, nc)
        return pltpu.make_async_copy(ys_ref.at[pl.ds(pl.multiple_of(src * nc, nc), nc), :],
                                     buf_ref.at[sl, pl.ds(dst_row, nc), :], sems.at[sl])

    def gather(sl, rows_ref):
        def issue(j, carry):
            for r in (2 * j, 2 * j + 1):
                for k in range(TOP_K):
                    slab_copy(sl, k, r, rows_ref[0, 0, r * TOP_K + k]).start(priority=k % 2)
            return carry

        lax.fori_loop(0, tm // 2, issue, 0)

    @pl.when(i == 0)
    def _():
        gather(slot, dest_ref)

    @pl.when(i + 1 < pl.num_programs(0))
    def _():
        gather(1 - slot, destn_ref)

    xb = hb_ref[...]
    shared = _dot((_silu(_dot(xb, sg_ref[...])) * _dot(xb, su_ref[...])).astype(BF16), sd_ref[...])
    z = alpha * hf_ref[...] + shared

    pltpu.make_async_copy(buf_ref.at[slot], buf_ref.at[slot], sems.at[slot]).wait()
    wts = wts_ref[...]
    his = [None] * nc
    los = [None] * nc
    for k in range(TOP_K):
        wk = wts[:, k:k + 1]
        for s in range(nc):
            hi, lo = _unpack_pairs(buf_ref[slot, pl.ds(k * tm * nc + s, tm, stride=nc), :])
            his[s] = wk * hi if k == 0 else his[s] + wk * hi
            los[s] = wk * lo if k == 0 else los[s] + wk * lo
    out = _layer_norm(z + jnp.concatenate(his + los, axis=1), g_ref[...], b_ref[...])
    of_ref[...] = out
    ob_ref[...] = out.astype(BF16)


def _combine(ys, dest, wts, h_f32, h_bf16, sh_gate, sh_up, sh_down, g, b, alpha, nc, tm=256):
    t, d = h_f32.shape
    f = sh_gate.shape[1]
    tm = min(tm, t)
    n = t // tm
    smem = lambda im: pl.BlockSpec((1, 1, tm * TOP_K), im, memory_space=pltpu.SMEM)
    tile = lambda w: pl.BlockSpec((tm, w), lambda i: (i, 0))
    const = lambda shape: pl.BlockSpec(shape, lambda i: (0, 0))
    rows = dest.reshape(n, 1, tm * TOP_K)
    return pl.pallas_call(
        functools.partial(_combine_kernel, tm=tm, nc=nc, alpha=alpha),
        grid=(n,),
        in_specs=[smem(lambda i: (i, 0, 0)), smem(lambda i: (jnp.minimum(i + 1, n - 1), 0, 0)),
                  tile(LANES), tile(d), tile(d),
                  const((d, f)), const((d, f)), const((f, d)), const((1, d)), const((1, d)),
                  pl.BlockSpec(memory_space=pl.ANY)],
        out_specs=[tile(d), tile(d)],
        out_shape=[jax.ShapeDtypeStruct((t, d), F32), jax.ShapeDtypeStruct((t, d), BF16)],
        scratch_shapes=[pltpu.VMEM((2, TOP_K * tm * nc, LANES), ys.dtype),
                        pltpu.SemaphoreType.DMA((2,))],
        compiler_params=_cparams(1),
        name="moe_combine",
    )(rows, rows, wts, h_f32, h_bf16, sh_gate, sh_up, sh_down, g.reshape(1, d), b.reshape(1, d), ys)


def _moe_ln(h_f32, h_bf16, h_slabs, router_w, router_bias, w_gate, w_up, w_down, layer, sh_gate,
            sh_up, sh_down, g, b, alpha):
    t, d = h_f32.shape
    n_exp = router_w.shape[1]
    nc = _slab_rows(d)
    idx, wts, rank, counts = _router(h_f32, router_w, router_bias)
    idx, rank = idx[:, :TOP_K], rank[:, :TOP_K]

    counts = counts[0, :n_exp].astype(jnp.int32)
    padded = (counts + EXPERT_BLOCK - 1) // EXPERT_BLOCK * EXPERT_BLOCK
    pends = jnp.cumsum(padded)
    pstart = jnp.concatenate([jnp.zeros((1,), jnp.int32), pends]).astype(jnp.int32)
    n_blocks = t * TOP_K // EXPERT_BLOCK + n_exp
    block_start = jnp.arange(n_blocks, dtype=jnp.int32) * EXPERT_BLOCK
    block_e = jnp.sum((pends[None, :] <= block_start[:, None]).astype(jnp.int32), axis=1)
    block_e = jnp.minimum(block_e, n_exp - 1)
    n_used = (pends[-1:] // EXPERT_BLOCK).astype(jnp.int32)
    experts = jnp.arange(n_exp, dtype=jnp.int32)
    owns = padded > 0
    order_e = jnp.cumsum(owns.astype(jnp.int32)) - 1
    later = jnp.logical_and(owns[None, :], experts[None, :] > experts[:, None])
    next_e = jnp.min(jnp.where(later, experts[None, :], n_exp), axis=1)
    next_e = jnp.where(next_e < n_exp, next_e, -1)
    of_block = (block_e[:, None] == experts[None, :]).astype(jnp.int32)
    block_ord = jnp.sum(of_block * order_e[None, :], axis=1)
    block_next = jnp.sum(of_block * next_e[None, :], axis=1)
    rows_left = counts[None, :] - (block_start[:, None] - pstart[None, :n_exp])
    n_valid = jnp.clip(jnp.sum(of_block * rows_left, axis=1), 0, EXPERT_BLOCK)
    group_start = jnp.sum(jnp.where(idx[:, :, None] == jnp.arange(n_exp, dtype=jnp.int32),
                                    pstart[:n_exp], 0), axis=-1)
    dest = group_start + rank

    xs = _dispatch(h_slabs, dest, pstart, counts, n_blocks * EXPERT_BLOCK, nc)
    ys = _expert_ffn(xs, block_e, n_used, block_ord, block_next, n_valid, w_gate, w_up, w_down, layer,
                     nc)
    return _combine(ys, dest, wts, h_f32, h_bf16, sh_gate.astype(BF16), sh_up.astype(BF16),
                    sh_down.astype(BF16), g, b, alpha, nc)


def kernel(x, even_w_in, ret_gn_g, rg_conv_w, rg_conv_b, rg_w_a, rg_b_a, rg_w_i, rg_b_i, rg_lambda,
           even_w_out, fox_w_in, fox_b_f, fox_w_out, ln_g, ln_b, router_w, router_bias, exp_w_gate,
           exp_w_up, exp_w_down, sh_w_gate, sh_w_up, sh_w_down):
    batch, seq, d = x.shape
    depth = ln_g.shape[0]
    alpha = (2 * depth) ** 0.25
    ret_width = ret_gn_g.shape[1]
    fox_heads = fox_b_f.shape[1]

    h_f32 = x.reshape(batch * seq, d)
    h_bf16 = h_f32
    for layer in range(depth):
        j = layer // 2
        if layer % 2 == 0:
            proj = _matmul(h_bf16, even_w_in[j].astype(BF16), BF16)
            y_ret = _retention(proj, ret_gn_g[j], batch, seq, chunk=256)
            y_rnn = _rglru(proj, rg_conv_w[j], rg_conv_b[j], rg_w_a[j], rg_b_a[j], rg_w_i[j],
                           rg_b_i[j], rg_lambda[j], batch, seq, tt=256)
            w_out = even_w_out[j].astype(BF16)
            ys, ws = [y_ret, y_rnn], [w_out[:ret_width], w_out[ret_width:]]
        else:
            w_in = fox_w_in[j]
            qkv = _matmul(h_bf16, w_in[:, :3 * d].astype(BF16), BF16)
            w_f = jnp.pad(w_in[:, 3 * d:], ((0, 0), (0, LANES - fox_heads))).astype(BF16)
            fl = _matmul(h_bf16, w_f, F32)
            b_f = jnp.pad(fox_b_f[j], (0, LANES - fox_heads)).reshape(1, LANES)
            kc = _forget_key_bias(_logf_cumsum(fl, b_f, batch, seq), batch, seq, fox_heads)
            o = _fox_attention(qkv, kc, batch, seq, fox_heads, tq=512)
            ys, ws = [o], [fox_w_out[j].astype(BF16)]
        h_f32, h_bf16, h_slabs = _proj_ln(ys, ws, h_f32, ln_g[layer, 0], ln_b[layer, 0], alpha)
        h_f32, h_bf16 = _moe_ln(h_f32, h_bf16, h_slabs, router_w[layer], router_bias[layer],
                                exp_w_gate, exp_w_up, exp_w_down, layer,
                                sh_w_gate[layer], sh_w_up[layer], sh_w_down[layer],
                                ln_g[layer, 1], ln_b[layer, 1], alpha)
    return h_f32.reshape(batch, seq, d)
```

```python
import functools
import math

import jax
import jax.numpy as jnp
from jax import lax
from jax.experimental import pallas as pl
from jax.experimental.pallas import tpu as pltpu

F32 = jnp.float32
BF16 = jnp.bfloat16

HEAD_DIM = 128
TOP_K = 8
ROUTED_SCALE = 2.5
ROPE_BASE = 10000.0
RG_C = 8.0
LN_EPS = 1e-5
CONV_TAIL = 8

V7X_VMEM_BYTES = 64 * 1024 * 1024
VMEM_LIMIT = V7X_VMEM_BYTES - 8 * 1024 * 1024
LANES = 128
SUBLANES = 8

EXPERT_BLOCK = 512


def _cparams(grid_rank):
    return pltpu.CompilerParams(dimension_semantics=("arbitrary",) * grid_rank,
                                vmem_limit_bytes=VMEM_LIMIT)


def _dot(a, b):
    return jnp.dot(a, b, preferred_element_type=F32)


def _silu(x):
    return x * jax.nn.sigmoid(x)


def _softplus(x):
    return jnp.maximum(x, 0.0) + jnp.log1p(jnp.exp(-jnp.abs(x)))


def _layer_norm(z, g, b):
    mu = jnp.mean(z, axis=-1, keepdims=True)
    zc = z - mu
    var = jnp.mean(zc * zc, axis=-1, keepdims=True)
    return zc * lax.rsqrt(var + LN_EPS) * g + b


def _mm_kernel(x_ref, w_ref, o_ref):
    o_ref[...] = _dot(x_ref[...].astype(BF16), w_ref[...]).astype(o_ref.dtype)


def _tile(n, target):
    best = LANES
    for c in range(LANES, min(n, target) + 1, LANES):
        if n % c == 0:
            best = c
    assert n % best == 0
    return best


def _matmul(x, w, out_dtype, tm=1024, tn=2048):
    m, k = x.shape
    n = w.shape[1]
    tm, tn = _tile(m, tm), _tile(n, tn)
    return pl.pallas_call(
        _mm_kernel,
        grid=(m // tm, n // tn),
        in_specs=[pl.BlockSpec((tm, k), lambda i, j: (i, 0)),
                  pl.BlockSpec((k, tn), lambda i, j: (0, j))],
        out_specs=pl.BlockSpec((tm, tn), lambda i, j: (i, j)),
        out_shape=jax.ShapeDtypeStruct((m, n), out_dtype),
        compiler_params=_cparams(2),
        name="in_proj",
    )(x, w)


def _slab_rows(d):
    assert d % (2 * LANES) == 0
    return d // (2 * LANES)


def _pack_pairs(x):
    half = x.shape[1] // 2
    hi = lax.bitcast_convert_type(x[:, :half].astype(BF16).astype(F32), jnp.uint32)
    lo = lax.bitcast_convert_type(x[:, half:].astype(BF16).astype(F32), jnp.uint32)
    return hi | (lo >> 16)


def _unpack_pairs(w):
    hi = lax.bitcast_convert_type(w & jnp.uint32(0xFFFF0000), F32)
    lo = lax.bitcast_convert_type(w << 16, F32)
    return hi, lo


def _store_slabs(ref, words, rows, nc):
    for s in range(nc):
        ref[pl.ds(s, rows, stride=nc), :] = words[:, s * LANES:(s + 1) * LANES]


def _proj_ln_kernel(*refs, n_in, alpha, tm, nc):
    ys, ws = refs[:n_in], refs[n_in:2 * n_in]
    h_ref, g_ref, b_ref, of_ref, ob_ref, op_ref = refs[2 * n_in:]
    acc = _dot(ys[0][...], ws[0][...])
    for y_ref, w_ref in zip(ys[1:], ws[1:]):
        acc = acc + _dot(y_ref[...], w_ref[...])
    out = _layer_norm(alpha * h_ref[...] + acc, g_ref[...], b_ref[...])
    of_ref[...] = out
    ob_ref[...] = out.astype(BF16)
    _store_slabs(op_ref, _pack_pairs(out), tm, nc)


def _proj_ln(ys, ws, h, g, b, alpha, tm=512):
    t, d = h.shape
    tm = min(tm, t)
    n_in = len(ys)
    nc = _slab_rows(d)
    in_specs = ([pl.BlockSpec((tm, y.shape[1]), lambda i: (i, 0)) for y in ys]
                + [pl.BlockSpec(w.shape, lambda i: (0, 0), pipeline_mode=pl.Buffered(1)) for w in ws]
                + [pl.BlockSpec((tm, d), lambda i: (i, 0)),
                   pl.BlockSpec((1, d), lambda i: (0, 0)),
                   pl.BlockSpec((1, d), lambda i: (0, 0))])
    return pl.pallas_call(
        functools.partial(_proj_ln_kernel, n_in=n_in, alpha=alpha, tm=tm, nc=nc),
        grid=(t // tm,),
        in_specs=in_specs,
        out_specs=[pl.BlockSpec((tm, d), lambda i: (i, 0)),
                   pl.BlockSpec((tm, d), lambda i: (i, 0)),
                   pl.BlockSpec((tm * nc, LANES), lambda i: (i, 0))],
        out_shape=[jax.ShapeDtypeStruct((t, d), F32), jax.ShapeDtypeStruct((t, d), BF16),
                   jax.ShapeDtypeStruct((t * nc, LANES), jnp.uint32)],
        compiler_params=_cparams(1),
        name="out_proj_ln",
    )(*ys, *ws, h, g.reshape(1, d), b.reshape(1, d))


def _retention_kernel(q_ref, k_ref, v_ref, g_ref, cos_ref, sin_ref, intra_ref, qdec_ref, kdec_ref,
                      gn_ref, o_ref, state_ref, *, heads, chunk_decay):
    @pl.when(pl.program_id(1) == 0)
    def _():
        state_ref[...] = jnp.zeros_like(state_ref)

    cos, sin = cos_ref[...], sin_ref[...]
    half = HEAD_DIM // 2
    for h in range(heads):
        sl = slice(h * HEAD_DIM, (h + 1) * HEAD_DIM)
        q = q_ref[:, sl].astype(F32)
        k = k_ref[:, sl].astype(F32)
        q = q * cos + pltpu.roll(q, half, 1) * sin
        k = (k * cos + pltpu.roll(k, half, 1) * sin) * (HEAD_DIM ** -0.5)
        v = v_ref[:, sl]
        s = lax.dot_general(q.astype(BF16), k.astype(BF16), (((1,), (1,)), ((), ())),
                            preferred_element_type=F32) * intra_ref[h]
        state = state_ref[h]
        y = _dot(s.astype(BF16), v) + _dot((q * qdec_ref[h]).astype(BF16), state.astype(BF16))
        kv = lax.dot_general((k * kdec_ref[h]).astype(BF16), v, (((0,), (0,)), ((), ())),
                             preferred_element_type=F32)
        state_ref[h] = state * chunk_decay[h] + kv
        mu = jnp.mean(y, axis=-1, keepdims=True)
        yc = y - mu
        var = jnp.mean(yc * yc, axis=-1, keepdims=True)
        yn = yc * lax.rsqrt(var + LN_EPS) * gn_ref[:, sl]
        o_ref[:, sl] = (yn * _silu(g_ref[:, sl].astype(F32))).astype(BF16)


def _retention(proj, gn_g, batch, seq, chunk):
    t = proj.shape[0]
    width = gn_g.shape[0]
    heads = width // HEAD_DIM
    chunk = min(chunk, seq)
    n_chunks = seq // chunk
    half = HEAD_DIM // 2

    pos = jnp.arange(seq, dtype=F32)
    inv = ROPE_BASE ** (-jnp.arange(half, dtype=F32) / half)
    ang = pos[:, None] * inv[None, :]
    cos2 = jnp.concatenate([jnp.cos(ang), jnp.cos(ang)], axis=1)
    sin2 = jnp.concatenate([-jnp.sin(ang), jnp.sin(ang)], axis=1)

    log_g = jnp.log1p(-jnp.exp2(-5.0 - jnp.arange(heads, dtype=F32)))
    i = jnp.arange(chunk, dtype=F32)
    rel = i[:, None] - i[None, :]
    intra = jnp.where(rel >= 0, jnp.exp(log_g[:, None, None] * jnp.maximum(rel, 0.0)), 0.0)
    kdec = jnp.exp(log_g[:, None] * (chunk - 1.0 - i)[None, :])
    qdec = jnp.exp(log_g[:, None] * (i + 1.0)[None, :])
    kdec = jnp.broadcast_to(kdec[:, :, None], (heads, chunk, HEAD_DIM))
    qdec = jnp.broadcast_to(qdec[:, :, None], (heads, chunk, HEAD_DIM))
    chunk_decay = tuple(math.exp(math.log1p(-2.0 ** (-5.0 - h)) * chunk) for h in range(heads))

    def col(c):
        return pl.BlockSpec((chunk, width), lambda b, n: (b * n_chunks + n, c))

    const3 = lambda shape: pl.BlockSpec(shape, lambda b, n: (0, 0, 0))
    return pl.pallas_call(
        functools.partial(_retention_kernel, heads=heads, chunk_decay=chunk_decay),
        grid=(batch, n_chunks),
        in_specs=[col(0), col(1), col(2), col(3),
                  pl.BlockSpec((chunk, HEAD_DIM), lambda b, n: (n, 0)),
                  pl.BlockSpec((chunk, HEAD_DIM), lambda b, n: (n, 0)),
                  const3((heads, chunk, chunk)),
                  const3((heads, chunk, HEAD_DIM)),
                  const3((heads, chunk, HEAD_DIM)),
                  pl.BlockSpec((1, width), lambda b, n: (0, 0))],
        out_specs=pl.BlockSpec((chunk, width), lambda b, n: (b * n_chunks + n, 0)),
        out_shape=jax.ShapeDtypeStruct((t, width), BF16),
        scratch_shapes=[pltpu.VMEM((heads, HEAD_DIM, HEAD_DIM), F32)],
        compiler_params=_cparams(2),
        name="retention",
    )(proj, proj, proj, proj, cos2, sin2, intra, qdec, kdec, gn_g.reshape(1, width))


def _rglru_kernel(u_ref, gate_ref, cw_ref, cb_ref, wa_ref, ba_ref, wi_ref, bi_ref, lam_ref,
                  o_ref, uext_ref, hc_ref, *, tt, nblk):
    n = pl.program_id(1)

    @pl.when(n == 0)
    def _():
        uext_ref[0:CONV_TAIL, :] = jnp.zeros((CONV_TAIL, uext_ref.shape[1]), F32)
        hc_ref[...] = jnp.zeros_like(hc_ref)

    @pl.when(n > 0)
    def _():
        uext_ref[0:CONV_TAIL, :] = uext_ref[tt:tt + CONV_TAIL, :]

    u = u_ref[...].astype(F32)
    uext_ref[CONV_TAIL:CONV_TAIL + tt, :] = u
    uc = (cb_ref[...] + cw_ref[3:4, :] * u
          + cw_ref[2:3, :] * uext_ref[CONV_TAIL - 1:CONV_TAIL - 1 + tt, :]
          + cw_ref[1:2, :] * uext_ref[CONV_TAIL - 2:CONV_TAIL - 2 + tt, :]
          + cw_ref[0:1, :] * uext_ref[CONV_TAIL - 3:CONV_TAIL - 3 + tt, :])
    ucb = uc.astype(BF16)
    ra = jnp.concatenate(
        [_dot(ucb[:, i * HEAD_DIM:(i + 1) * HEAD_DIM], wa_ref[i]) for i in range(nblk)], axis=1)
    ia = jnp.concatenate(
        [_dot(ucb[:, i * HEAD_DIM:(i + 1) * HEAD_DIM], wi_ref[i]) for i in range(nblk)], axis=1)
    r = jax.nn.sigmoid(ra + ba_ref[...])
    ig = jax.nn.sigmoid(ia + bi_ref[...])
    log_a = (-RG_C) * r * _softplus(-lam_ref[...])
    a = jnp.exp(log_a)
    bt = jnp.sqrt(-jnp.tanh(log_a) * (a * a + 1.0)) * (ig * uc)

    row = lax.broadcasted_iota(jnp.int32, a.shape, 0) & (SUBLANES - 1)
    d = 1
    while d < SUBLANES:
        keep = row >= d
        bt = jnp.where(keep, a * pltpu.roll(bt, d, 0) + bt, bt)
        a = jnp.where(keep, a * pltpu.roll(a, d, 0), a)
        d *= 2
    carry = hc_ref[...]
    groups = []
    for g in range(tt // SUBLANES):
        rows = slice(g * SUBLANES, (g + 1) * SUBLANES)
        hg = a[rows] * carry + bt[rows]
        groups.append(hg)
        carry = hg[SUBLANES - 1:SUBLANES, :]
    h = jnp.concatenate(groups, axis=0)
    hc_ref[...] = carry

    g = gate_ref[...].astype(F32)
    gelu = 0.5 * g * (1.0 + jnp.tanh(math.sqrt(2.0 / math.pi) * (g + 0.044715 * (g * g * g))))
    o_ref[...] = (h * gelu).astype(BF16)


def _rglru(proj, conv_w, conv_b, w_a, b_a, w_i, b_i, lam, batch, seq, tt):
    t = proj.shape[0]
    width = conv_w.shape[1]
    nblk = w_a.shape[0]
    tt = min(tt, seq)
    n_t = seq // tt
    row = lambda x: x.reshape(1, width)
    vec = pl.BlockSpec((1, width), lambda b, n: (0, 0))
    blk = pl.BlockSpec((nblk, HEAD_DIM, HEAD_DIM), lambda b, n: (0, 0, 0))
    return pl.pallas_call(
        functools.partial(_rglru_kernel, tt=tt, nblk=nblk),
        grid=(batch, n_t),
        in_specs=[pl.BlockSpec((tt, width), lambda b, n: (b * n_t + n, 4)),
                  pl.BlockSpec((tt, width), lambda b, n: (b * n_t + n, 5)),
                  pl.BlockSpec((4, width), lambda b, n: (0, 0)),
                  vec, blk, vec, blk, vec, vec],
        out_specs=pl.BlockSpec((tt, width), lambda b, n: (b * n_t + n, 0)),
        out_shape=jax.ShapeDtypeStruct((t, width), BF16),
        scratch_shapes=[pltpu.VMEM((tt + CONV_TAIL, width), F32), pltpu.VMEM((1, width), F32)],
        compiler_params=_cparams(2),
        name="rglru",
    )(proj, proj, conv_w, row(conv_b), w_a.astype(BF16), row(b_a), w_i.astype(BF16), row(b_i),
      row(lam))


FORGET_SPLIT = 3


def _bf16_head(x):
    bits = lax.bitcast_convert_type(x, jnp.uint32) & jnp.uint32(0xFFFF0000)
    return lax.bitcast_convert_type(bits, F32)


def _logf_cumsum_kernel(fl_ref, bf_ref, *o_refs, seq):
    z = fl_ref[...] + bf_ref[...]
    c = jnp.minimum(z, 0.0) - jnp.log1p(jnp.exp(-jnp.abs(z)))
    row = lax.broadcasted_iota(jnp.int32, c.shape, 0)
    d = 1
    while d < seq:
        c = c + jnp.where(row >= d, pltpu.roll(c, d, 0), 0.0)
        d *= 2
    x = c * (-(HEAD_DIM ** 0.5))
    for o_ref in o_refs:
        piece = _bf16_head(x)
        o_ref[...] = piece.astype(BF16)
        x = x - piece


def _logf_cumsum(fl, b_f, batch, seq):
    tile = pl.BlockSpec((seq, LANES), lambda b: (b, 0))
    return pl.pallas_call(
        functools.partial(_logf_cumsum_kernel, seq=seq),
        grid=(batch,),
        in_specs=[tile, pl.BlockSpec((1, LANES), lambda b: (0, 0))],
        out_specs=[tile] * FORGET_SPLIT,
        out_shape=[jax.ShapeDtypeStruct(fl.shape, BF16)] * FORGET_SPLIT,
        compiler_params=_cparams(1),
        name="logf_cumsum",
    )(fl, b_f)


def _fox_kernel(q_ref, k_ref, v_ref, kc_ref, o_ref, kext_ref, vext_ref, m_ref, l_ref, acc_ref, *,
                tq, seq, group):
    qi = pl.program_id(2)
    head = lambda g: slice(g * HEAD_DIM, (g + 1) * HEAD_DIM)

    @pl.when(qi == 0)
    def _():
        for g in range(group):
            kext_ref[g, :, :HEAD_DIM] = k_ref[:, head(g)]
            kext_ref[g, :, HEAD_DIM:] = kc_ref[g]
            vext_ref[g, :, :HEAD_DIM] = v_ref[:, head(g)]
            vext_ref[g, :, HEAD_DIM:] = jnp.ones((seq, HEAD_DIM), BF16)

    lane = lax.broadcasted_iota(jnp.int32, (tq, HEAD_DIM), 1)
    ones = jnp.where(lane < FORGET_SPLIT, 1.0, 0.0).astype(BF16)
    q2 = [jnp.concatenate([q_ref[:, head(g)], ones], axis=1) for g in range(group)]
    c = (HEAD_DIM ** -0.5) * math.log2(math.e)
    reps = tq // HEAD_DIM

    def logits(g, kb):
        off = pl.multiple_of(kb * tq, tq)
        return lax.dot_general(q2[g], kext_ref[g, pl.ds(off, tq), :], (((1,), (1,)), ((), ())),
                               preferred_element_type=F32)

    def fold(g, kb, masked):
        u = logits(g, kb)
        if masked:
            row = lax.broadcasted_iota(jnp.int32, u.shape, 0)
            col = lax.broadcasted_iota(jnp.int32, u.shape, 1)
            u = jnp.where(col <= row, u, -jnp.inf)
        m = m_ref[g]
        m_new = jnp.maximum(m, jnp.broadcast_to(jnp.max(u, axis=-1, keepdims=True), m.shape))
        p = jnp.exp2((u - jnp.concatenate([m_new] * reps, axis=1)) * c)
        alpha = jnp.exp2((m - m_new) * c)
        off = pl.multiple_of(kb * tq, tq)
        pv = _dot(p.astype(BF16), vext_ref[g, pl.ds(off, tq), :])
        m_ref[g] = m_new
        l_ref[g] = alpha * l_ref[g] + pv[:, HEAD_DIM:]
        acc_ref[g] = alpha * acc_ref[g] + pv[:, :HEAD_DIM]

    m_ref[...] = jnp.full(m_ref.shape, -jnp.inf, F32)
    l_ref[...] = jnp.zeros(l_ref.shape, F32)
    acc_ref[...] = jnp.zeros(acc_ref.shape, F32)

    def body(kb, carry):
        for g in range(group):
            fold(g, kb, False)
        return carry

    lax.fori_loop(0, qi, body, 0)
    for g in range(group):
        fold(g, qi, True)
        o_ref[:, head(g)] = (acc_ref[g] / l_ref[g]).astype(BF16)


def _fox_attention(qkv, kc, batch, seq, heads, tq, group=4):
    t = qkv.shape[0]
    tq = min(tq, max(HEAD_DIM, seq // 4))
    nq = seq // tq
    assert tq % HEAD_DIM == 0 and seq % tq == 0 and heads % group == 0
    hg = heads // group
    w = group * HEAD_DIM
    return pl.pallas_call(
        functools.partial(_fox_kernel, tq=tq, seq=seq, group=group),
        grid=(batch, hg, nq),
        in_specs=[
            pl.BlockSpec((tq, w), lambda b, h, qi: (b * nq + qi, h)),
            pl.BlockSpec((seq, w), lambda b, h, qi: (b, hg + h)),
            pl.BlockSpec((seq, w), lambda b, h, qi: (b, 2 * hg + h)),
            pl.BlockSpec((group, seq, HEAD_DIM), lambda b, h, qi: (b * hg + h, 0, 0)),
        ],
        out_specs=pl.BlockSpec((tq, w), lambda b, h, qi: (b * nq + qi, h)),
        out_shape=jax.ShapeDtypeStruct((t, heads * HEAD_DIM), BF16),
        scratch_shapes=[pltpu.VMEM((group, seq, 2 * HEAD_DIM), BF16),
                        pltpu.VMEM((group, seq, 2 * HEAD_DIM), BF16)]
        + [pltpu.VMEM((group, tq, HEAD_DIM), F32)] * 3,
        compiler_params=_cparams(3),
        name="fox_attention",
    )(qkv, qkv, qkv, kc)


def _forget_key_bias(pieces, batch, seq, heads):
    kc = jnp.stack([p.reshape(batch, seq, LANES)[:, :, :heads] for p in pieces], axis=-1)
    kc = jnp.transpose(kc, (0, 2, 1, 3)).reshape(batch * heads, seq, FORGET_SPLIT)
    return jnp.pad(kc, ((0, 0), (0, 0), (0, HEAD_DIM - FORGET_SPLIT)))


def _router_kernel(h_ref, rw_ref, rb_ref, idx_ref, wts_ref, rank_ref, cnt_ref, *, tm):
    @pl.when(pl.program_id(0) == 0)
    def _():
        cnt_ref[...] = jnp.zeros_like(cnt_ref)

    h = h_ref[...]
    h_hi = _bf16_head(h)
    first_order = _dot(h_hi.astype(BF16), rw_ref[...])
    logits = (first_order[:, :LANES] + first_order[:, LANES:]
              + _dot((h - h_hi).astype(BF16), rw_ref[:, :LANES]))
    scores = jax.nn.sigmoid(logits)
    sel = scores + rb_ref[...]
    lane = lax.broadcasted_iota(jnp.int32, sel.shape, 1)
    lane_f = lane.astype(F32)
    idx = jnp.zeros(sel.shape, F32)
    wts = jnp.zeros(sel.shape, F32)
    chosen = jnp.zeros(sel.shape, F32)
    hits = []
    for k in range(TOP_K):
        top = jnp.max(sel, axis=-1, keepdims=True)
        first = jnp.min(jnp.where(sel == top, lane_f, float(LANES)), axis=-1, keepdims=True)
        hit = lane_f == first
        hits.append(hit)
        idx = jnp.where(lane == k, first, idx)
        wts = jnp.where(lane == k, jnp.sum(jnp.where(hit, scores, 0.0), axis=-1, keepdims=True), wts)
        chosen = jnp.where(hit, 1.0, chosen)
        sel = jnp.where(hit, -jnp.inf, sel)
    wts = wts / jnp.sum(wts, axis=-1, keepdims=True) * ROUTED_SCALE

    r = lax.broadcasted_iota(jnp.int32, (tm, tm), 0)
    c = lax.broadcasted_iota(jnp.int32, (tm, tm), 1)
    before = jnp.where(c < r, 1.0, 0.0).astype(BF16)
    rank_all = _dot(before, chosen.astype(BF16)) + cnt_ref[...]
    rank = jnp.zeros(sel.shape, F32)
    for k in range(TOP_K):
        rank = jnp.where(lane == k, jnp.sum(jnp.where(hits[k], rank_all, 0.0), axis=-1, keepdims=True),
                         rank)
    cnt_ref[...] = cnt_ref[...] + jnp.sum(chosen, axis=0, keepdims=True)
    idx_ref[...] = idx.astype(jnp.int32)
    wts_ref[...] = wts
    rank_ref[...] = rank.astype(jnp.int32)


def _router(h, router_w, router_bias, tm=512):
    t, d = h.shape
    e = router_w.shape[1]
    tm = min(tm, t)
    rw = jnp.pad(router_w, ((0, 0), (0, LANES - e)))
    rw_hi = _bf16_head(rw)
    rw = jnp.concatenate([rw_hi.astype(BF16), (rw - rw_hi).astype(BF16)], axis=1)
    rb = jnp.pad(router_bias.astype(F32), (0, LANES - e), constant_values=-jnp.inf).reshape(1, LANES)
    tile = pl.BlockSpec((tm, LANES), lambda i: (i, 0))
    return pl.pallas_call(
        functools.partial(_router_kernel, tm=tm),
        grid=(t // tm,),
        in_specs=[pl.BlockSpec((tm, d), lambda i: (i, 0)),
                  pl.BlockSpec((d, 2 * LANES), lambda i: (0, 0)),
                  pl.BlockSpec((1, LANES), lambda i: (0, 0))],
        out_specs=[tile, tile, tile, pl.BlockSpec((1, LANES), lambda i: (0, 0))],
        out_shape=[jax.ShapeDtypeStruct((t, LANES), jnp.int32),
                   jax.ShapeDtypeStruct((t, LANES), F32),
                   jax.ShapeDtypeStruct((t, LANES), jnp.int32),
                   jax.ShapeDtypeStruct((1, LANES), F32)],
        compiler_params=_cparams(1),
        name="router_topk",
    )(h, rw, rb)


def _dispatch_kernel(pstart_ref, cnt_ref, dest_ref, x_ref, xs_ref, zero_ref, sem, *, tm, n_exp, nc,
                     n_blocks):
    i = pl.program_id(0)
    blk = EXPERT_BLOCK * nc

    def row_copy(r, dst):
        return pltpu.make_async_copy(x_ref.at[pl.ds(pl.multiple_of(r * nc, nc), nc), :],
                                     xs_ref.at[pl.ds(pl.multiple_of(dst * nc, nc), nc), :], sem)

    def zero_run(dst, n):
        return pltpu.make_async_copy(zero_ref.at[pl.ds(0, n * nc), :],
                                     xs_ref.at[pl.ds(pl.multiple_of(dst * nc, nc), n * nc), :], sem)

    def zero_block(b):
        return pltpu.make_async_copy(zero_ref, xs_ref.at[pl.ds(pl.multiple_of(b * blk, blk), blk), :],
                                     sem)

    def pad_rows(e, do):
        first = pstart_ref[e] + cnt_ref[e]
        n = pstart_ref[e + 1] - first
        bit = EXPERT_BLOCK // 2
        while bit:
            pl.when((n & bit) != 0)(functools.partial(do, first, bit))
            first = first + (n & bit)
            bit //= 2

    def first_unused_block():
        return lax.div(pstart_ref[n_exp], EXPERT_BLOCK)

    @pl.when(i == 0)
    def _():
        zero_ref[...] = jnp.zeros_like(zero_ref)

        def per_expert(e, carry):
            pad_rows(e, lambda first, n: zero_run(first, n).start())
            return carry

        lax.fori_loop(0, n_exp, per_expert, 0)
        lax.fori_loop(first_unused_block(), n_blocks, lambda b, c: (zero_block(b).start(), c)[1], 0)

    def issue(r, carry):
        for k in range(TOP_K):
            row_copy(r, dest_ref[0, 0, r * TOP_K + k]).start(priority=k % 2)
        return carry

    lax.fori_loop(0, tm, issue, 0)
    all_rows = xs_ref.at[pl.ds(0, tm * TOP_K * nc), :]
    pltpu.make_async_copy(all_rows, all_rows, sem).wait()

    @pl.when(i == 0)
    def _():
        def per_expert(e, carry):
            pad_rows(e, lambda first, n: zero_run(0, n).wait())
            return carry

        lax.fori_loop(0, n_exp, per_expert, 0)
        lax.fori_loop(first_unused_block(), n_blocks, lambda b, c: (zero_block(0).wait(), c)[1], 0)


def _dispatch(xp, dest, pstart, counts, n_rows, nc, tm=256):
    t = xp.shape[0] // nc
    tm = min(tm, t)
    n_exp = counts.shape[0]
    assert n_rows >= tm * TOP_K
    grid_spec = pltpu.PrefetchScalarGridSpec(
        num_scalar_prefetch=2,
        grid=(t // tm,),
        in_specs=[pl.BlockSpec((1, 1, tm * TOP_K), lambda i, ps, cn: (i, 0, 0),
                               memory_space=pltpu.SMEM),
                  pl.BlockSpec((tm * nc, LANES), lambda i, ps, cn: (i, 0))],
        out_specs=pl.BlockSpec(memory_space=pl.ANY),
        scratch_shapes=[pltpu.VMEM((EXPERT_BLOCK * nc, LANES), xp.dtype),
                        pltpu.SemaphoreType.DMA(())],
    )
    assert n_rows % EXPERT_BLOCK == 0
    return pl.pallas_call(
        functools.partial(_dispatch_kernel, tm=tm, n_exp=n_exp, nc=nc,
                          n_blocks=n_rows // EXPERT_BLOCK),
        grid_spec=grid_spec,
        out_shape=jax.ShapeDtypeStruct((n_rows * nc, LANES), xp.dtype),
        compiler_params=_cparams(1),
        name="moe_dispatch",
    )(pstart, counts, dest.reshape(t // tm, 1, tm * TOP_K), xp)


def _expert_ffn_kernel(be_ref, nb_ref, ord_ref, nxt_ref, nv_ref, x_ref, wg_hbm, wu_hbm, wd_hbm, o_ref,
                       wgf_ref, wuf_ref, wdf_ref, wgb_ref, wub_ref, wdb_ref, sems, *, nc, layer):
    i = pl.program_id(0)
    used = i < nb_ref[0]
    expert = be_ref[i]
    new_expert = jnp.logical_or(i == 0, expert != be_ref[jnp.maximum(i - 1, 0)])
    slot = ord_ref[i] % 2
    half = EXPERT_BLOCK // 2
    short = nv_ref[i] <= half

    def weight_copies(e, sl):
        return [pltpu.make_async_copy(src.at[layer, e], dst.at[sl], sems.at[sl])
                for src, dst in ((wg_hbm, wgf_ref), (wu_hbm, wuf_ref), (wd_hbm, wdf_ref))]

    @pl.when(i == 0)
    def _():
        for cp in weight_copies(expert, slot):
            cp.start()

    @pl.when(jnp.logical_and(used, new_expert))
    def _():
        for cp in weight_copies(expert, slot):
            cp.wait()
        for sl in range(2):
            @pl.when(slot == sl)
            def _(sl=sl):
                wgb_ref[...] = wgf_ref[sl].astype(BF16)
                wub_ref[...] = wuf_ref[sl].astype(BF16)
                wdb_ref[...] = wdf_ref[sl].astype(BF16)

        @pl.when(nxt_ref[i] >= 0)
        def _():
            for cp in weight_copies(nxt_ref[i], 1 - slot):
                cp.start()

    def ffn(rows):
        xt = jnp.swapaxes(x_ref[0:rows * nc, :].reshape(rows // nc, nc, nc, LANES), 1, 2)
        halves = [_unpack_pairs(xt[:, s].reshape(rows, LANES)) for s in range(nc)]
        x = jnp.concatenate([h[0].astype(BF16) for h in halves] + [h[1].astype(BF16) for h in halves],
                            axis=1)
        hb = (_silu(_dot(x, wgb_ref[...])) * _dot(x, wub_ref[...])).astype(BF16)
        _store_slabs(o_ref, _pack_pairs(_dot(hb, wdb_ref[...])), rows, nc)

    @pl.when(jnp.logical_and(used, jnp.logical_not(short)))
    def _():
        ffn(EXPERT_BLOCK)

    @pl.when(jnp.logical_and(used, short))
    def _():
        ffn(half)
        o_ref[half * nc:, :] = jnp.zeros((half * nc, LANES), o_ref.dtype)

    @pl.when(jnp.logical_not(used))
    def _():
        o_ref[...] = jnp.zeros_like(o_ref)


def _expert_ffn(xs, block_e, n_used, block_ord, block_next, n_valid, w_gate, w_up, w_down, layer, nc):
    _, _, d, f = w_gate.shape
    nb = xs.shape[0] // (EXPERT_BLOCK * nc)
    any_space = pl.BlockSpec(memory_space=pl.ANY)
    grid_spec = pltpu.PrefetchScalarGridSpec(
        num_scalar_prefetch=5,
        grid=(nb,),
        in_specs=[
            pl.BlockSpec((EXPERT_BLOCK * nc, LANES),
                         lambda i, be, nu, od, nx, nv: (jnp.minimum(i, nu[0] - 1), 0)),
            any_space, any_space, any_space,
        ],
        out_specs=pl.BlockSpec((EXPERT_BLOCK * nc, LANES), lambda i, be, nu, od, nx, nv: (i, 0)),
        scratch_shapes=[pltpu.VMEM((2, d, f), F32), pltpu.VMEM((2, d, f), F32),
                        pltpu.VMEM((2, f, d), F32),
                        pltpu.VMEM((d, f), BF16), pltpu.VMEM((d, f), BF16), pltpu.VMEM((f, d), BF16),
                        pltpu.SemaphoreType.DMA((2,))],
    )
    return pl.pallas_call(
        functools.partial(_expert_ffn_kernel, nc=nc, layer=layer),
        grid_spec=grid_spec,
        out_shape=jax.ShapeDtypeStruct(xs.shape, xs.dtype),
        compiler_params=_cparams(1),
        name="expert_ffn",
    )(block_e, n_used, block_ord, block_next, n_valid, xs, w_gate, w_up, w_down)


def _combine_kernel(dest_ref, destn_ref, wts_ref, hf_ref, hb_ref, sg_ref, su_ref, sd_ref, g_ref,
                    b_ref, ys_ref, of_ref, ob_ref, buf_ref, sems, *, tm, nc, alpha):
    i = pl.program_id(0)
    slot = i % 2

    def slab_copy(sl, k, r, src):
        dst_row = pl.multiple_of((k * tm + r) * nc, nc)
        return pltpu.make_async_copy(ys_ref.at[pl.ds(pl.multiple_of(src * nc, nc), nc), :],
                                     buf_ref.at[sl, pl.ds(dst_row, nc), :], sems.at[sl])

    def gather(sl, rows_ref):
        def issue(j, carry):
            for r in (2 * j, 2 * j + 1):
                for k in range(TOP_K):
                    slab_copy(sl, k, r, rows_ref[0, 0, r * TOP_K + k]).start(priority=k % 2)
            return carry

        lax.fori_loop(0, tm // 2, issue, 0)

    @pl.when(i == 0)
    def _():
        gather(slot, dest_ref)

    @pl.when(i + 1 < pl.num_programs(0))
    def _():
        gather(1 - slot, destn_ref)

    xb = hb_ref[...]
    shared = _dot((_silu(_dot(xb, sg_ref[...])) * _dot(xb, su_ref[...])).astype(BF16), sd_ref[...])
    z = alpha * hf_ref[...] + shared

    pltpu.make_async_copy(buf_ref.at[slot], buf_ref.at[slot], sems.at[slot]).wait()
    wts = wts_ref[...]
    his = [None] * nc
    los = [None] * nc
    for k in range(TOP_K):
        wk = wts[:, k:k + 1]
        for s in range(nc):
            hi, lo = _unpack_pairs(buf_ref[slot, pl.ds(k * tm * nc + s, tm, stride=nc), :])
            his[s] = wk * hi if k == 0 else his[s] + wk * hi
            los[s] = wk * lo if k == 0 else los[s] + wk * lo
    out = _layer_norm(z + jnp.concatenate(his + los, axis=1), g_ref[...], b_ref[...])
    of_ref[...] = out
    ob_ref[...] = out.astype(BF16)


def _combine(ys, dest, wts, h_f32, h_bf16, sh_gate, sh_up, sh_down, g, b, alpha, nc, tm=256):
    t, d = h_f32.shape
    f = sh_gate.shape[1]
    tm = min(tm, t)
    n = t // tm
    smem = lambda im: pl.BlockSpec((1, 1, tm * TOP_K), im, memory_space=pltpu.SMEM)
    tile = lambda w: pl.BlockSpec((tm, w), lambda i: (i, 0))
    const = lambda shape: pl.BlockSpec(shape, lambda i: (0, 0))
    rows = dest.reshape(n, 1, tm * TOP_K)
    return pl.pallas_call(
        functools.partial(_combine_kernel, tm=tm, nc=nc, alpha=alpha),
        grid=(n,),
        in_specs=[smem(lambda i: (i, 0, 0)), smem(lambda i: (jnp.minimum(i + 1, n - 1), 0, 0)),
                  tile(LANES), tile(d), tile(d),
                  const((d, f)), const((d, f)), const((f, d)), const((1, d)), const((1, d)),
                  pl.BlockSpec(memory_space=pl.ANY)],
        out_specs=[tile(d), tile(d)],
        out_shape=[jax.ShapeDtypeStruct((t, d), F32), jax.ShapeDtypeStruct((t, d), BF16)],
        scratch_shapes=[pltpu.VMEM((2, TOP_K * tm * nc, LANES), ys.dtype),
                        pltpu.SemaphoreType.DMA((2,))],
        compiler_params=_cparams(1),
        name="moe_combine",
    )(rows, rows, wts, h_f32, h_bf16, sh_gate, sh_up, sh_down, g.reshape(1, d), b.reshape(1, d), ys)


def _moe_ln(h_f32, h_bf16, h_slabs, router_w, router_bias, w_gate, w_up, w_down, layer, sh_gate,
            sh_up, sh_down, g, b, alpha):
    t, d = h_f32.shape
    n_exp = router_w.shape[1]
    nc = _slab_rows(d)
    idx, wts, rank, counts = _router(h_f32, router_w, router_bias)
    idx, rank = idx[:, :TOP_K], rank[:, :TOP_K]

    counts = counts[0, :n_exp].astype(jnp.int32)
    padded = (counts + EXPERT_BLOCK - 1) // EXPERT_BLOCK * EXPERT_BLOCK
    pends = jnp.cumsum(padded)
    pstart = jnp.concatenate([jnp.zeros((1,), jnp.int32), pends]).astype(jnp.int32)
    n_blocks = t * TOP_K // EXPERT_BLOCK + n_exp
    block_start = jnp.arange(n_blocks, dtype=jnp.int32) * EXPERT_BLOCK
    block_e = jnp.sum((pends[None, :] <= block_start[:, None]).astype(jnp.int32), axis=1)
    block_e = jnp.minimum(block_e, n_exp - 1)
    n_used = (pends[-1:] // EXPERT_BLOCK).astype(jnp.int32)
    experts = jnp.arange(n_exp, dtype=jnp.int32)
    owns = padded > 0
    order_e = jnp.cumsum(owns.astype(jnp.int32)) - 1
    later = jnp.logical_and(owns[None, :], experts[None, :] > experts[:, None])
    next_e = jnp.min(jnp.where(later, experts[None, :], n_exp), axis=1)
    next_e = jnp.where(next_e < n_exp, next_e, -1)
    of_block = (block_e[:, None] == experts[None, :]).astype(jnp.int32)
    block_ord = jnp.sum(of_block * order_e[None, :], axis=1)
    block_next = jnp.sum(of_block * next_e[None, :], axis=1)
    rows_left = counts[None, :] - (block_start[:, None] - pstart[None, :n_exp])
    n_valid = jnp.clip(jnp.sum(of_block * rows_left, axis=1), 0, EXPERT_BLOCK)
    group_start = jnp.sum(jnp.where(idx[:, :, None] == jnp.arange(n_exp, dtype=jnp.int32),
                                    pstart[:n_exp], 0), axis=-1)
    dest = group_start + rank

    xs = _dispatch(h_slabs, dest, pstart, counts, n_blocks * EXPERT_BLOCK, nc)
    ys = _expert_ffn(xs, block_e, n_used, block_ord, block_next, n_valid, w_gate, w_up, w_down, layer,
                     nc)
    return _combine(ys, dest, wts, h_f32, h_bf16, sh_gate.astype(BF16), sh_up.astype(BF16),
                    sh_down.astype(BF16), g, b, alpha, nc)


def kernel(x, even_w_in, ret_gn_g, rg_conv_w, rg_conv_b, rg_w_a, rg_b_a, rg_w_i, rg_b_i, rg_lambda,
           even_w_out, fox_w_in, fox_b_f, fox_w_out, ln_g, ln_b, router_w, router_bias, exp_w_gate,
           exp_w_up, exp_w_down, sh_w_gate, sh_w_up, sh_w_down):
    batch, seq, d = x.shape
    depth = ln_g.shape[0]
    alpha = (2 * depth) ** 0.25
    ret_width = ret_gn_g.shape[1]
    fox_heads = fox_b_f.shape[1]

    h_f32 = x.reshape(batch * seq, d)
    h_bf16 = h_f32
    for layer in range(depth):
        j = layer // 2
        if layer % 2 == 0:
            proj = _matmul(h_bf16, even_w_in[j].astype(BF16), BF16)
            y_ret = _retention(proj, ret_gn_g[j], batch, seq, chunk=256)
            y_rnn = _rglru(proj, rg_conv_w[j], rg_conv_b[j], rg_w_a[j], rg_b_a[j], rg_w_i[j],
                           rg_b_i[j], rg_lambda[j], batch, seq, tt=256)
            w_out = even_w_out[j].astype(BF16)
            ys, ws = [y_ret, y_rnn], [w_out[:ret_width], w_out[ret_width:]]
        else:
            w_in = fox_w_in[j]
            qkv = _matmul(h_bf16, w_in[:, :3 * d].astype(BF16), BF16)
            w_f = jnp.pad(w_in[:, 3 * d:], ((0, 0), (0, LANES - fox_heads))).astype(BF16)
            fl = _matmul(h_bf16, w_f, F32)
            b_f = jnp.pad(fox_b_f[j], (0, LANES - fox_heads)).reshape(1, LANES)
            kc = _forget_key_bias(_logf_cumsum(fl, b_f, batch, seq), batch, seq, fox_heads)
            o = _fox_attention(qkv, kc, batch, seq, fox_heads, tq=512)
            ys, ws = [o], [fox_w_out[j].astype(BF16)]
        h_f32, h_bf16, h_slabs = _proj_ln(ys, ws, h_f32, ln_g[layer, 0], ln_b[layer, 0], alpha)
        h_f32, h_bf16 = _moe_ln(h_f32, h_bf16, h_slabs, router_w[layer], router_bias[layer],
                                exp_w_gate, exp_w_up, exp_w_down, layer,
                                sh_w_gate[layer], sh_w_up[layer], sh_w_down[layer],
                                ln_g[layer, 1], ln_b[layer, 1], alpha)
    return h_f32.reshape(batch, seq, d)
```
